```python
import math
import jax, jax.numpy as jnp
from jax import lax
import numpy as np

D_MODEL = 4096
BATCH = 4
SEQ = 2048
DEPTH = 2
DEC_BATCH = 128
DEC_SEQ = 8
PAST_LEN = 16384
PAGE_SIZE = 128

HEAD_DIM = D_MODEL // 32
H_A = 8
H_B = 8
H_C = 8
W_A = H_A * HEAD_DIM
W_B = H_B * HEAD_DIM
W_C = H_C * HEAD_DIM
CHUNK = 64
CONV_W = 4
N_EXPERTS = 16
N_GROUPS = 4
EXPERTS_PER_GROUP = N_EXPERTS // N_GROUPS
TOP_K = 2
D_FF_EXPERT = D_MODEL // 4
ALPHA = (2 * DEPTH) ** 0.25
BETA_INIT = (8 * DEPTH) ** -0.25
LN_EPS = 1e-5
NORM_EPS = 1e-6
LB_TINY = 1e-20
NEG_BIG = -1e30

kernel_name = 'hybrid_hgrn2_mlstm_gdn_moe_step'


def _split_sizes():
    return [W_A] * 4 + [W_B] * 4 + [H_B] * 2 + [W_C] * 4 + [H_C] * 2


def _split_points():
    return [int(v) for v in np.cumsum(_split_sizes())[:-1]]


def _layer_norm(x, g, b):
    xf = x.astype(jnp.float32)
    mu = jnp.mean(xf, -1, keepdims=True)
    var = jnp.mean(jnp.square(xf - mu), -1, keepdims=True)
    return ((xf - mu) * lax.rsqrt(var + LN_EPS) * g + b).astype(x.dtype)


def _rms_norm(x, g):
    return x * lax.rsqrt(jnp.mean(jnp.square(x), -1, keepdims=True) + NORM_EPS) * g


def _l2norm(x):
    return x * lax.rsqrt(jnp.sum(jnp.square(x), -1, keepdims=True) + NORM_EPS)


def _to_chunks(t, cn):
    B, L = t.shape[:2]
    t = t.reshape((B, L // cn, cn) + t.shape[2:])
    return jnp.moveaxis(jnp.moveaxis(t, 1, 0), 2, 3)


def _from_chunks(t):
    NC, B, H, cn, d = t.shape
    return t.transpose(1, 0, 3, 2, 4).reshape(B, NC * cn, H, d)


def _hgrn2_chunked(q, k, v, log_f, S0):
    cn = math.gcd(q.shape[1], CHUNK)
    incl = jnp.tril(jnp.ones((cn, cn), bool))[:, :, None]

    def step(S, xs):
        qc, kc, vc, gc = xs
        b = jnp.cumsum(gc, axis=2)
        diff = b[:, :, :, None, :] - b[:, :, None, :, :]
        dec = jnp.where(incl, jnp.exp(jnp.where(incl, diff, 0.0)), 0.0)
        att = jnp.einsum('bhtk,bhsk,bhtsk->bhts', qc, kc, dec)
        o = jnp.einsum('bhts,bhsv->bhtv', att, vc) + jnp.einsum('bhtk,bhkv->bhtv', qc * jnp.exp(b), S)
        bL = b[:, :, -1:, :]
        S = jnp.exp(bL[:, :, 0, :, None]) * S + jnp.einsum('bhsk,bhsv->bhkv', kc * jnp.exp(bL - b), vc)
        return S, o

    S, o = lax.scan(step, S0.astype(jnp.float32),
                    (_to_chunks(q, cn), _to_chunks(k, cn), _to_chunks(v, cn), _to_chunks(log_f, cn)))
    return _from_chunks(o), S


def _mlstm_chunked(q, k, v, log_i, log_f, C0, n0, m0):
    cn = math.gcd(q.shape[1], CHUNK)
    incl = jnp.tril(jnp.ones((cn, cn), bool))

    def step(carry, xs):
        C, n, m = carry
        qc, kc, vc, ic, fc = xs
        F = jnp.cumsum(fc, axis=-1)
        raw = F[..., :, None] - F[..., None, :] + ic[..., None, :]
        logD = jnp.where(incl, raw, NEG_BIG)
        inter = F + m[..., None]
        m_t = jnp.maximum(inter, jnp.max(logD, axis=-1))
        w_inter = jnp.exp(inter - m_t)
        s = jnp.where(incl, jnp.einsum('bhtk,bhsk->bhts', qc, kc)
                      * jnp.exp(jnp.where(incl, raw - m_t[..., None], 0.0)), 0.0)
        num = jnp.einsum('bhts,bhsv->bhtv', s, vc) + w_inter[..., None] * jnp.einsum('bhtk,bhkv->bhtv', qc, C)
        den = jnp.sum(s, -1) + w_inter * jnp.einsum('bhtk,bhk->bht', qc, n)
        h = num / jnp.maximum(jnp.abs(den), jnp.exp(-m_t))[..., None]
        m_new = m_t[..., -1]
        wC = jnp.exp(inter[..., -1] - m_new)
        wS = jnp.exp(F[..., -1:] - F + ic - m_new[..., None])
        C = wC[..., None, None] * C + jnp.einsum('bhsk,bhsv->bhkv', kc * wS[..., None], vc)
        n = wC[..., None] * n + jnp.einsum('bhsk,bhs->bhk', kc, wS)
        return (C, n, m_new), h

    carry0 = (C0.astype(jnp.float32), n0.astype(jnp.float32), m0.astype(jnp.float32))
    (C, n, m), h = lax.scan(step, carry0,
                            (_to_chunks(q, cn), _to_chunks(k, cn), _to_chunks(v, cn),
                             _to_chunks(log_i, cn), _to_chunks(log_f, cn)))
    return _from_chunks(h), C, n, m


def _gdn_chunked(q, k, v, g, beta, S0):
    cn = math.gcd(q.shape[1], CHUNK)
    dv = v.shape[-1]
    incl = jnp.tril(jnp.ones((cn, cn), bool))
    strict = jnp.tril(jnp.ones((cn, cn), bool), -1)
    eye = jnp.eye(cn, dtype=jnp.float32)

    def step(S, xs):
        qc, kc, vc, gc, bc = xs
        gam = jnp.cumsum(gc, axis=-1)
        diff = gam[..., :, None] - gam[..., None, :]
        A = jnp.where(strict, bc[..., :, None] * jnp.einsum('bhtk,bhsk->bhts', kc, kc)
                      * jnp.exp(jnp.where(strict, diff, 0.0)), 0.0)
        rhs = jnp.concatenate([bc[..., None] * vc, (bc * jnp.exp(gam))[..., None] * kc], axis=-1)
        sol = lax.linalg.triangular_solve(eye + A, rhs, left_side=True, lower=True, unit_diagonal=True)
        u = sol[..., :dv] - jnp.einsum('bhtk,bhkv->bhtv', sol[..., dv:], S)
        qk = jnp.where(incl, jnp.einsum('bhtk,bhsk->bhts', qc, kc) * jnp.exp(jnp.where(incl, diff, 0.0)), 0.0)
        o = jnp.einsum('bhts,bhsv->bhtv', qk, u) + jnp.einsum('bhtk,bhkv->bhtv', qc * jnp.exp(gam)[..., None], S)
        S = jnp.exp(gam[..., -1])[..., None, None] * S + jnp.einsum(
            'bhsk,bhsv->bhkv', kc * jnp.exp(gam[..., -1:] - gam)[..., None], u)
        return S, o

    S, o = lax.scan(step, S0.astype(jnp.float32),
                    (_to_chunks(q, cn), _to_chunks(k, cn), _to_chunks(v, cn),
                     _to_chunks(g, cn), _to_chunks(beta, cn)))
    return _from_chunks(o), S


def _causal_conv(x, buf, w):
    L = x.shape[1]
    xp = jnp.concatenate([buf, x], axis=1)
    y = sum(xp[:, j:j + L] * w[:, j] for j in range(CONV_W))
    return y, xp[:, -(CONV_W - 1):]


def _mixer(h, st, lb, lw):
    S_a, C_b, n_b, m_b, S_c, buf_c = st
    B, L, _ = h.shape
    f32 = lambda t: t.astype(jnp.float32)
    heads = lambda t: f32(t).reshape(B, L, -1, HEAD_DIM)
    proj = h @ lw['w_in'] + lw['b_in']
    (qa, fa, ia, ga, qb, kb, vb, ob, ib, fb,
     qc, kc, vc, gc, ac, bc) = jnp.split(proj, _split_points(), axis=-1)

    lb_h = lb.reshape(H_A, HEAD_DIM)
    fa = heads(fa)
    log_f = jnp.logaddexp(jnp.log(jnp.maximum(lb_h, LB_TINY)), jnp.log1p(-lb_h) + jax.nn.log_sigmoid(fa))
    k_a = (1.0 - lb_h) * jax.nn.sigmoid(-fa)
    o_a, S_a = _hgrn2_chunked(jax.nn.silu(heads(qa)), k_a, heads(ia), log_f, S_a)
    y_a = _rms_norm(o_a, lw['hgrn_norm_g']) * jax.nn.sigmoid(heads(ga))

    o_b, C_b, n_b, m_b = _mlstm_chunked(heads(qb), heads(kb) * HEAD_DIM ** -0.5, heads(vb),
                                        f32(ib), jax.nn.log_sigmoid(f32(fb)), C_b, n_b, m_b)
    y_b = jax.nn.sigmoid(heads(ob)) * o_b

    qkv, buf_c = _causal_conv(f32(jnp.concatenate([qc, kc, vc], axis=-1)), f32(buf_c), f32(lw['gdn_conv_w']))
    q_c, k_c, v_c = jnp.split(jax.nn.silu(qkv), 3, axis=-1)
    g_c = -jnp.exp(f32(lw['gdn_A_log'])) * jax.nn.softplus(f32(ac) + lw['gdn_dt_bias'])
    o_c, S_c = _gdn_chunked(_l2norm(heads(q_c)) * HEAD_DIM ** -0.5, _l2norm(heads(k_c)), heads(v_c),
                            g_c, jax.nn.sigmoid(f32(bc)), S_c)
    y_c = _rms_norm(o_c, lw['gdn_norm_g']) * jax.nn.silu(heads(gc))

    merged = sum(jax.nn.sigmoid(h @ lw['w_merge'][j] + lw['b_merge'][j])
                 * (y.reshape(B, L, -1).astype(h.dtype) @ lw['w_branch'][j])
                 for j, y in enumerate((y_a, y_b, y_c)))
    return merged @ lw['w_out'], (S_a, C_b, n_b, m_b, S_c, buf_c)


def _moe(h, w_router, b_router, wg, wu, wd):
    B, L, D = h.shape
    t = h.reshape(-1, D)
    probs = jax.nn.softmax((t @ w_router).astype(jnp.float32) + b_router, axis=-1)
    gscore = jnp.sum(lax.top_k(probs.reshape(-1, N_GROUPS, EXPERTS_PER_GROUP), TOP_K)[0], -1)
    gsel = jnp.argmax(gscore, axis=-1)
    in_group = (jnp.arange(N_EXPERTS) // EXPERTS_PER_GROUP)[None, :] == gsel[:, None]
    topv, topi = lax.top_k(jnp.where(in_group, probs, -1.0), TOP_K)
    topw = topv / jnp.sum(topv, -1, keepdims=True)
    comb = jnp.sum(jax.nn.one_hot(topi, N_EXPERTS, dtype=jnp.float32) * topw[..., None], axis=1)
    y = jnp.zeros(t.shape, jnp.float32)
    for e in range(N_EXPERTS):
        he = jax.nn.silu(t @ wg[e]) * (t @ wu[e])
        y = y + comb[:, e:e + 1] * (he @ wd[e])
    return y.reshape(B, L, D).astype(h.dtype)


def _layer(x, c, st, lb, lw, w_router, b_router):
    mod = (jax.nn.silu(c) @ lw['w_ada'] + lw['b_ada'])[:, None, :]
    sh1, sc1, gt1, sh2, sc2, gt2 = jnp.split(mod, 6, axis=-1)
    mix, st = _mixer(x * (1 + sc1) + sh1, st, lb, lw)
    x = _layer_norm(ALPHA * x + gt1 * mix, lw['ln_mix_g'], lw['ln_mix_b'])
    ffn = _moe(x * (1 + sc2) + sh2, w_router, b_router, lw['w_exp_gate'], lw['w_exp_up'], lw['w_exp_down'])
    x = _layer_norm(ALPHA * x + gt2 * ffn, lw['ln_ffn_g'], lw['ln_ffn_b'])
    return x, st


def _zero_state(b):
    z = lambda *s: jnp.zeros(s, jnp.float32)
    return (z(b, H_A, HEAD_DIM, HEAD_DIM), z(b, H_B, HEAD_DIM, HEAD_DIM), z(b, H_B, HEAD_DIM),
            z(b, H_B), z(b, H_C, HEAD_DIM, HEAD_DIM), z(b, CONV_W - 1, 3 * W_C))


def setup_inputs(seed: int = 0) -> dict:
    key = jax.random.key(seed)
    ks = iter(jax.random.split(key, 48))
    nrm = lambda shape, s: jax.random.normal(next(ks), shape, jnp.float32) * s
    D = D_MODEL
    sizes = _split_sizes()
    n_in = sum(sizes)
    f_off = sum(sizes[:9])
    b_in = nrm((DEPTH, n_in), 0.02).at[:, f_off:f_off + H_B].add(jnp.linspace(3.0, 6.0, H_B))
    dt = jnp.exp(jax.random.uniform(next(ks), (DEPTH, H_C), jnp.float32, math.log(1e-3), math.log(1e-1)))
    return {
        'x_prompt': nrm((BATCH, SEQ, D), 1.0),
        'x_sample': nrm((DEC_BATCH, DEC_SEQ, D), 1.0),
        'state_hgrn': nrm((DEPTH, DEC_BATCH, H_A, HEAD_DIM, HEAD_DIM), 0.5),
        'state_mlstm_C': nrm((DEPTH, DEC_BATCH, H_B, HEAD_DIM, HEAD_DIM), 0.3),
        'state_mlstm_n': nrm((DEPTH, DEC_BATCH, H_B, HEAD_DIM), 0.3),
        'state_mlstm_m': nrm((DEPTH, DEC_BATCH, H_B), 1.0),
        'state_gdn': nrm((DEPTH, DEC_BATCH, H_C, HEAD_DIM, HEAD_DIM), 0.3),
        'state_conv': nrm((DEPTH, DEC_BATCH, CONV_W - 1, 3 * W_C), 1.0),
        'c_prompt': nrm((BATCH, D), 1.0),
        'c_sample': nrm((DEC_BATCH, D), 1.0),
        'w_in': nrm((DEPTH, D, n_in), D ** -0.5),
        'b_in': b_in,
        'hgrn_lb_logits': nrm((DEPTH, W_A), 0.1),
        'hgrn_norm_g': 1.0 + nrm((DEPTH, HEAD_DIM), 0.02),
        'gdn_conv_w': nrm((DEPTH, 3 * W_C, CONV_W), CONV_W ** -0.5),
        'gdn_A_log': jnp.log(jax.random.uniform(next(ks), (DEPTH, H_C), jnp.float32, 1.0, 16.0)),
        'gdn_dt_bias': dt + jnp.log(-jnp.expm1(-dt)),
        'gdn_norm_g': 1.0 + nrm((DEPTH, HEAD_DIM), 0.02),
        'w_branch': nrm((DEPTH, 3, W_A, D), W_A ** -0.5 * BETA_INIT),
        'w_merge': nrm((DEPTH, 3, D, D), D ** -0.5),
        'b_merge': nrm((DEPTH, 3, D), 0.02),
        'w_out': nrm((DEPTH, D, D), D ** -0.5 * BETA_INIT),
        'ln_mix_g': 1.0 + nrm((DEPTH, D), 0.02),
        'ln_mix_b': nrm((DEPTH, D), 0.02),
        'w_ada': nrm((DEPTH, D, 6 * D), 0.5 * D ** -0.5),
        'b_ada': nrm((DEPTH, 6 * D), 0.02),
        'w_router': nrm((D, N_EXPERTS), D ** -0.5),
        'b_router': nrm((N_EXPERTS,), 0.01),
        'w_exp_gate': nrm((DEPTH, N_EXPERTS, D, D_FF_EXPERT), D ** -0.5),
        'w_exp_up': nrm((DEPTH, N_EXPERTS, D, D_FF_EXPERT), D ** -0.5 * BETA_INIT),
        'w_exp_down': nrm((DEPTH, N_EXPERTS, D_FF_EXPERT, D), D_FF_EXPERT ** -0.5 * BETA_INIT),
        'ln_ffn_g': 1.0 + nrm((DEPTH, D), 0.02),
        'ln_ffn_b': nrm((DEPTH, D), 0.02),
    }


def reference(x_prompt, x_sample, state_hgrn, state_mlstm_C, state_mlstm_n, state_mlstm_m, state_gdn,
              state_conv, c_prompt, c_sample, w_in, b_in, hgrn_lb_logits, hgrn_norm_g, gdn_conv_w,
              gdn_A_log, gdn_dt_bias, gdn_norm_g, w_branch, w_merge, b_merge, w_out, ln_mix_g, ln_mix_b,
              w_ada, b_ada, w_router, b_router, w_exp_gate, w_exp_up, w_exp_down, ln_ffn_g, ln_ffn_b):
    p = jax.nn.softmax(hgrn_lb_logits.astype(jnp.float32), axis=0)
    lower_bounds = jnp.cumsum(p, axis=0) - p[0]
    layer_w = [dict(w_in=w_in[l], b_in=b_in[l], hgrn_norm_g=hgrn_norm_g[l], gdn_conv_w=gdn_conv_w[l],
                    gdn_A_log=gdn_A_log[l], gdn_dt_bias=gdn_dt_bias[l], gdn_norm_g=gdn_norm_g[l],
                    w_branch=w_branch[l], w_merge=w_merge[l], b_merge=b_merge[l], w_out=w_out[l],
                    ln_mix_g=ln_mix_g[l], ln_mix_b=ln_mix_b[l], w_ada=w_ada[l], b_ada=b_ada[l],
                    w_exp_gate=w_exp_gate[l], w_exp_up=w_exp_up[l], w_exp_down=w_exp_down[l],
                    ln_ffn_g=ln_ffn_g[l], ln_ffn_b=ln_ffn_b[l]) for l in range(DEPTH)]

    def run(x, c, states):
        new = []
        for l in range(DEPTH):
            x, st = _layer(x, c, states[l], lower_bounds[l], layer_w[l], w_router, b_router)
            new.append(st)
        return x, [jnp.stack(s, axis=0) for s in zip(*new)]

    y_prompt, new_p = run(x_prompt, c_prompt, [_zero_state(x_prompt.shape[0]) for _ in range(DEPTH)])
    y_sample, new_s = run(x_sample, c_sample,
                          [(state_hgrn[l], state_mlstm_C[l], state_mlstm_n[l], state_mlstm_m[l],
                            state_gdn[l], state_conv[l]) for l in range(DEPTH)])
    hgrn_p, mlstm_C_p, mlstm_n_p, mlstm_m_p, gdn_p, conv_p = new_p
    hgrn_s, mlstm_C_s, mlstm_n_s, mlstm_m_s, gdn_s, conv_s = new_s
    return (y_prompt, y_sample, hgrn_p, hgrn_s, mlstm_C_p, mlstm_C_s, mlstm_n_p, mlstm_n_s,
            mlstm_m_p, mlstm_m_s, gdn_p, gdn_s, conv_p, conv_s)
```

```python
import functools
import math

import numpy as np
import jax
import jax.numpy as jnp
from jax import lax
from jax.experimental import pallas as pl
from jax.experimental.pallas import tpu as pltpu

F32 = jnp.float32
BF16 = jnp.bfloat16
HIGHEST = lax.Precision.HIGHEST

HEAD_DIM = 128
N_HEADS = 8
WIDTH = N_HEADS * HEAD_DIM
CHUNK = 64
CONV_W = 4
N_EXPERTS = 16
N_GROUPS = 4
EXPERTS_PER_GROUP = N_EXPERTS // N_GROUPS
LN_EPS = 1e-5
NORM_EPS = 1e-6
LB_TINY = 1e-20
NEG_BIG = -1e30
LANES = 128
VMEM_LIMIT = 56 * 1024 * 1024


def _params(*sem):
    return pltpu.CompilerParams(dimension_semantics=sem, vmem_limit_bytes=VMEM_LIMIT)


def _dot(a, b):
    return jnp.dot(a.astype(BF16), b.astype(BF16), preferred_element_type=F32)


def _dot_nt(a, b):
    return lax.dot_general(a.astype(BF16), b.astype(BF16), (((1,), (1,)), ((), ())),
                           preferred_element_type=F32)


def _dot_tn(a, b):
    return lax.dot_general(a.astype(BF16), b.astype(BF16), (((0,), (0,)), ((), ())),
                           preferred_element_type=F32)


def _dot_hi(a, b):
    return jnp.dot(a, b, precision=HIGHEST, preferred_element_type=F32)


def _dot_nt_hi(a, b):
    return lax.dot_general(a, b, (((1,), (1,)), ((), ())), precision=HIGHEST,
                           preferred_element_type=F32)


def _dot_tn_hi(a, b):
    return lax.dot_general(a, b, (((0,), (0,)), ((), ())), precision=HIGHEST,
                           preferred_element_type=F32)


def _sigmoid(x):
    return jax.nn.sigmoid(x)


def _silu(x):
    return x * jax.nn.sigmoid(x)


def _log_sigmoid(x):
    return jnp.minimum(x, 0.0) - jnp.log1p(jnp.exp(-jnp.abs(x)))


def _softplus(x):
    return jnp.maximum(x, 0.0) + jnp.log1p(jnp.exp(-jnp.abs(x)))


def _logaddexp(a, b):
    return jnp.maximum(a, b) + jnp.log1p(jnp.exp(-jnp.abs(a - b)))


def _row_form(col, sel):
    cn = col.shape[0]
    return _dot_nt_hi(sel, jnp.broadcast_to(col, (cn, LANES)))


def _tri_masks(cn):
    r = lax.broadcasted_iota(jnp.int32, (cn, cn), 0)
    c = lax.broadcasted_iota(jnp.int32, (cn, cn), 1)
    return r >= c, r > c, r == c


def _lane0_onehot(cn):
    return (lax.broadcasted_iota(jnp.int32, (cn, LANES), 1) == 0).astype(F32)


def _mm_kernel(x_ref, w_ref, b_ref, o_ref):
    acc = jnp.dot(x_ref[...].astype(BF16), w_ref[...].astype(BF16), preferred_element_type=F32)
    o_ref[...] = (acc + b_ref[...]).astype(o_ref.dtype)


def _matmul(x, w, b, out_dtype, tm, tn, name):
    M, K = x.shape
    N = w.shape[1]
    tm = min(tm, M)
    assert M % tm == 0 and N % tn == 0, (M, tm, N, tn)
    return pl.pallas_call(
        _mm_kernel,
        grid=(M // tm, N // tn),
        in_specs=[pl.BlockSpec((tm, K), lambda i, j: (i, 0)),
                  pl.BlockSpec((K, tn), lambda i, j: (0, j)),
                  pl.BlockSpec((1, tn), lambda i, j: (0, j))],
        out_specs=pl.BlockSpec((tm, tn), lambda i, j: (i, j)),
        out_shape=jax.ShapeDtypeStruct((M, N), out_dtype),
        compiler_params=_params("parallel", "arbitrary"),
        name=name,
    )(x, w, b.reshape(1, N).astype(F32))


def _modulate_kernel(x_ref, mod_ref, o_ref, *, sh_idx, sc_idx):
    m = mod_ref[...]
    o_ref[...] = (x_ref[...] * (1.0 + m[:, sc_idx:sc_idx + 1, :]) + m[:, sh_idx:sh_idx + 1, :]).astype(o_ref.dtype)


def _seq_tiles(B, L):
    if L >= 256:
        return 1, 256
    return max(1, 256 // L), L


def _modulate(x, mod, sh_idx, sc_idx):
    B, L, D = x.shape
    bb, tl = _seq_tiles(B, L)
    return pl.pallas_call(
        functools.partial(_modulate_kernel, sh_idx=sh_idx, sc_idx=sc_idx),
        grid=(B // bb, L // tl),
        in_specs=[pl.BlockSpec((bb, tl, D), lambda i, j: (i, j, 0)),
                  pl.BlockSpec((bb, 6, D), lambda i, j: (i, 0, 0))],
        out_specs=pl.BlockSpec((bb, tl, D), lambda i, j: (i, j, 0)),
        out_shape=jax.ShapeDtypeStruct((B, L, D), BF16),
        compiler_params=_params("parallel", "arbitrary"),
        name="modulate",
    )(x, mod)


def _hgrn_tables(cn):
    nlev = int(math.log2(cn))
    assert 1 << nlev == cn
    r = np.arange(cn)
    mats = [np.tril(np.ones((cn, cn)))]
    masks = [np.eye(cn)]
    j = r[None, :]
    for lev in range(nlev):
        s = cn >> (lev + 1)
        blk = 2 * s
        pos = r % blk
        ref = r - pos + s - 1
        second = pos >= s
        m_second = (j > ref[:, None]) & (j <= r[:, None])
        m_first = (j > r[:, None]) & (j <= ref[:, None])
        mats.append(np.where(second[:, None], m_second, m_first))
        same = (r[:, None] // blk) == (r[None, :] // blk)
        masks.append(same & second[:, None] & (~second)[None, :])
    return (np.concatenate(mats, 0).astype(np.float32), np.stack(masks).astype(np.float32))


def _hgrn_kernel(*refs, layer, cn, nchunk, has_state):
    if has_state:
        (lbl_ref, ng_ref, mall_ref, masks_ref, q_ref, f_ref, i_ref, g_ref, s0_ref, y_ref, s_ref) = refs
    else:
        (lbl_ref, ng_ref, mall_ref, masks_ref, q_ref, f_ref, i_ref, g_ref, y_ref, s_ref) = refs
        s0_ref = None
    j = pl.program_id(1)
    nlev = masks_ref.shape[0] - 1

    @pl.when(j == 0)
    def _():
        if has_state:
            s_ref[...] = s0_ref[...]
        else:
            s_ref[...] = jnp.zeros(s_ref.shape, F32)

    logits = lbl_ref[...]
    e = jnp.exp(logits - jnp.max(logits, axis=0, keepdims=True))
    p = e / jnp.sum(e, axis=0, keepdims=True)
    lb_all = jnp.zeros((1, WIDTH), F32)
    for l in range(1, layer + 1):
        lb_all = lb_all + p[l:l + 1, :]
    ng = ng_ref[...]
    mall = mall_ref[...]
    ones_cn = jnp.ones((cn, HEAD_DIM), F32)

    def chunk(c, carry):
        r0 = pl.multiple_of(c * cn, cn)
        for h in range(N_HEADS):
            cs = slice(h * HEAD_DIM, (h + 1) * HEAD_DIM)
            lb = lb_all[:, cs]
            q = _silu(q_ref[pl.ds(r0, cn), cs])
            fr = f_ref[pl.ds(r0, cn), cs]
            v = i_ref[pl.ds(r0, cn), cs]
            log_f = _logaddexp(jnp.log(jnp.maximum(lb, LB_TINY)), jnp.log1p(-lb) + _log_sigmoid(fr))
            k = (1.0 - lb) * _sigmoid(-fr)
            dall = _dot_hi(mall, log_f)
            b = dall[:cn]
            att = masks_ref[0] * _dot_nt(q, k)
            for lev in range(1, nlev + 1):
                ex = jnp.exp(dall[lev * cn:(lev + 1) * cn])
                att = att + masks_ref[lev] * _dot_nt(q * ex, k * ex)
            S = s_ref[0, h]
            o = _dot(att, v) + _dot(q * jnp.exp(b), S)
            b_last = b[cn - 1:cn]
            k_dec = k * jnp.exp(b_last - b)
            dec_col = jnp.exp(_dot_tn_hi(log_f, ones_cn))
            s_ref[0, h] = dec_col * S + _dot_tn(k_dec, v)
            yn = o * lax.rsqrt(jnp.mean(o * o, axis=-1, keepdims=True) + NORM_EPS) * ng
            y_ref[pl.ds(r0, cn), cs] = (yn * _sigmoid(g_ref[pl.ds(r0, cn), cs])).astype(y_ref.dtype)
        return carry

    if nchunk == 1:
        chunk(0, 0)
    else:
        lax.fori_loop(0, nchunk, chunk, 0)


def _seq_blocking(L):
    cn = math.gcd(L, CHUNK)
    lb = min(L, 512)
    assert L % lb == 0 and lb % cn == 0
    return cn, lb, L // lb


def _hgrn(proj, row0, B, L, lb_logits, norm_g, layer, state):
    cn, LB, ncb = _seq_blocking(L)
    assert row0 % LB == 0
    rb0 = row0 // LB
    mall, masks = _hgrn_tables(cn)
    has_state = state is not None
    col = lambda cidx: pl.BlockSpec((LB, WIDTH), lambda b, j: (rb0 + b * ncb + j, cidx))
    full = lambda a: pl.BlockSpec(a.shape, lambda b, j: (0,) * a.ndim)
    sspec = pl.BlockSpec((1, N_HEADS, HEAD_DIM, HEAD_DIM), lambda b, j: (b, 0, 0, 0))
    ng = norm_g.reshape(1, HEAD_DIM)
    mall = jnp.asarray(mall)
    masks = jnp.asarray(masks)
    args = [lb_logits, ng, mall, masks, proj, proj, proj, proj]
    in_specs = [full(lb_logits), full(ng), full(mall), full(masks), col(0), col(1), col(2), col(3)]
    if has_state:
        args.append(state)
        in_specs.append(sspec)
    return pl.pallas_call(
        functools.partial(_hgrn_kernel, layer=layer, cn=cn, nchunk=LB // cn, has_state=has_state),
        grid=(B, ncb),
        in_specs=in_specs,
        out_specs=[pl.BlockSpec((LB, WIDTH), lambda b, j: (b * ncb + j, 0)), sspec],
        out_shape=[jax.ShapeDtypeStruct((B * L, WIDTH), F32),
                   jax.ShapeDtypeStruct((B, N_HEADS, HEAD_DIM, HEAD_DIM), F32)],
        compiler_params=_params("parallel", "arbitrary"),
        name="hgrn2",
    )(*args)


def _mlstm_kernel(*refs, cn, nchunk, has_state):
    if has_state:
        (gt_ref, q_ref, k_ref, v_ref, o_ref, c0_ref, n0_ref, m0_ref,
         y_ref, c_ref, n_ref, m_ref) = refs
    else:
        (gt_ref, q_ref, k_ref, v_ref, o_ref, y_ref, c_ref, n_ref, m_ref) = refs
    j = pl.program_id(1)

    @pl.when(j == 0)
    def _():
        if has_state:
            c_ref[...] = c0_ref[...]
            n_ref[...] = n0_ref[...]
            m_ref[...] = m0_ref[...]
        else:
            c_ref[...] = jnp.zeros(c_ref.shape, F32)
            n_ref[...] = jnp.zeros(n_ref.shape, F32)
            m_ref[...] = jnp.zeros(m_ref.shape, F32)

    incl, _, _ = _tri_masks(cn)
    tril = incl.astype(F32)
    sel = _lane0_onehot(cn)

    def chunk(c, carry):
        r0 = pl.multiple_of(c * cn, cn)
        gates = gt_ref[pl.ds(r0, cn), :]
        f_all = _dot_hi(tril, _log_sigmoid(gates))
        for h in range(N_HEADS):
            cs = slice(h * HEAD_DIM, (h + 1) * HEAD_DIM)
            i_col = gates[:, h:h + 1]
            F = f_all[:, N_HEADS + h:N_HEADS + h + 1]
            q = q_ref[pl.ds(r0, cn), cs]
            k = k_ref[pl.ds(r0, cn), cs] * (HEAD_DIM ** -0.5)
            v = v_ref[pl.ds(r0, cn), cs]
            C = c_ref[0, h]
            n = n_ref[0, h:h + 1, :]
            m = m_ref[0, h:h + 1, 0:1]
            raw = F - _row_form(F - i_col, sel)
            inter = F + m
            m_t = jnp.maximum(inter, jnp.max(jnp.where(incl, raw, NEG_BIG), axis=-1, keepdims=True))
            w_inter = jnp.exp(inter - m_t)
            s = jnp.where(incl, _dot_nt(q, k) * jnp.exp(jnp.where(incl, raw - m_t, 0.0)), 0.0)
            num = _dot(s, v) + w_inter * _dot(q, C)
            den = jnp.sum(s, axis=-1, keepdims=True) + w_inter * jnp.sum(q * n, axis=-1, keepdims=True)
            hh = num / jnp.maximum(jnp.abs(den), jnp.exp(-m_t))
            m_new = m_t[cn - 1:cn]
            w_c = jnp.exp(inter[cn - 1:cn] - m_new)
            w_s = jnp.exp(F[cn - 1:cn] - F + i_col - m_new)
            ks = k * w_s
            c_ref[0, h] = w_c * C + _dot_tn(ks, v)
            n_ref[0, h:h + 1, :] = w_c * n + jnp.sum(ks, axis=0, keepdims=True)
            m_ref[0, h:h + 1, :] = jnp.broadcast_to(m_new, (1, HEAD_DIM))
            y_ref[pl.ds(r0, cn), cs] = (_sigmoid(o_ref[pl.ds(r0, cn), cs]) * hh).astype(y_ref.dtype)
        return carry

    if nchunk == 1:
        chunk(0, 0)
    else:
        lax.fori_loop(0, nchunk, chunk, 0)


def _mlstm(proj, gates, row0, B, L, state):
    cn, LB, ncb = _seq_blocking(L)
    rb0 = row0 // LB
    has_state = state is not None
    col = lambda cidx: pl.BlockSpec((LB, WIDTH), lambda b, j: (rb0 + b * ncb + j, cidx))
    cspec = pl.BlockSpec((1, N_HEADS, HEAD_DIM, HEAD_DIM), lambda b, j: (b, 0, 0, 0))
    nspec = pl.BlockSpec((1, N_HEADS, HEAD_DIM), lambda b, j: (b, 0, 0))
    args = [gates, proj, proj, proj, proj]
    in_specs = [pl.BlockSpec((LB, LANES), lambda b, j: (rb0 + b * ncb + j, 0)),
                col(4), col(5), col(6), col(7)]
    if has_state:
        c0, n0, m0 = state
        args += [c0, n0, jnp.broadcast_to(m0[:, :, None], (B, N_HEADS, HEAD_DIM))]
        in_specs += [cspec, nspec, nspec]
    y, c, n, m = pl.pallas_call(
        functools.partial(_mlstm_kernel, cn=cn, nchunk=LB // cn, has_state=has_state),
        grid=(B, ncb),
        in_specs=in_specs,
        out_specs=[pl.BlockSpec((LB, WIDTH), lambda b, j: (b * ncb + j, 0)), cspec, nspec, nspec],
        out_shape=[jax.ShapeDtypeStruct((B * L, WIDTH), F32),
                   jax.ShapeDtypeStruct((B, N_HEADS, HEAD_DIM, HEAD_DIM), F32),
                   jax.ShapeDtypeStruct((B, N_HEADS, HEAD_DIM), F32),
                   jax.ShapeDtypeStruct((B, N_HEADS, HEAD_DIM), F32)],
        compiler_params=_params("parallel", "arbitrary"),
        name="mlstm",
    )(*args)
    return y, c, n, m[:, :, 0]


_CONV_PAD = 8


def _gdn_kernel(*refs, cn, nchunk, has_state):
    if has_state:
        (gt_ref, cw_ref, al_ref, dtb_ref, ng_ref, xq_ref, xk_ref, xv_ref, g_ref, s0_ref, b0_ref,
         y_ref, s_ref, buf_ref, xp_ref, cv_ref) = refs
    else:
        (gt_ref, cw_ref, al_ref, dtb_ref, ng_ref, xq_ref, xk_ref, xv_ref, g_ref,
         y_ref, s_ref, buf_ref, xp_ref, cv_ref) = refs
    j = pl.program_id(1)
    LB = xq_ref.shape[0]
    hist = CONV_W - 1
    h0 = _CONV_PAD - hist

    @pl.when(j == 0)
    def _():
        if has_state:
            s_ref[...] = s0_ref[...]
            xp_ref[h0:_CONV_PAD, :] = b0_ref[0]
        else:
            s_ref[...] = jnp.zeros(s_ref.shape, F32)
            xp_ref[h0:_CONV_PAD, :] = jnp.zeros((hist, 3 * WIDTH), F32)

    xp_ref[_CONV_PAD:_CONV_PAD + LB, 0:WIDTH] = xq_ref[...]
    xp_ref[_CONV_PAD:_CONV_PAD + LB, WIDTH:2 * WIDTH] = xk_ref[...]
    xp_ref[_CONV_PAD:_CONV_PAD + LB, 2 * WIDTH:3 * WIDTH] = xv_ref[...]
    acc = xp_ref[h0:h0 + LB, :] * cw_ref[0:1, :]
    for t in range(1, CONV_W):
        acc = acc + xp_ref[h0 + t:h0 + t + LB, :] * cw_ref[t:t + 1, :]
    cv_ref[...] = _silu(acc)
    new_hist = xp_ref[_CONV_PAD + LB - hist:_CONV_PAD + LB, :]
    xp_ref[h0:_CONV_PAD, :] = new_hist
    buf_ref[0] = new_hist

    incl, strict, eye = _tri_masks(cn)
    tril = incl.astype(F32)
    eye_f = eye.astype(F32)
    sel = _lane0_onehot(cn)
    lane = lax.broadcasted_iota(jnp.int32, (1, LANES), 1)
    is_a = (lane >= 2 * N_HEADS) & (lane < 3 * N_HEADS)
    neg_rate = -jnp.exp(al_ref[...])
    dtb = dtb_ref[...]
    ng = ng_ref[...]
    nsq = int(math.log2(cn))

    def chunk(c, carry):
        r0 = pl.multiple_of(c * cn, cn)
        gates = gt_ref[pl.ds(r0, cn), :]
        g_all = jnp.where(is_a, neg_rate * _softplus(gates + dtb), 0.0)
        gam_all = _dot_hi(tril, g_all)
        beta_all = _sigmoid(gates)
        for h in range(N_HEADS):
            cs = slice(h * HEAD_DIM, (h + 1) * HEAD_DIM)
            gam = gam_all[:, 2 * N_HEADS + h:2 * N_HEADS + h + 1]
            beta = beta_all[:, 3 * N_HEADS + h:3 * N_HEADS + h + 1]
            qr = cv_ref[pl.ds(r0, cn), cs]
            kr = cv_ref[pl.ds(r0, cn), WIDTH + h * HEAD_DIM:WIDTH + (h + 1) * HEAD_DIM]
            v = cv_ref[pl.ds(r0, cn), 2 * WIDTH + h * HEAD_DIM:2 * WIDTH + (h + 1) * HEAD_DIM]
            q = qr * lax.rsqrt(jnp.sum(qr * qr, axis=-1, keepdims=True) + NORM_EPS) * (HEAD_DIM ** -0.5)
            k = kr * lax.rsqrt(jnp.sum(kr * kr, axis=-1, keepdims=True) + NORM_EPS)
            S = s_ref[0, h]
            diff = gam - _row_form(gam, sel)
            a_mat = jnp.where(strict, beta * _dot_nt(k, k) * jnp.exp(jnp.where(strict, diff, 0.0)), 0.0)
            pw = -a_mat
            t_inv = eye_f + pw
            for _ in range(nsq - 1):
                pw = _dot_hi(pw, pw)
                t_inv = t_inv + _dot_hi(t_inv, pw)
            eg = jnp.exp(gam)
            u = _dot_hi(t_inv, beta * v) - _dot(_dot_hi(t_inv, (beta * eg) * k), S)
            qk = jnp.where(incl, _dot_nt(q, k) * jnp.exp(jnp.where(incl, diff, 0.0)), 0.0)
            o = _dot(qk, u) + _dot(q * eg, S)
            g_last = gam[cn - 1:cn]
            s_ref[0, h] = jnp.exp(g_last) * S + _dot_tn(k * jnp.exp(g_last - gam), u)
            yn = o * lax.rsqrt(jnp.mean(o * o, axis=-1, keepdims=True) + NORM_EPS) * ng
            y_ref[pl.ds(r0, cn), cs] = (yn * _silu(g_ref[pl.ds(r0, cn), cs])).astype(y_ref.dtype)
        return carry

    if nchunk == 1:
        chunk(0, 0)
    else:
        lax.fori_loop(0, nchunk, chunk, 0)


def _gate_lane_params(a_log, dt_bias):
    z = jnp.zeros((1, LANES), F32)
    al = z.at[0, 2 * N_HEADS:3 * N_HEADS].set(a_log.astype(F32))
    db = z.at[0, 2 * N_HEADS:3 * N_HEADS].set(dt_bias.astype(F32))
    return al, db


def _gdn(proj, gates, row0, B, L, conv_w_t, a_log, dt_bias, norm_g, state):
    cn, LB, ncb = _seq_blocking(L)
    rb0 = row0 // LB
    has_state = state is not None
    hist = CONV_W - 1
    col = lambda cidx: pl.BlockSpec((LB, WIDTH), lambda b, j: (rb0 + b * ncb + j, cidx))
    full = lambda a: pl.BlockSpec(a.shape, lambda b, j: (0,) * a.ndim)
    sspec = pl.BlockSpec((1, N_HEADS, HEAD_DIM, HEAD_DIM), lambda b, j: (b, 0, 0, 0))
    bspec = pl.BlockSpec((1, hist, 3 * WIDTH), lambda b, j: (b, 0, 0))
    al, db = _gate_lane_params(a_log, dt_bias)
    ng = norm_g.reshape(1, HEAD_DIM)
    args = [gates, conv_w_t, al, db, ng, proj, proj, proj, proj]
    in_specs = [pl.BlockSpec((LB, LANES), lambda b, j: (rb0 + b * ncb + j, 0)),
                full(conv_w_t), full(al), full(db), full(ng), col(8), col(9), col(10), col(11)]
    if has_state:
        args += [state[0], state[1]]
        in_specs += [sspec, bspec]
    return pl.pallas_call(
        functools.partial(_gdn_kernel, cn=cn, nchunk=LB // cn, has_state=has_state),
        grid=(B, ncb),
        in_specs=in_specs,
        out_specs=[pl.BlockSpec((LB, WIDTH), lambda b, j: (b * ncb + j, 0)), sspec, bspec],
        out_shape=[jax.ShapeDtypeStruct((B * L, WIDTH), F32),
                   jax.ShapeDtypeStruct((B, N_HEADS, HEAD_DIM, HEAD_DIM), F32),
                   jax.ShapeDtypeStruct((B, hist, 3 * WIDTH), F32)],
        scratch_shapes=[pltpu.VMEM((_CONV_PAD + LB, 3 * WIDTH), F32),
                        pltpu.VMEM((LB, 3 * WIDTH), F32)],
        compiler_params=_params("parallel", "arbitrary"),
        name="gdn",
    )(*args)


def _merge_kernel(h_ref, y_ref, wm_ref, wb_ref, bm_ref, o_ref):
    h = h_ref[...]
    acc = None
    for jdx in range(3):
        gate = _sigmoid(jnp.dot(h, wm_ref[jdx], preferred_element_type=F32) + bm_ref[jdx])
        br = jnp.dot(y_ref[jdx], wb_ref[jdx], preferred_element_type=F32)
        acc = gate * br if acc is None else acc + gate * br
    o_ref[...] = acc.astype(o_ref.dtype)


def _merge(h, y3, wm, wb, bm, tm=512, tn=256):
    T, D = h.shape
    N = wm.shape[2]
    W = y3.shape[2]
    return pl.pallas_call(
        _merge_kernel,
        grid=(T // tm, N // tn),
        in_specs=[pl.BlockSpec((tm, D), lambda i, j: (i, 0)),
                  pl.BlockSpec((3, tm, W), lambda i, j: (0, i, 0)),
                  pl.BlockSpec((3, D, tn), lambda i, j: (0, 0, j)),
                  pl.BlockSpec((3, W, tn), lambda i, j: (0, 0, j)),
                  pl.BlockSpec((3, 1, tn), lambda i, j: (0, 0, j))],
        out_specs=pl.BlockSpec((tm, tn), lambda i, j: (i, j)),
        out_shape=jax.ShapeDtypeStruct((T, N), BF16),
        compiler_params=_params("parallel", "arbitrary"),
        name="merge",
    )(h, y3, wm, wb, bm.reshape(3, 1, N).astype(F32))


def _layer_norm(z, g, b):
    mu = jnp.mean(z, axis=-1, keepdims=True)
    zc = z - mu
    var = jnp.mean(zc * zc, axis=-1, keepdims=True)
    return zc * lax.rsqrt(var + LN_EPS) * g + b


def _route(logits):
    lane = lax.broadcasted_iota(jnp.int32, logits.shape, 1)
    valid = lane < N_EXPERTS
    lg = jnp.where(valid, logits, NEG_BIG)
    ex = jnp.where(valid, jnp.exp(lg - jnp.max(lg, axis=-1, keepdims=True)), 0.0)
    probs = ex / jnp.sum(ex, axis=-1, keepdims=True)
    big = LANES
    best_score = best_i1 = best_i2 = best_m1 = best_m2 = None
    for g in range(N_GROUPS):
        in_g = (lane >= g * EXPERTS_PER_GROUP) & (lane < (g + 1) * EXPERTS_PER_GROUP)
        m1 = jnp.max(jnp.where(in_g, probs, -1.0), axis=-1, keepdims=True)
        i1 = jnp.min(jnp.where(in_g & (probs == m1), lane, big), axis=-1, keepdims=True)
        rest = in_g & (lane != i1)
        m2 = jnp.max(jnp.where(rest, probs, -1.0), axis=-1, keepdims=True)
        i2 = jnp.min(jnp.where(rest & (probs == m2), lane, big), axis=-1, keepdims=True)
        score = m1 + m2
        if g == 0:
            best_score, best_i1, best_i2, best_m1, best_m2 = score, i1, i2, m1, m2
        else:
            better = score > best_score
            best_score = jnp.where(better, score, best_score)
            best_i1 = jnp.where(better, i1, best_i1)
            best_i2 = jnp.where(better, i2, best_i2)
            best_m1 = jnp.where(better, m1, best_m1)
            best_m2 = jnp.where(better, m2, best_m2)
    tot = best_m1 + best_m2
    return (jnp.where(lane == best_i1, best_m1 / tot, 0.0)
            + jnp.where(lane == best_i2, best_m2 / tot, 0.0))


def _post_kernel(*refs, alpha, gt_idx, sh_idx, sc_idx, route):
    if route:
        x_ref, y_ref, mod_ref, g_ref, b_ref, wr_ref, br_ref, xo_ref, h_ref, comb_ref = refs
    else:
        x_ref, y_ref, mod_ref, g_ref, b_ref, xo_ref = refs
    m = mod_ref[...]
    z = alpha * x_ref[...] + m[:, gt_idx:gt_idx + 1, :] * y_ref[...]
    xn = _layer_norm(z, g_ref[...], b_ref[...])
    xo_ref[...] = xn
    if route:
        h2 = xn * (1.0 + m[:, sc_idx:sc_idx + 1, :]) + m[:, sh_idx:sh_idx + 1, :]
        h_ref[...] = h2.astype(h_ref.dtype)
        bb, tl, D = h2.shape
        logits = _dot_hi(h2.reshape(bb * tl, D), wr_ref[...]) + br_ref[...]
        comb_ref[...] = _route(logits).reshape(bb, tl, LANES)


def _post(x, y, mod, g, b, alpha, gt_idx, route_args=None):
    B, L, D = x.shape
    bb, tl = _seq_tiles(B, L)
    blk = pl.BlockSpec((bb, tl, D), lambda i, j: (i, j, 0))
    vec = pl.BlockSpec((1, 1, D), lambda i, j: (0, 0, 0))
    in_specs = [blk, blk, pl.BlockSpec((bb, 6, D), lambda i, j: (i, 0, 0)), vec, vec]
    args = [x, y, mod, g.reshape(1, 1, D), b.reshape(1, 1, D)]
    out_specs = [blk]
    out_shape = [jax.ShapeDtypeStruct((B, L, D), F32)]
    route = route_args is not None
    if route:
        wr, br = route_args
        in_specs += [pl.BlockSpec((D, LANES), lambda i, j: (0, 0)), pl.BlockSpec((1, LANES), lambda i, j: (0, 0))]
        args += [wr, br]
        out_specs += [blk, pl.BlockSpec((bb, tl, LANES), lambda i, j: (i, j, 0))]
        out_shape += [jax.ShapeDtypeStruct((B, L, D), BF16), jax.ShapeDtypeStruct((B, L, LANES), F32)]
    return pl.pallas_call(
        functools.partial(_post_kernel, alpha=alpha, gt_idx=gt_idx, sh_idx=3, sc_idx=4, route=route),
        grid=(B // bb, L // tl),
        in_specs=in_specs,
        out_specs=out_specs,
        out_shape=out_shape,
        compiler_params=_params("parallel", "arbitrary"),
        name="post_route" if route else "post",
    )(*args)


def _moe_kernel(x_ref, comb_ref, wg_ref, wu_ref, wd_ref, o_ref):
    e = pl.program_id(1)
    f = pl.program_id(2)

    @pl.when((e == 0) & (f == 0))
    def _():
        o_ref[...] = jnp.zeros(o_ref.shape, F32)

    x = x_ref[...]
    lane = lax.broadcasted_iota(jnp.int32, comb_ref.shape, 1)
    w_col = jnp.sum(jnp.where(lane == e, comb_ref[...], 0.0), axis=-1, keepdims=True)
    he = _silu(jnp.dot(x, wg_ref[0], preferred_element_type=F32)) * jnp.dot(x, wu_ref[0], preferred_element_type=F32)
    o_ref[...] += w_col * jnp.dot(he.astype(BF16), wd_ref[0], preferred_element_type=F32)


def _moe_dense(h2, comb, wg, wu, wd, tm=512, tf=512):
    T, D = h2.shape
    E, _, FF = wg.shape
    return pl.pallas_call(
        _moe_kernel,
        grid=(T // tm, E, FF // tf),
        in_specs=[pl.BlockSpec((tm, D), lambda i, e, f: (i, 0)),
                  pl.BlockSpec((tm, LANES), lambda i, e, f: (i, 0)),
                  pl.BlockSpec((1, D, tf), lambda i, e, f: (e, 0, f)),
                  pl.BlockSpec((1, D, tf), lambda i, e, f: (e, 0, f)),
                  pl.BlockSpec((1, tf, D), lambda i, e, f: (e, f, 0))],
        out_specs=pl.BlockSpec((tm, D), lambda i, e, f: (i, 0)),
        out_shape=jax.ShapeDtypeStruct((T, D), F32),
        compiler_params=_params("parallel", "arbitrary", "arbitrary"),
        name="moe_dense",
    )(h2, comb, wg, wu, wd)


def _permute_w_in(w_in_l, b_in_l):
    W = WIDTH
    g0 = 8 * W
    c0 = g0 + 2 * N_HEADS
    g1 = c0 + 4 * W
    wide = lambda a: jnp.concatenate([a[..., :g0], a[..., c0:g1]], axis=-1)
    narrow = lambda a: jnp.concatenate([a[..., g0:c0], a[..., g1:]], axis=-1)
    pad = LANES - 4 * N_HEADS
    w_g = jnp.pad(narrow(w_in_l), ((0, 0), (0, pad)))
    b_g = jnp.pad(narrow(b_in_l), ((0, pad),))
    return wide(w_in_l).astype(BF16), wide(b_in_l), w_g, b_g


def kernel(x_prompt, x_sample, state_hgrn, state_mlstm_C, state_mlstm_n, state_mlstm_m, state_gdn, state_conv, c_prompt, c_sample, w_in, b_in, hgrn_lb_logits, hgrn_norm_g, gdn_conv_w, gdn_A_log, gdn_dt_bias, gdn_norm_g, w_branch, w_merge, b_merge, w_out, ln_mix_g, ln_mix_b, w_ada, b_ada, w_router, b_router, w_exp_gate, w_exp_up, w_exp_down, ln_ffn_g, ln_ffn_b):
    depth = w_in.shape[0]
    Bp, Lp, D = x_prompt.shape
    Bs, Ls, _ = x_sample.shape
    Tp, Ts = Bp * Lp, Bs * Ls
    alpha = (2 * depth) ** 0.25

    nc = Bp + Bs
    nc_pad = -(-nc // 16) * 16
    c_all = jnp.pad(jnp.concatenate([c_prompt, c_sample], axis=0), ((0, nc_pad - nc), (0, 0)))
    c_act = (c_all * jax.nn.sigmoid(c_all)).astype(BF16)

    wr = jnp.pad(w_router.astype(F32), ((0, 0), (0, LANES - N_EXPERTS)))
    br = jnp.pad(b_router.astype(F32), ((0, LANES - N_EXPERTS),)).reshape(1, LANES)

    xp, xs = x_prompt, x_sample
    new_p, new_s = [], []
    for l in range(depth):
        mod = _matmul(c_act, w_ada[l], b_ada[l], F32, tm=nc_pad, tn=512, name="ada")
        mod_p = mod[:Bp].reshape(Bp, 6, D)
        mod_s = mod[Bp:nc].reshape(Bs, 6, D)

        h = jnp.concatenate([_modulate(xp, mod_p, 0, 1).reshape(Tp, D),
                             _modulate(xs, mod_s, 0, 1).reshape(Ts, D)], axis=0)
        w_wide, b_wide, w_g, b_g = _permute_w_in(w_in[l], b_in[l])
        proj = _matmul(h, w_wide, b_wide, F32, tm=1024, tn=512, name="proj")
        gates = _matmul(h, w_g, b_g, F32, tm=1024, tn=LANES, name="proj_gates")

        conv_w_t = gdn_conv_w[l].T.astype(F32)
        ya_p, sa_p = _hgrn(proj, 0, Bp, Lp, hgrn_lb_logits, hgrn_norm_g[l], l, None)
        ya_s, sa_s = _hgrn(proj, Tp, Bs, Ls, hgrn_lb_logits, hgrn_norm_g[l], l, state_hgrn[l])
        yb_p, cb_p, nb_p, mb_p = _mlstm(proj, gates, 0, Bp, Lp, None)
        yb_s, cb_s, nb_s, mb_s = _mlstm(proj, gates, Tp, Bs, Ls,
                                        (state_mlstm_C[l], state_mlstm_n[l], state_mlstm_m[l]))
        yc_p, sc_p, bc_p = _gdn(proj, gates, 0, Bp, Lp, conv_w_t, gdn_A_log[l], gdn_dt_bias[l],
                                gdn_norm_g[l], None)
        yc_s, sc_s, bc_s = _gdn(proj, gates, Tp, Bs, Ls, conv_w_t, gdn_A_log[l], gdn_dt_bias[l],
                                gdn_norm_g[l], (state_gdn[l], state_conv[l]))
        new_p.append((sa_p, cb_p, nb_p, mb_p, sc_p, bc_p))
        new_s.append((sa_s, cb_s, nb_s, mb_s, sc_s, bc_s))

        y3 = jnp.stack([jnp.concatenate([ya_p, ya_s], axis=0),
                        jnp.concatenate([yb_p, yb_s], axis=0),
                        jnp.concatenate([yc_p, yc_s], axis=0)]).astype(BF16)
        merged = _merge(h, y3, w_merge[l].astype(BF16), w_branch[l].astype(BF16), b_merge[l])
        mix = _matmul(merged, w_out[l].astype(BF16), jnp.zeros((D,), F32), F32, tm=1024, tn=512, name="out_proj")

        xp, h2_p, comb_p = _post(xp, mix[:Tp].reshape(Bp, Lp, D), mod_p, ln_mix_g[l], ln_mix_b[l],
                                 alpha, 2, (wr, br))
        xs, h2_s, comb_s = _post(xs, mix[Tp:].reshape(Bs, Ls, D), mod_s, ln_mix_g[l], ln_mix_b[l],
                                 alpha, 2, (wr, br))
        h2 = jnp.concatenate([h2_p.reshape(Tp, D), h2_s.reshape(Ts, D)], axis=0)
        comb = jnp.concatenate([comb_p.reshape(Tp, LANES), comb_s.reshape(Ts, LANES)], axis=0)
        ffn = _moe_dense(h2, comb, w_exp_gate[l].astype(BF16), w_exp_up[l].astype(BF16),
                         w_exp_down[l].astype(BF16))
        (xp,) = _post(xp, ffn[:Tp].reshape(Bp, Lp, D), mod_p, ln_ffn_g[l], ln_ffn_b[l], alpha, 5)
        (xs,) = _post(xs, ffn[Tp:].reshape(Bs, Ls, D), mod_s, ln_ffn_g[l], ln_ffn_b[l], alpha, 5)

    st_p = [jnp.stack(s, axis=0) for s in zip(*new_p)]
    st_s = [jnp.stack(s, axis=0) for s in zip(*new_s)]
    hgrn_p, mc_p, mn_p, mm_p, gdn_p, conv_p = st_p
    hgrn_s, mc_s, mn_s, mm_s, gdn_s, conv_s = st_s
    return (xp, xs, hgrn_p, hgrn_s, mc_p, mc_s, mn_p, mn_s, mm_p, mm_s, gdn_p, gdn_s, conv_p, conv_s)
```

```python
import functools
import math

import numpy as np
import jax
import jax.numpy as jnp
from jax import lax
from jax.experimental import pallas as pl
from jax.experimental.pallas import tpu as pltpu

F32 = jnp.float32
BF16 = jnp.bfloat16
HIGHEST = lax.Precision.HIGHEST

HEAD_DIM = 128
N_HEADS = 8
WIDTH = N_HEADS * HEAD_DIM
CHUNK = 64
CONV_W = 4
N_EXPERTS = 16
N_GROUPS = 4
EXPERTS_PER_GROUP = N_EXPERTS // N_GROUPS
TOP_K = 2
LN_EPS = 1e-5
NORM_EPS = 1e-6
LB_TINY = 1e-20
NEG_BIG = -1e30
LANES = 128
VMEM_LIMIT = 56 * 1024 * 1024
TOK_BLOCK = 256
MOE_TM = 512
MOE_TF = 256
HEADS = tuple(slice(h * HEAD_DIM, (h + 1) * HEAD_DIM) for h in range(N_HEADS))


def _params(*sem):
    return pltpu.CompilerParams(dimension_semantics=sem, vmem_limit_bytes=VMEM_LIMIT)


def _dot(a, b):
    return jnp.dot(a.astype(BF16), b.astype(BF16), preferred_element_type=F32)


def _dot_nt(a, b):
    return lax.dot_general(a.astype(BF16), b.astype(BF16), (((1,), (1,)), ((), ())),
                           preferred_element_type=F32)


def _dot_tn(a, b):
    return lax.dot_general(a.astype(BF16), b.astype(BF16), (((0,), (0,)), ((), ())),
                           preferred_element_type=F32)


def _dot_hi(a, b):
    return jnp.dot(a, b, precision=HIGHEST, preferred_element_type=F32)


def _split3(x):
    x1 = x.astype(BF16)
    r1 = x - x1.astype(F32)
    x2 = r1.astype(BF16)
    x3 = (r1 - x2.astype(F32)).astype(BF16)
    return x1, x2, x3


def _split2(x):
    x1 = x.astype(BF16)
    return x1, (x - x1.astype(F32)).astype(BF16)


def _dot_exact_l(a_bf, x):
    x1, x2, x3 = _split3(x)
    d = lambda p: jnp.dot(a_bf, p, preferred_element_type=F32)
    return d(x1) + (d(x2) + d(x3))


def _dot_exact_r(x, b_bf):
    x1, x2, x3 = _split3(x)
    d = lambda p: jnp.dot(p, b_bf, preferred_element_type=F32)
    return d(x1) + (d(x2) + d(x3))


def _dot3(a, b):
    a1, a2 = _split2(a)
    b1, b2 = _split2(b)
    d = lambda p, q: jnp.dot(p, q, preferred_element_type=F32)
    return d(a1, b1) + (d(a1, b2) + d(a2, b1))


def _sigmoid(x):
    return jax.nn.sigmoid(x)


def _silu(x):
    return x * jax.nn.sigmoid(x)


def _log_sigmoid(x):
    return jnp.minimum(x, 0.0) - jnp.log1p(jnp.exp(-jnp.abs(x)))


def _softplus(x):
    return jnp.maximum(x, 0.0) + jnp.log1p(jnp.exp(-jnp.abs(x)))


def _logaddexp(a, b):
    return jnp.maximum(a, b) + jnp.log1p(jnp.exp(-jnp.abs(a - b)))


def _tri_masks(cn):
    r = lax.broadcasted_iota(jnp.int32, (cn, cn), 0)
    c = lax.broadcasted_iota(jnp.int32, (cn, cn), 1)
    return r >= c, r > c, r == c


def _rms_gate(o, ng, gate):
    return o * lax.rsqrt(jnp.mean(o * o, axis=-1, keepdims=True) + NORM_EPS) * ng * gate


def _mm_kernel(x_ref, w_ref, b_ref, o_ref, wbf_ref, *, act):
    @pl.when(pl.program_id(1) == 0)
    def _():
        wbf_ref[...] = w_ref[...].astype(BF16)

    acc = jnp.dot(x_ref[...], wbf_ref[...], preferred_element_type=F32) + b_ref[...]
    if act == "sigmoid":
        acc = _sigmoid(acc)
    o_ref[...] = acc.astype(o_ref.dtype)


def _matmul(x, w, w_block, w_index, b, b_block, b_index, n_out, out_dtype, *, tm, tn, name, act=None):
    M, K = x.shape
    tm = min(tm, M)
    assert M % tm == 0 and n_out % tn == 0, (M, tm, n_out, tn)
    return pl.pallas_call(
        functools.partial(_mm_kernel, act=act),
        grid=(n_out // tn, M // tm),
        in_specs=[pl.BlockSpec((tm, K), lambda n, m: (m, 0)),
                  pl.BlockSpec(w_block, lambda n, m: w_index(n)),
                  pl.BlockSpec(b_block, lambda n, m: b_index(n))],
        out_specs=pl.BlockSpec((tm, tn), lambda n, m: (m, n)),
        out_shape=jax.ShapeDtypeStruct((M, n_out), out_dtype),
        scratch_shapes=[pltpu.VMEM((K, tn), BF16)],
        compiler_params=_params("arbitrary", "arbitrary"),
        name=name,
    )(x, w, b)


def _with_mod(npb, mp_ref, ms_ref, body):
    i = pl.program_id(0)

    @pl.when(i < npb)
    def _():
        body(mp_ref[...])

    @pl.when(i >= npb)
    def _():
        body(ms_ref[...])


def _mod_specs(npb, n_sample_seq, blocks_per_prompt_seq, n_prompt_seq, spb, D):
    mp = pl.BlockSpec((1, 6, D), lambda i: (n_sample_seq + jnp.minimum(i // blocks_per_prompt_seq,
                                                                        n_prompt_seq - 1), 0, 0))
    ms = pl.BlockSpec((spb, 6, D), lambda i: (jnp.maximum(i - npb, 0), 0, 0))
    return mp, ms


def _modulate_kernel(x_ref, mp_ref, ms_ref, o_ref, *, npb, spb, sh_idx, sc_idx):
    tb, D = x_ref.shape

    def body(m):
        x = x_ref[...].reshape(spb, tb // spb, D)
        h = x * (1.0 + m[:, sc_idx:sc_idx + 1, :]) + m[:, sh_idx:sh_idx + 1, :]
        o_ref[...] = h.reshape(tb, D).astype(o_ref.dtype)

    _with_mod(npb, mp_ref, ms_ref, body)


def _layer_norm(z, g, b):
    mu = jnp.mean(z, axis=-1, keepdims=True)
    zc = z - mu
    var = jnp.mean(zc * zc, axis=-1, keepdims=True)
    return zc * lax.rsqrt(var + LN_EPS) * g + b


def _route(logits):
    lane = lax.broadcasted_iota(jnp.int32, logits.shape, 1)
    valid = lane < N_EXPERTS
    lg = jnp.where(valid, logits, NEG_BIG)
    ex = jnp.where(valid, jnp.exp(lg - jnp.max(lg, axis=-1, keepdims=True)), 0.0)
    probs = ex / jnp.sum(ex, axis=-1, keepdims=True)
    big = LANES
    best = None
    for g in range(N_GROUPS):
        in_g = (lane >= g * EXPERTS_PER_GROUP) & (lane < (g + 1) * EXPERTS_PER_GROUP)
        m1 = jnp.max(jnp.where(in_g, probs, -1.0), axis=-1, keepdims=True)
        i1 = jnp.min(jnp.where(in_g & (probs == m1), lane, big), axis=-1, keepdims=True)
        rest = in_g & (lane != i1)
        m2 = jnp.max(jnp.where(rest, probs, -1.0), axis=-1, keepdims=True)
        i2 = jnp.min(jnp.where(rest & (probs == m2), lane, big), axis=-1, keepdims=True)
        cand = (m1 + m2, i1, i2, m1, m2)
        if best is None:
            best = cand
        else:
            better = cand[0] > best[0]
            best = tuple(jnp.where(better, c, o) for c, o in zip(cand, best))
    _, i1, i2, m1, m2 = best
    tot = m1 + m2
    return (jnp.where(lane == 0, i1.astype(F32), 0.0) + jnp.where(lane == 1, i2.astype(F32), 0.0)
            + jnp.where(lane == 2, m1 / tot, 0.0) + jnp.where(lane == 3, m2 / tot, 0.0))


def _post_route_kernel(x_ref, y_ref, mp_ref, ms_ref, g_ref, b_ref, wr_ref, br_ref,
                       xo_ref, h_ref, r_ref, *, npb, spb, alpha):
    tb, D = x_ref.shape

    def body(m):
        x = x_ref[...].reshape(spb, tb // spb, D)
        y = y_ref[...].reshape(spb, tb // spb, D)
        xn = _layer_norm(alpha * x + m[:, 2:3, :] * y, g_ref[...], b_ref[...])
        xo_ref[...] = xn.reshape(tb, D)
        h2 = (xn * (1.0 + m[:, 4:5, :]) + m[:, 3:4, :]).reshape(tb, D)
        h_ref[...] = h2
        r_ref[...] = _route(_dot_hi(h2, wr_ref[...]) + br_ref[...])

    _with_mod(npb, mp_ref, ms_ref, body)


def _post_final_kernel(x_ref, y0_ref, y1_ref, mp_ref, ms_ref, g_ref, b_ref, xo_ref, *, npb, spb, alpha):
    tb, D = x_ref.shape

    def body(m):
        x = x_ref[...].reshape(spb, tb // spb, D)
        y = (y0_ref[...] + y1_ref[...]).reshape(spb, tb // spb, D)
        xn = _layer_norm(alpha * x + m[:, 5:6, :] * y, g_ref[...], b_ref[...])
        xo_ref[...] = xn.reshape(tb, D)

    _with_mod(npb, mp_ref, ms_ref, body)


def _seq_blocking(L):
    cn = math.gcd(L, CHUNK)
    lb = min(L, 512)
    assert L % lb == 0 and lb % cn == 0
    return cn, lb, L // lb


def _run_chunks(chunk, nchunk):
    if nchunk == 1:
        chunk(0, 0)
    else:
        lax.fori_loop(0, nchunk, chunk, 0)


def _hgrn_tables(cn):
    nlev = int(math.log2(cn))
    assert 1 << nlev == cn
    r = np.arange(cn)
    mats = [np.tril(np.ones((cn, cn)))]
    masks = [np.eye(cn)]
    j = r[None, :]
    for lev in range(nlev):
        s = cn >> (lev + 1)
        blk = 2 * s
        pos = r % blk
        ref = r - pos + s - 1
        second = pos >= s
        m_second = (j > ref[:, None]) & (j <= r[:, None])
        m_first = (j > r[:, None]) & (j <= ref[:, None])
        mats.append(np.where(second[:, None], m_second, m_first))
        same = (r[:, None] // blk) == (r[None, :] // blk)
        masks.append(same & second[:, None] & (~second)[None, :])
    return (np.concatenate(mats, 0).astype(np.float32), np.stack(masks).astype(np.float32))


def _hgrn_kernel(*refs, layer, cn, nchunk, has_state, aliased):
    refs = list(refs)
    lbl_ref, ng_ref, mall_ref, masks_ref, q_ref, f_ref, i_ref, g_ref = refs[:8]
    pos = 8
    s0_ref = None
    if has_state:
        s0_ref = refs[pos]
        pos += 1
    if aliased:
        pos += 1
    y_ref, s_ref, st_ref = refs[pos:pos + 3]
    j = pl.program_id(1)
    nlev = masks_ref.shape[0] - 1

    @pl.when(j == 0)
    def _():
        for h in range(N_HEADS):
            if has_state:
                st_ref[h] = s0_ref[0, h].T
            else:
                st_ref[h] = jnp.zeros((HEAD_DIM, HEAD_DIM), F32)

    logits = lbl_ref[...]
    e = jnp.exp(logits - jnp.max(logits, axis=0, keepdims=True))
    p = e / jnp.sum(e, axis=0, keepdims=True)
    lb = jnp.zeros((1, WIDTH), F32)
    for l in range(1, layer + 1):
        lb = lb + p[l:l + 1, :]
    log_lb = jnp.log(jnp.maximum(lb, LB_TINY))
    log1m_lb = jnp.log1p(-lb)
    one_m_lb = 1.0 - lb
    ng = ng_ref[...]
    mall = mall_ref[...]

    def chunk(c, carry):
        r0 = pl.multiple_of(c * cn, cn)
        rows = pl.ds(r0, cn)
        q = _silu(q_ref[rows, :])
        fr = f_ref[rows, :]
        v = i_ref[rows, :].astype(BF16)
        log_f = _logaddexp(log_lb, log1m_lb + _log_sigmoid(fr))
        k = one_m_lb * _sigmoid(-fr)
        dall = _dot_exact_l(mall, log_f)
        b = dall[:cn]
        qb = q.astype(BF16)
        kb = k.astype(BF16)
        att = [masks_ref[0] * _dot_nt(qb[:, cs], kb[:, cs]) for cs in HEADS]
        for lev in range(1, nlev + 1):
            ex = jnp.exp(dall[lev * cn:(lev + 1) * cn])
            qe = (q * ex).astype(BF16)
            ke = (k * ex).astype(BF16)
            att = [a + masks_ref[lev] * _dot_nt(qe[:, cs], ke[:, cs]) for a, cs in zip(att, HEADS)]
        st = [st_ref[h] for h in range(N_HEADS)]
        qd = (q * jnp.exp(b)).astype(BF16)
        o = [_dot(a, v[:, cs]) + _dot_nt(qd[:, cs], s) for a, cs, s in zip(att, HEADS, st)]
        b_last = b[cn - 1:cn]
        k_dec = (k * jnp.exp(b_last - b)).astype(BF16)
        dec = jnp.exp(b_last)
        for h, cs in enumerate(HEADS):
            st_ref[h] = st[h] * dec[:, cs] + _dot_tn(v[:, cs], k_dec[:, cs])
        gate = _sigmoid(g_ref[rows, :])
        for h, cs in enumerate(HEADS):
            y_ref[rows, cs] = _rms_gate(o[h], ng[:, cs], gate[:, cs])
        return carry

    _run_chunks(chunk, nchunk)

    @pl.when(j == pl.num_programs(1) - 1)
    def _():
        for h in range(N_HEADS):
            s_ref[0, h] = st_ref[h].T


def _mixer_specs(T_total, row0, B, L, y_prev):
    cn, LB, ncb = _seq_blocking(L)
    assert row0 % LB == 0
    rb0 = row0 // LB
    col = lambda cidx: pl.BlockSpec((LB, WIDTH), lambda b, j: (rb0 + b * ncb + j, cidx))
    yspec = pl.BlockSpec((LB, WIDTH), lambda b, j: (rb0 + b * ncb + j, 0))
    yshape = jax.ShapeDtypeStruct((T_total, WIDTH), F32)
    sspec = pl.BlockSpec((1, N_HEADS, HEAD_DIM, HEAD_DIM), lambda b, j: (b, 0, 0, 0))
    sshape = jax.ShapeDtypeStruct((B, N_HEADS, HEAD_DIM, HEAD_DIM), F32)
    return cn, LB, ncb, rb0, col, yspec, yshape, sspec, sshape


def _full(a):
    return pl.BlockSpec(a.shape, lambda b, j: (0,) * a.ndim)


def _alias_args(args, in_specs, y_prev):
    if y_prev is None:
        return {}
    args.append(y_prev)
    in_specs.append(pl.BlockSpec(memory_space=pl.ANY))
    return {len(args) - 1: 0}


def _hgrn(proj, a_col0, row0, B, L, lb_logits, norm_g, layer, state, y_prev):
    cn, LB, ncb, rb0, col, yspec, yshape, sspec, sshape = _mixer_specs(proj.shape[0], row0, B, L, y_prev)
    mall, masks = _hgrn_tables(cn)
    mall = jnp.asarray(mall, BF16)
    masks = jnp.asarray(masks)
    ng = jnp.tile(norm_g.reshape(1, HEAD_DIM).astype(F32), (1, N_HEADS))
    args = [lb_logits, ng, mall, masks, proj, proj, proj, proj]
    in_specs = [_full(lb_logits), _full(ng), _full(mall), _full(masks)] + [col(a_col0 + t) for t in range(4)]
    if state is not None:
        args.append(state)
        in_specs.append(sspec)
    aliases = _alias_args(args, in_specs, y_prev)
    return pl.pallas_call(
        functools.partial(_hgrn_kernel, layer=layer, cn=cn, nchunk=LB // cn,
                          has_state=state is not None, aliased=y_prev is not None),
        grid=(B, ncb),
        in_specs=in_specs,
        out_specs=[yspec, sspec],
        out_shape=[yshape, sshape],
        scratch_shapes=[pltpu.VMEM((N_HEADS, HEAD_DIM, HEAD_DIM), F32)],
        input_output_aliases=aliases,
        compiler_params=_params("arbitrary", "arbitrary"),
        name="hgrn2",
    )(*args)


def _head_cols(mat, base):
    return [mat[:, base + h:base + h + 1] for h in range(N_HEADS)]


def _head_rows(mat, base):
    return [mat[base + h:base + h + 1, :] for h in range(N_HEADS)]


def _mlstm_kernel(*refs, cn, nchunk, has_state, aliased):
    refs = list(refs)
    gt_ref, gtt_ref, q_ref, k_ref, v_ref, o_ref = refs[:6]
    pos = 6
    if has_state:
        c0_ref, n0_ref, m0_ref = refs[pos:pos + 3]
        pos += 3
    if aliased:
        pos += 1
    y_ref, c_ref, n_ref, m_ref = refs[pos:pos + 4]
    j = pl.program_id(1)

    @pl.when(j == 0)
    def _():
        if has_state:
            c_ref[...] = c0_ref[...]
            n_ref[...] = n0_ref[...]
            m_ref[...] = m0_ref[...]
        else:
            c_ref[...] = jnp.zeros(c_ref.shape, F32)
            n_ref[...] = jnp.zeros(n_ref.shape, F32)
            m_ref[...] = jnp.zeros(m_ref.shape, F32)

    incl, _, _ = _tri_masks(cn)
    tril = incl.astype(BF16)
    triu = jnp.logical_not(_tri_masks(cn)[1]).astype(BF16)

    def chunk(c, carry):
        r0 = pl.multiple_of(c * cn, cn)
        rows = pl.ds(r0, cn)
        gates = gt_ref[rows, :]
        gates_r = gtt_ref[c]
        f_c = _dot_exact_l(tril, _log_sigmoid(gates))
        f_r = _dot_exact_r(_log_sigmoid(gates_r), triu)
        Fc = _head_cols(f_c, N_HEADS)
        ic = _head_cols(gates, 0)
        Fr = _head_rows(f_r, N_HEADS)
        ir = _head_rows(gates_r, 0)
        q = [q_ref[rows, cs].astype(BF16) for cs in HEADS]
        k = [k_ref[rows, cs] * (HEAD_DIM ** -0.5) for cs in HEADS]
        kb = [x.astype(BF16) for x in k]
        v = [v_ref[rows, cs].astype(BF16) for cs in HEADS]
        C = [c_ref[0, h] for h in range(N_HEADS)]
        n = [n_ref[0, h] for h in range(N_HEADS)]
        m = [m_ref[0, h][:, 0:1] for h in range(N_HEADS)]
        qk = [_dot_nt(a, b) for a, b in zip(q, kb)]
        qc = [_dot(a, b) for a, b in zip(q, C)]
        raw = [a - b + d for a, b, d in zip(Fc, Fr, ir)]
        inter = [a + b for a, b in zip(Fc, m)]
        m_t = [jnp.maximum(a, jnp.max(jnp.where(incl, r, NEG_BIG), axis=-1, keepdims=True))
               for a, r in zip(inter, raw)]
        w_inter = [jnp.exp(a - b) for a, b in zip(inter, m_t)]
        s = [jnp.where(incl, a * jnp.exp(jnp.where(incl, r - mt, 0.0)), 0.0)
             for a, r, mt in zip(qk, raw, m_t)]
        num = [_dot(a, b) + w * d for a, b, w, d in zip(s, v, w_inter, qc)]
        den = [jnp.sum(a, axis=-1, keepdims=True)
               + w * jnp.sum(q_ref[rows, cs] * nn, axis=-1, keepdims=True)
               for a, w, cs, nn in zip(s, w_inter, HEADS, n)]
        hh = [a / jnp.maximum(jnp.abs(d), jnp.exp(-mt)) for a, d, mt in zip(num, den, m_t)]
        for h, cs in enumerate(HEADS):
            m_new = m_t[h][cn - 1:cn]
            w_c = jnp.exp(inter[h][cn - 1:cn] - m_new)
            w_s = jnp.exp(Fc[h][cn - 1:cn] - Fc[h] + ic[h] - m_new)
            ks = k[h] * w_s
            c_ref[0, h] = w_c * C[h] + _dot_tn(ks, v[h])
            n_ref[0, h] = w_c * n[h] + jnp.sum(ks, axis=0, keepdims=True)
            m_ref[0, h] = jnp.broadcast_to(m_new, (1, HEAD_DIM))
            y_ref[rows, cs] = _sigmoid(o_ref[rows, cs]) * hh[h]
        return carry

    _run_chunks(chunk, nchunk)


def _gates_t(gates, row0, B, L, cn):
    g = gates[row0:row0 + B * L, :4 * N_HEADS]
    return g.reshape(B * L // cn, cn, 4 * N_HEADS).transpose(0, 2, 1)


def _mlstm(proj, b_col0, gates, row0, B, L, state, y_prev):
    cn, LB, ncb, rb0, col, yspec, yshape, cspec, cshape = _mixer_specs(proj.shape[0], row0, B, L, y_prev)
    nspec = pl.BlockSpec((1, N_HEADS, 1, HEAD_DIM), lambda b, j: (b, 0, 0, 0))
    nshape = jax.ShapeDtypeStruct((B, N_HEADS, 1, HEAD_DIM), F32)
    gt = _gates_t(gates, row0, B, L, cn)
    args = [gates, gt, proj, proj, proj, proj]
    in_specs = [pl.BlockSpec((LB, LANES), lambda b, j: (rb0 + b * ncb + j, 0)),
                pl.BlockSpec((LB // cn, 4 * N_HEADS, cn), lambda b, j: (b * ncb + j, 0, 0))]
    in_specs += [col(b_col0 + t) for t in range(4)]
    if state is not None:
        c0, n0, m0 = state
        args += [c0, n0.reshape(B, N_HEADS, 1, HEAD_DIM),
                 jnp.broadcast_to(m0[:, :, None, None], (B, N_HEADS, 1, HEAD_DIM))]
        in_specs += [cspec, nspec, nspec]
    aliases = _alias_args(args, in_specs, y_prev)
    y, c, n, m = pl.pallas_call(
        functools.partial(_mlstm_kernel, cn=cn, nchunk=LB // cn, has_state=state is not None,
                          aliased=y_prev is not None),
        grid=(B, ncb),
        in_specs=in_specs,
        out_specs=[yspec, cspec, nspec, nspec],
        out_shape=[yshape, cshape, nshape, nshape],
        input_output_aliases=aliases,
        compiler_params=_params("arbitrary", "arbitrary"),
        name="mlstm",
    )(*args)
    return y, c, n.reshape(B, N_HEADS, HEAD_DIM), m[:, :, 0, 0]


_CONV_PAD = 8


def _gdn_kernel(*refs, cn, nchunk, has_state, aliased):
    refs = list(refs)
    (gt_ref, gtt_ref, cw_ref, al_ref, dtb_ref, alc_ref, dtbc_ref, ng_ref,
     xq_ref, xk_ref, xv_ref, g_ref) = refs[:12]
    pos = 12
    if has_state:
        s0_ref, b0_ref = refs[pos:pos + 2]
        pos += 2
    if aliased:
        pos += 1
    y_ref, s_ref, buf_ref, xp_ref, cv_ref = refs[pos:pos + 5]
    j = pl.program_id(1)
    LB = xq_ref.shape[0]
    hist = CONV_W - 1
    h0 = _CONV_PAD - hist

    @pl.when(j == 0)
    def _():
        if has_state:
            s_ref[...] = s0_ref[...]
            xp_ref[h0:_CONV_PAD, :] = b0_ref[0]
        else:
            s_ref[...] = jnp.zeros(s_ref.shape, F32)
            xp_ref[h0:_CONV_PAD, :] = jnp.zeros((hist, 3 * WIDTH), F32)

    xp_ref[_CONV_PAD:_CONV_PAD + LB, 0:WIDTH] = xq_ref[...]
    xp_ref[_CONV_PAD:_CONV_PAD + LB, WIDTH:2 * WIDTH] = xk_ref[...]
    xp_ref[_CONV_PAD:_CONV_PAD + LB, 2 * WIDTH:3 * WIDTH] = xv_ref[...]
    acc = xp_ref[h0:h0 + LB, :] * cw_ref[0:1, :]
    for t in range(1, CONV_W):
        acc = acc + xp_ref[h0 + t:h0 + t + LB, :] * cw_ref[t:t + 1, :]
    cv_ref[...] = _silu(acc)
    new_hist = xp_ref[_CONV_PAD + LB - hist:_CONV_PAD + LB, :]
    xp_ref[h0:_CONV_PAD, :] = new_hist
    buf_ref[0] = new_hist

    incl, strict, eye = _tri_masks(cn)
    tril = incl.astype(BF16)
    triu = jnp.logical_not(strict).astype(BF16)
    eye_f = eye.astype(F32)
    lane = lax.broadcasted_iota(jnp.int32, (1, LANES), 1)
    is_a = (lane >= 2 * N_HEADS) & (lane < 3 * N_HEADS)
    sub = lax.broadcasted_iota(jnp.int32, (4 * N_HEADS, 1), 0)
    is_a_r = (sub >= 2 * N_HEADS) & (sub < 3 * N_HEADS)
    neg_rate = -jnp.exp(al_ref[...])
    dtb = dtb_ref[...]
    neg_rate_r = -jnp.exp(alc_ref[...])
    dtb_r = dtbc_ref[...]
    ng = ng_ref[...]
    nsq = int(math.log2(cn))

    def chunk(c, carry):
        r0 = pl.multiple_of(c * cn, cn)
        rows = pl.ds(r0, cn)
        gates = gt_ref[rows, :]
        gates_r = gtt_ref[c]
        g_c = jnp.where(is_a, neg_rate * _softplus(gates + dtb), 0.0)
        g_r = jnp.where(is_a_r, neg_rate_r * _softplus(gates_r + dtb_r), 0.0)
        gam_c = _head_cols(_dot_exact_l(tril, g_c), 2 * N_HEADS)
        gam_r = _head_rows(_dot_exact_r(g_r, triu), 2 * N_HEADS)
        beta = _head_cols(_sigmoid(gates), 3 * N_HEADS)
        q, k, v = [], [], []
        for cs in HEADS:
            qr = cv_ref[rows, cs]
            kr = cv_ref[rows, slice(WIDTH + cs.start, WIDTH + cs.stop)]
            v.append(cv_ref[rows, slice(2 * WIDTH + cs.start, 2 * WIDTH + cs.stop)])
            q.append(qr * (lax.rsqrt(jnp.sum(qr * qr, axis=-1, keepdims=True) + NORM_EPS) * (HEAD_DIM ** -0.5)))
            k.append(kr * lax.rsqrt(jnp.sum(kr * kr, axis=-1, keepdims=True) + NORM_EPS))
        kb = [x.astype(BF16) for x in k]
        S = [s_ref[0, h] for h in range(N_HEADS)]
        kk = [_dot_nt(a, a) for a in kb]
        qk = [_dot_nt(a, b) for a, b in zip(q, kb)]
        dec = [jnp.exp(jnp.where(incl, a - b, 0.0)) for a, b in zip(gam_c, gam_r)]
        pw = [jnp.where(strict, -(bt * a * d), 0.0) for bt, a, d in zip(beta, kk, dec)]
        t_inv = [eye_f + a for a in pw]
        for _ in range(nsq - 1):
            pw = [_dot3(a, a) for a in pw]
            t_inv = [t + _dot3(t, a) for t, a in zip(t_inv, pw)]
        eg = [jnp.exp(a) for a in gam_c]
        sol_v = [_dot3(t, bt * x) for t, bt, x in zip(t_inv, beta, v)]
        sol_k = [_dot3(t, (bt * e) * x) for t, bt, e, x in zip(t_inv, beta, eg, k)]
        u = [a - _dot(b, s) for a, b, s in zip(sol_v, sol_k, S)]
        o = [_dot(jnp.where(incl, a * d, 0.0), uu) + _dot(qq * e, s)
             for a, d, uu, qq, e, s in zip(qk, dec, u, q, eg, S)]
        for h, cs in enumerate(HEADS):
            g_last = gam_c[h][cn - 1:cn]
            s_ref[0, h] = jnp.exp(g_last) * S[h] + _dot_tn(k[h] * jnp.exp(g_last - gam_c[h]), u[h])
            y_ref[rows, cs] = _rms_gate(o[h], ng, _silu(g_ref[rows, cs]))
        return carry

    _run_chunks(chunk, nchunk)


def _gdn(proj, c_col0, gates, row0, B, L, conv_w_t, a_log, dt_bias, norm_g, state, y_prev):
    cn, LB, ncb, rb0, col, yspec, yshape, sspec, sshape = _mixer_specs(proj.shape[0], row0, B, L, y_prev)
    hist = CONV_W - 1
    bspec = pl.BlockSpec((1, hist, 3 * WIDTH), lambda b, j: (b, 0, 0))
    a0, a1 = 2 * N_HEADS, 3 * N_HEADS
    al = jnp.zeros((1, LANES), F32).at[0, a0:a1].set(a_log.astype(F32))
    db = jnp.zeros((1, LANES), F32).at[0, a0:a1].set(dt_bias.astype(F32))
    alc = al[0, :4 * N_HEADS].reshape(4 * N_HEADS, 1)
    dbc = db[0, :4 * N_HEADS].reshape(4 * N_HEADS, 1)
    ng = norm_g.reshape(1, HEAD_DIM).astype(F32)
    gt = _gates_t(gates, row0, B, L, cn)
    args = [gates, gt, conv_w_t, al, db, alc, dbc, ng, proj, proj, proj, proj]
    in_specs = [pl.BlockSpec((LB, LANES), lambda b, j: (rb0 + b * ncb + j, 0)),
                pl.BlockSpec((LB // cn, 4 * N_HEADS, cn), lambda b, j: (b * ncb + j, 0, 0)),
                _full(conv_w_t), _full(al), _full(db), _full(alc), _full(dbc), _full(ng)]
    in_specs += [col(c_col0 + t) for t in range(4)]
    if state is not None:
        args += [state[0], state[1]]
        in_specs += [sspec, bspec]
    aliases = _alias_args(args, in_specs, y_prev)
    return pl.pallas_call(
        functools.partial(_gdn_kernel, cn=cn, nchunk=LB // cn, has_state=state is not None,
                          aliased=y_prev is not None),
        grid=(B, ncb),
        in_specs=in_specs,
        out_specs=[yspec, sspec, bspec],
        out_shape=[yshape, sshape, jax.ShapeDtypeStruct((B, hist, 3 * WIDTH), F32)],
        scratch_shapes=[pltpu.VMEM((_CONV_PAD + LB, 3 * WIDTH), F32),
                        pltpu.VMEM((LB, 3 * WIDTH), F32)],
        input_output_aliases=aliases,
        compiler_params=_params("arbitrary", "arbitrary"),
        name="gdn",
    )(*args)


def _merge_kernel(ya_ref, yb_ref, yc_ref, ga_ref, gb_ref, gc_ref, wb_ref, o_ref, wbf_ref):
    @pl.when(pl.program_id(1) == 0)
    def _():
        wbf_ref[...] = wb_ref[...].astype(BF16)

    acc = None
    for jdx, (y_ref, g_ref) in enumerate(((ya_ref, ga_ref), (yb_ref, gb_ref), (yc_ref, gc_ref))):
        br = jnp.dot(y_ref[...].astype(BF16), wbf_ref[jdx], preferred_element_type=F32)
        term = g_ref[...].astype(F32) * br
        acc = term if acc is None else acc + term
    o_ref[...] = acc.astype(o_ref.dtype)


def _merge(ys, gate, w_branch, layer, tm=256, tn=1024):
    T, W = ys[0].shape
    D = w_branch.shape[-1]
    tn = min(tn, D)
    nb = D // tn
    yspec = pl.BlockSpec((tm, W), lambda n, m: (m, 0))
    gspec = lambda jdx: pl.BlockSpec((tm, tn), lambda n, m: (m, jdx * nb + n))
    return pl.pallas_call(
        _merge_kernel,
        grid=(nb, T // tm),
        in_specs=[yspec, yspec, yspec, gspec(0), gspec(1), gspec(2),
                  pl.BlockSpec((None, 3, W, tn), lambda n, m: (layer, 0, 0, n))],
        out_specs=pl.BlockSpec((tm, tn), lambda n, m: (m, n)),
        out_shape=jax.ShapeDtypeStruct((T, D), BF16),
        scratch_shapes=[pltpu.VMEM((3, W, tn), BF16)],
        compiler_params=_params("arbitrary", "arbitrary"),
        name="merge",
    )(ys[0], ys[1], ys[2], gate, gate, gate, w_branch)


def _moe_kernel(te_ref, nv_ref, tok_ref, dst_ref,
                h_hbm, roww_ref, wg_ref, wu_ref, wd_ref, out_hbm,
                xbuf, xb_ref, acc_ref, gsem, ssem, *, tm, n_real):
    r = pl.program_id(0)
    f = pl.program_id(1)
    nr = pl.num_programs(0)
    nf = pl.num_programs(1)
    nv = nv_ref[0]
    valid = r < nv

    def gather_start(tile):
        base = tile * tm

        def body(i, carry):
            tok = tok_ref[base + i]
            pltpu.make_async_copy(h_hbm.at[pl.ds(tok, 1)], xbuf.at[pl.ds(i, 1)], gsem).start()
            return carry

        lax.fori_loop(0, tm, body, 0, unroll=8)

    def scatter_start(tile):
        base = tile * tm

        def body(i, carry):
            dst = dst_ref[base + i]
            pltpu.make_async_copy(acc_ref.at[pl.ds(i, 1)], out_hbm.at[pl.ds(dst, 1)], ssem).start()
            return carry

        lax.fori_loop(0, tm, body, 0, unroll=8)

    def gather_wait():
        pltpu.make_async_copy(h_hbm.at[pl.ds(0, tm)], xbuf, gsem).wait()

    def scatter_wait():
        pltpu.make_async_copy(acc_ref, out_hbm.at[pl.ds(0, tm)], ssem).wait()

    @pl.when((f == 0) & (r == 0))
    def _():
        acc_ref[...] = jnp.zeros(acc_ref.shape, F32)
        for half in range(2):
            fill = pltpu.make_async_copy(acc_ref, out_hbm.at[pl.ds(n_real + half * tm, tm)], ssem)
            fill.start()
            fill.wait()
        gather_start(0)

    @pl.when((f == 0) & valid)
    def _():
        gather_wait()
        xb_ref[...] = xbuf[...].astype(BF16)

    @pl.when((f == 0) & (r + 1 < nv))
    def _():
        gather_start(r + 1)

    @pl.when(valid)
    def _():
        x = xb_ref[...]
        g = jnp.dot(x, wg_ref[...].astype(BF16), preferred_element_type=F32)
        u = jnp.dot(x, wu_ref[...].astype(BF16), preferred_element_type=F32)
        he = (_silu(g) * u).astype(BF16)
        wd = wd_ref[...].astype(BF16)

        @pl.when(f == 0)
        def _():
            @pl.when(r > 0)
            def _():
                scatter_wait()
            acc_ref[...] = jnp.dot(he, wd, preferred_element_type=F32)

        @pl.when(f > 0)
        def _():
            acc_ref[...] += jnp.dot(he, wd, preferred_element_type=F32)

        @pl.when(f == nf - 1)
        def _():
            acc_ref[...] = acc_ref[...] * roww_ref[:, 0:1]
            scatter_start(r)

    last_valid_step = (f == nf - 1) & (r == nv - 1)

    @pl.when(last_valid_step)
    def _():
        scatter_wait()


def _moe_plan(rinfo, T, tm):
    ids = rinfo[:, 0:TOP_K].astype(jnp.int32)
    wts = rinfo[:, TOP_K:2 * TOP_K]
    n_assign = TOP_K * T
    n_rows = n_assign + N_EXPERTS * tm
    n_tiles = n_rows // tm
    e_flat = ids.reshape(-1)
    onehot = (e_flat[:, None] == jnp.arange(N_EXPERTS, dtype=jnp.int32)[None, :]).astype(jnp.int32)
    rank = jnp.take_along_axis(jnp.cumsum(onehot, axis=0), e_flat[:, None], axis=1)[:, 0] - 1
    counts = jnp.sum(onehot, axis=0)
    padded = ((counts + tm - 1) // tm) * tm
    ends = jnp.cumsum(padded)
    starts = ends - padded
    dest = starts[e_flat] + rank
    tok = jnp.arange(n_assign, dtype=jnp.int32) // TOP_K
    slot = jnp.arange(n_assign, dtype=jnp.int32) % TOP_K
    p = jnp.arange(n_rows, dtype=jnp.int32)
    dump = n_assign + ((p // tm) % 2) * tm + (p % tm)
    row_tok = jnp.zeros((n_rows,), jnp.int32).at[dest].set(tok)
    row_dst = dump.at[dest].set(slot * T + tok)
    row_w = jnp.zeros((n_rows,), F32).at[dest].set(wts.reshape(-1))
    n_valid = (ends[-1] // tm).astype(jnp.int32)
    tile_start = jnp.arange(n_tiles, dtype=jnp.int32) * tm
    tile_e = jnp.minimum(jnp.searchsorted(ends, tile_start, side="right"), N_EXPERTS - 1).astype(jnp.int32)
    last_e = tile_e[jnp.maximum(n_valid - 1, 0)]
    tile_e = jnp.where(tile_start < ends[-1], tile_e, last_e)
    return tile_e, n_valid.reshape(1), row_tok, row_dst, jnp.broadcast_to(row_w[:, None], (n_rows, LANES))


def _moe(h2, rinfo, wg, wu, wd, layer, tm=MOE_TM, tf=MOE_TF):
    T, D = h2.shape
    FF = wg.shape[-1]
    nf = FF // tf
    tile_e, n_valid, row_tok, row_dst, row_w = _moe_plan(rinfo, T, tm)
    n_tiles = tile_e.shape[0]

    def f_eff(r, f, nv):
        return jnp.where(r < nv[0], f, nf - 1)

    grid_spec = pltpu.PrefetchScalarGridSpec(
        num_scalar_prefetch=4,
        grid=(n_tiles, nf),
        in_specs=[pl.BlockSpec(memory_space=pl.ANY),
                  pl.BlockSpec((tm, LANES), lambda r, f, te, nv, tok, dst: (r, 0)),
                  pl.BlockSpec((None, None, D, tf), lambda r, f, te, nv, tok, dst: (layer, te[r], 0, f_eff(r, f, nv))),
                  pl.BlockSpec((None, None, D, tf), lambda r, f, te, nv, tok, dst: (layer, te[r], 0, f_eff(r, f, nv))),
                  pl.BlockSpec((None, None, tf, D), lambda r, f, te, nv, tok, dst: (layer, te[r], f_eff(r, f, nv), 0))],
        out_specs=pl.BlockSpec(memory_space=pl.ANY),
        scratch_shapes=[pltpu.VMEM((tm, D), F32), pltpu.VMEM((tm, D), BF16), pltpu.VMEM((tm, D), F32),
                        pltpu.SemaphoreType.DMA(()), pltpu.SemaphoreType.DMA(())],
    )
    return pl.pallas_call(
        functools.partial(_moe_kernel, tm=tm, n_real=TOP_K * T),
        grid_spec=grid_spec,
        out_shape=jax.ShapeDtypeStruct((TOP_K * T + 2 * tm, D), F32),
        compiler_params=_params("arbitrary", "arbitrary"),
        name="moe_routed",
    )(tile_e, n_valid, row_tok, row_dst, h2, row_w, wg, wu, wd)


def kernel(x_prompt, x_sample, state_hgrn, state_mlstm_C, state_mlstm_n, state_mlstm_m, state_gdn, state_conv, c_prompt, c_sample, w_in, b_in, hgrn_lb_logits, hgrn_norm_g, gdn_conv_w, gdn_A_log, gdn_dt_bias, gdn_norm_g, w_branch, w_merge, b_merge, w_out, ln_mix_g, ln_mix_b, w_ada, b_ada, w_router, b_router, w_exp_gate, w_exp_up, w_exp_down, ln_ffn_g, ln_ffn_b):
    depth = w_in.shape[0]
    Bp, Lp, D = x_prompt.shape
    Bs, Ls, _ = x_sample.shape
    Tp, Ts = Bp * Lp, Bs * Ls
    T = Tp + Ts
    alpha = (2 * depth) ** 0.25
    tb = TOK_BLOCK
    assert Lp % tb == 0 and tb % Ls == 0 and Ts % tb == 0
    npb = Tp // tb
    spb = tb // Ls
    n_ab = 8 * WIDTH
    g0, c0 = n_ab, n_ab + 2 * N_HEADS
    c1 = c0 + 4 * WIDTH

    nc = Bp + Bs
    nc_pad = -(-nc // 16) * 16
    c_all = jnp.pad(jnp.concatenate([c_sample, c_prompt], axis=0), ((0, nc_pad - nc), (0, 0)))
    c_act = (c_all * jax.nn.sigmoid(c_all)).astype(BF16)

    wr = jnp.pad(w_router.astype(F32), ((0, 0), (0, LANES - N_EXPERTS)))
    br = jnp.pad(b_router.astype(F32), ((0, LANES - N_EXPERTS),)).reshape(1, LANES)
    mp_spec, ms_spec = _mod_specs(npb, Bs, Lp // tb, Bp, spb, D)
    tok_spec = pl.BlockSpec((tb, D), lambda i: (i, 0))
    vec_spec = pl.BlockSpec((None, 1, D), lambda i: (0, 0, 0))
    vec = lambda a, l: a[l].reshape(1, 1, D)
    zero_bias = jnp.zeros((1, D), F32)

    x = jnp.concatenate([x_prompt.reshape(Tp, D), x_sample.reshape(Ts, D)], axis=0)
    new_p, new_s = [], []
    for l in range(depth):
        mod = _matmul(c_act, w_ada, (None, D, 1024), lambda n: (l, 0, n),
                      b_ada.reshape(depth, 1, 6 * D), (None, 1, 1024), lambda n: (l, 0, n),
                      6 * D, F32, tm=nc_pad, tn=1024, name="ada").reshape(nc_pad, 6, D)

        h = pl.pallas_call(
            functools.partial(_modulate_kernel, npb=npb, spb=spb, sh_idx=0, sc_idx=1),
            grid=(T // tb,),
            in_specs=[tok_spec, mp_spec, ms_spec],
            out_specs=tok_spec,
            out_shape=jax.ShapeDtypeStruct((T, D), BF16),
            compiler_params=_params("arbitrary"),
            name="modulate",
        )(x, mod, mod)

        b_in3 = b_in.reshape(depth, 1, -1)
        proj_ab = _matmul(h, w_in, (None, D, 512), lambda n: (l, 0, n),
                          b_in3, (None, 1, 512), lambda n: (l, 0, n),
                          n_ab, F32, tm=1024, tn=512, name="proj_ab")
        proj_c = _matmul(h, w_in[l, :, c0:c1], (D, 512), lambda n: (0, n),
                         b_in[l, c0:c1].reshape(1, -1), (1, 512), lambda n: (0, n),
                         4 * WIDTH, F32, tm=1024, tn=512, name="proj_c")
        w_g = jnp.pad(jnp.concatenate([w_in[l, :, g0:c0], w_in[l, :, c1:]], axis=1),
                      ((0, 0), (0, LANES - 4 * N_HEADS)))
        b_g = jnp.pad(jnp.concatenate([b_in[l, g0:c0], b_in[l, c1:]]), ((0, LANES - 4 * N_HEADS),))
        gates = _matmul(h, w_g, (D, LANES), lambda n: (0, n), b_g.reshape(1, LANES), (1, LANES),
                        lambda n: (0, n), LANES, F32, tm=1024, tn=LANES, name="proj_gates")
        gate = _matmul(h, w_merge, (None, None, D, 512), lambda n: (l, n // (D // 512), 0, n % (D // 512)),
                       b_merge.reshape(depth, 3, 1, D), (None, None, 1, 512),
                       lambda n: (l, n // (D // 512), 0, n % (D // 512)),
                       3 * D, BF16, tm=1024, tn=512, name="merge_gates", act="sigmoid")

        conv_w_t = gdn_conv_w[l].T.astype(F32)
        y_init = jnp.zeros((T, WIDTH), F32)
        ya, sa_p = _hgrn(proj_ab, 0, 0, Bp, Lp, hgrn_lb_logits, hgrn_norm_g[l], l, None, y_init)
        ya, sa_s = _hgrn(proj_ab, 0, Tp, Bs, Ls, hgrn_lb_logits, hgrn_norm_g[l], l, state_hgrn[l], ya)
        yb, cb_p, nb_p, mb_p = _mlstm(proj_ab, 4, gates, 0, Bp, Lp, None, y_init)
        yb, cb_s, nb_s, mb_s = _mlstm(proj_ab, 4, gates, Tp, Bs, Ls,
                                      (state_mlstm_C[l], state_mlstm_n[l], state_mlstm_m[l]), yb)
        yc, sc_p, bc_p = _gdn(proj_c, 0, gates, 0, Bp, Lp, conv_w_t, gdn_A_log[l], gdn_dt_bias[l],
                              gdn_norm_g[l], None, y_init)
        yc, sc_s, bc_s = _gdn(proj_c, 0, gates, Tp, Bs, Ls, conv_w_t, gdn_A_log[l], gdn_dt_bias[l],
                              gdn_norm_g[l], (state_gdn[l], state_conv[l]), yc)
        new_p.append((sa_p, cb_p, nb_p, mb_p, sc_p, bc_p))
        new_s.append((sa_s, cb_s, nb_s, mb_s, sc_s, bc_s))

        merged = _merge((ya, yb, yc), gate, w_branch, l)
        mix = _matmul(merged, w_out, (None, D, 512), lambda n: (l, 0, n),
                      zero_bias, (1, 512), lambda n: (0, n), D, F32, tm=1024, tn=512, name="out_proj")

        x, h2, rinfo = pl.pallas_call(
            functools.partial(_post_route_kernel, npb=npb, spb=spb, alpha=alpha),
            grid=(T // tb,),
            in_specs=[tok_spec, tok_spec, mp_spec, ms_spec, vec_spec, vec_spec,
                      pl.BlockSpec((D, LANES), lambda i: (0, 0)), pl.BlockSpec((1, LANES), lambda i: (0, 0))],
            out_specs=[tok_spec, tok_spec, pl.BlockSpec((tb, LANES), lambda i: (i, 0))],
            out_shape=[jax.ShapeDtypeStruct((T, D), F32), jax.ShapeDtypeStruct((T, D), F32),
                       jax.ShapeDtypeStruct((T, LANES), F32)],
            compiler_params=_params("arbitrary"),
            name="post_route",
        )(x, mix, mod, mod, vec(ln_mix_g, l), vec(ln_mix_b, l), wr, br)

        ffn = _moe(h2, rinfo, w_exp_gate, w_exp_up, w_exp_down, l)
        x = pl.pallas_call(
            functools.partial(_post_final_kernel, npb=npb, spb=spb, alpha=alpha),
            grid=(T // tb,),
            in_specs=[tok_spec, tok_spec, pl.BlockSpec((tb, D), lambda i: (T // tb + i, 0)),
                      mp_spec, ms_spec, vec_spec, vec_spec],
            out_specs=tok_spec,
            out_shape=jax.ShapeDtypeStruct((T, D), F32),
            compiler_params=_params("arbitrary"),
            name="post_final",
        )(x, ffn, ffn, mod, mod, vec(ln_ffn_g, l), vec(ln_ffn_b, l))

    st_p = [jnp.stack(s, axis=0) for s in zip(*new_p)]
    st_s = [jnp.stack(s, axis=0) for s in zip(*new_s)]
    hgrn_p, mc_p, mn_p, mm_p, gdn_p, conv_p = st_p
    hgrn_s, mc_s, mn_s, mm_s, gdn_s, conv_s = st_s
    return (x[:Tp].reshape(Bp, Lp, D), x[Tp:].reshape(Bs, Ls, D), hgrn_p, hgrn_s, mc_p, mc_s,
            mn_p, mn_s, mm_p, mm_s, gdn_p, gdn_s, conv_p, conv_s)
```

```python
import functools
import math

import numpy as np
import jax
import jax.numpy as jnp
from jax import lax
from jax.experimental import pallas as pl
from jax.experimental.pallas import tpu as pltpu

F32 = jnp.float32
BF16 = jnp.bfloat16
HIGHEST = lax.Precision.HIGHEST

HEAD_DIM = 128
N_HEADS = 8
WIDTH = N_HEADS * HEAD_DIM
CHUNK = 64
CONV_W = 4
N_EXPERTS = 16
N_GROUPS = 4
EXPERTS_PER_GROUP = N_EXPERTS // N_GROUPS
TOP_K = 2
LN_EPS = 1e-5
NORM_EPS = 1e-6
LB_TINY = 1e-20
NEG_BIG = -1e30
LANES = 128
VMEM_LIMIT = 56 * 1024 * 1024
TOK_BLOCK = 256
MOE_TM = 512
MOE_TF = 256
HEADS = tuple(slice(h * HEAD_DIM, (h + 1) * HEAD_DIM) for h in range(N_HEADS))


def _params(*sem):
    return pltpu.CompilerParams(dimension_semantics=sem, vmem_limit_bytes=VMEM_LIMIT)


def _dot(a, b):
    return jnp.dot(a.astype(BF16), b.astype(BF16), preferred_element_type=F32)


def _dot_nt(a, b):
    return lax.dot_general(a.astype(BF16), b.astype(BF16), (((1,), (1,)), ((), ())),
                           preferred_element_type=F32)


def _dot_tn(a, b):
    return lax.dot_general(a.astype(BF16), b.astype(BF16), (((0,), (0,)), ((), ())),
                           preferred_element_type=F32)


def _dot_hi(a, b):
    return jnp.dot(a, b, precision=HIGHEST, preferred_element_type=F32)


def _split3(x):
    x1 = x.astype(BF16)
    r1 = x - x1.astype(F32)
    x2 = r1.astype(BF16)
    x3 = (r1 - x2.astype(F32)).astype(BF16)
    return x1, x2, x3


def _split2(x):
    x1 = x.astype(BF16)
    return x1, (x - x1.astype(F32)).astype(BF16)


def _dot_exact_l(a_bf, x):
    x1, x2, x3 = _split3(x)
    d = lambda p: jnp.dot(a_bf, p, preferred_element_type=F32)
    return d(x1) + (d(x2) + d(x3))


def _dot_exact_r(x, b_bf):
    x1, x2, x3 = _split3(x)
    d = lambda p: jnp.dot(p, b_bf, preferred_element_type=F32)
    return d(x1) + (d(x2) + d(x3))


def _dot3(a, b):
    a1, a2 = _split2(a)
    b1, b2 = _split2(b)
    d = lambda p, q: jnp.dot(p, q, preferred_element_type=F32)
    return d(a1, b1) + (d(a1, b2) + d(a2, b1))


def _sigmoid(x):
    return jax.nn.sigmoid(x)


def _silu(x):
    return x * jax.nn.sigmoid(x)


def _log_sigmoid(x):
    return jnp.minimum(x, 0.0) - jnp.log1p(jnp.exp(-jnp.abs(x)))


def _softplus(x):
    return jnp.maximum(x, 0.0) + jnp.log1p(jnp.exp(-jnp.abs(x)))


def _logaddexp(a, b):
    return jnp.maximum(a, b) + jnp.log1p(jnp.exp(-jnp.abs(a - b)))


def _tri_masks(cn):
    r = lax.broadcasted_iota(jnp.int32, (cn, cn), 0)
    c = lax.broadcasted_iota(jnp.int32, (cn, cn), 1)
    return r >= c, r > c, r == c


def _rms_gate(o, ng, gate):
    return o * lax.rsqrt(jnp.mean(o * o, axis=-1, keepdims=True) + NORM_EPS) * ng * gate


def _mm_kernel(x_ref, w_ref, b_ref, o_ref, wbf_ref, *, act, w_transposed):
    @pl.when(pl.program_id(1) == 0)
    def _():
        if w_transposed:
            wbf_ref[...] = w_ref[...].T.astype(BF16)
        else:
            wbf_ref[...] = w_ref[...].astype(BF16)

    acc = jnp.dot(x_ref[...], wbf_ref[...], preferred_element_type=F32) + b_ref[...]
    if act == "sigmoid":
        acc = _sigmoid(acc)
    o_ref[...] = acc.astype(o_ref.dtype)


def _matmul(x, w, w_block, w_index, b, b_block, b_index, n_out, out_dtype, *, tm, tn, name, act=None,
            w_transposed=False):
    M, K = x.shape
    tm = min(tm, M)
    assert M % tm == 0 and n_out % tn == 0, (M, tm, n_out, tn)
    return pl.pallas_call(
        functools.partial(_mm_kernel, act=act, w_transposed=w_transposed),
        grid=(n_out // tn, M // tm),
        in_specs=[pl.BlockSpec((tm, K), lambda n, m: (m, 0)),
                  pl.BlockSpec(w_block, lambda n, m: w_index(n)),
                  pl.BlockSpec(b_block, lambda n, m: b_index(n))],
        out_specs=pl.BlockSpec((tm, tn), lambda n, m: (m, n)),
        out_shape=jax.ShapeDtypeStruct((M, n_out), out_dtype),
        scratch_shapes=[pltpu.VMEM((K, tn), BF16)],
        compiler_params=_params("arbitrary", "arbitrary"),
        name=name,
    )(x, w, b)


def _with_mod(npb, mp_ref, ms_ref, body):
    i = pl.program_id(0)

    @pl.when(i < npb)
    def _():
        body(mp_ref[...])

    @pl.when(i >= npb)
    def _():
        body(ms_ref[...])


def _mod_specs(npb, n_sample_seq, blocks_per_prompt_seq, n_prompt_seq, spb, D):
    mp = pl.BlockSpec((1, 6, D), lambda i: (n_sample_seq + jnp.minimum(i // blocks_per_prompt_seq,
                                                                        n_prompt_seq - 1), 0, 0))
    ms = pl.BlockSpec((spb, 6, D), lambda i: (jnp.maximum(i - npb, 0), 0, 0))
    return mp, ms


def _modulate_kernel(x_ref, mp_ref, ms_ref, o_ref, *, npb, spb, sh_idx, sc_idx):
    tb, D = x_ref.shape

    def body(m):
        x = x_ref[...].reshape(spb, tb // spb, D)
        h = x * (1.0 + m[:, sc_idx:sc_idx + 1, :]) + m[:, sh_idx:sh_idx + 1, :]
        o_ref[...] = h.reshape(tb, D).astype(o_ref.dtype)

    _with_mod(npb, mp_ref, ms_ref, body)


def _layer_norm(z, g, b):
    mu = jnp.mean(z, axis=-1, keepdims=True)
    zc = z - mu
    var = jnp.mean(zc * zc, axis=-1, keepdims=True)
    return zc * lax.rsqrt(var + LN_EPS) * g + b


def _route(logits):
    lane = lax.broadcasted_iota(jnp.int32, logits.shape, 1)
    valid = lane < N_EXPERTS
    lg = jnp.where(valid, logits, NEG_BIG)
    ex = jnp.where(valid, jnp.exp(lg - jnp.max(lg, axis=-1, keepdims=True)), 0.0)
    probs = ex / jnp.sum(ex, axis=-1, keepdims=True)
    big = LANES
    best = None
    for g in range(N_GROUPS):
        in_g = (lane >= g * EXPERTS_PER_GROUP) & (lane < (g + 1) * EXPERTS_PER_GROUP)
        m1 = jnp.max(jnp.where(in_g, probs, -1.0), axis=-1, keepdims=True)
        i1 = jnp.min(jnp.where(in_g & (probs == m1), lane, big), axis=-1, keepdims=True)
        rest = in_g & (lane != i1)
        m2 = jnp.max(jnp.where(rest, probs, -1.0), axis=-1, keepdims=True)
        i2 = jnp.min(jnp.where(rest & (probs == m2), lane, big), axis=-1, keepdims=True)
        cand = (m1 + m2, i1, i2, m1, m2)
        if best is None:
            best = cand
        else:
            better = cand[0] > best[0]
            best = tuple(jnp.where(better, c, o) for c, o in zip(cand, best))
    _, i1, i2, m1, m2 = best
    tot = m1 + m2
    return (jnp.where(lane == 0, i1.astype(F32), 0.0) + jnp.where(lane == 1, i2.astype(F32), 0.0)
            + jnp.where(lane == 2, m1 / tot, 0.0) + jnp.where(lane == 3, m2 / tot, 0.0))


def _post_route_kernel(x_ref, y_ref, mp_ref, ms_ref, g_ref, b_ref, wr_ref, br_ref,
                       xo_ref, h_ref, r_ref, *, npb, spb, alpha):
    tb, D = x_ref.shape

    def body(m):
        x = x_ref[...].reshape(spb, tb // spb, D)
        y = y_ref[...].reshape(spb, tb // spb, D)
        xn = _layer_norm(alpha * x + m[:, 2:3, :] * y, g_ref[...], b_ref[...])
        xo_ref[...] = xn.reshape(tb, D)
        h2 = (xn * (1.0 + m[:, 4:5, :]) + m[:, 3:4, :]).reshape(tb, D)
        h_ref[...] = h2
        r_ref[...] = _route(_dot_hi(h2, wr_ref[...]) + br_ref[...])

    _with_mod(npb, mp_ref, ms_ref, body)


def _post_final_kernel(x_ref, y0_ref, y1_ref, mp_ref, ms_ref, g_ref, b_ref, xo_ref, *, npb, spb, alpha):
    tb, D = x_ref.shape

    def body(m):
        x = x_ref[...].reshape(spb, tb // spb, D)
        y = (y0_ref[...] + y1_ref[...]).reshape(spb, tb // spb, D)
        xn = _layer_norm(alpha * x + m[:, 5:6, :] * y, g_ref[...], b_ref[...])
        xo_ref[...] = xn.reshape(tb, D)

    _with_mod(npb, mp_ref, ms_ref, body)


def _seq_blocking(L):
    cn = math.gcd(L, CHUNK)
    lb = min(L, 512)
    assert L % lb == 0 and lb % cn == 0
    return cn, lb, L // lb


def _run_chunks(chunk, nchunk):
    if nchunk == 1:
        chunk(0, 0)
    else:
        lax.fori_loop(0, nchunk, chunk, 0)


def _hgrn_tables(cn):
    nlev = int(math.log2(cn))
    assert 1 << nlev == cn
    r = np.arange(cn)
    mats = [np.tril(np.ones((cn, cn)))]
    masks = [np.eye(cn)]
    j = r[None, :]
    for lev in range(nlev):
        s = cn >> (lev + 1)
        blk = 2 * s
        pos = r % blk
        ref = r - pos + s - 1
        second = pos >= s
        m_second = (j > ref[:, None]) & (j <= r[:, None])
        m_first = (j > r[:, None]) & (j <= ref[:, None])
        mats.append(np.where(second[:, None], m_second, m_first))
        same = (r[:, None] // blk) == (r[None, :] // blk)
        masks.append(same & second[:, None] & (~second)[None, :])
    return (np.concatenate(mats, 0).astype(np.float32), np.stack(masks).astype(np.float32))


def _hgrn_kernel(*refs, layer, cn, nchunk, has_state, n_alias):
    refs = list(refs)
    lbl_ref, ng_ref, mall_ref, masks_ref, q_ref, f_ref, i_ref, g_ref = refs[:8]
    pos = 8
    s0_ref = None
    if has_state:
        s0_ref = refs[pos]
        pos += 1
    pos += n_alias
    y_ref, s_ref, st_ref = refs[pos:pos + 3]
    j = pl.program_id(1)
    nlev = masks_ref.shape[0] - 1

    @pl.when(j == 0)
    def _():
        for h in range(N_HEADS):
            if has_state:
                st_ref[h] = s0_ref[0, h].T
            else:
                st_ref[h] = jnp.zeros((HEAD_DIM, HEAD_DIM), F32)

    logits = lbl_ref[...]
    e = jnp.exp(logits - jnp.max(logits, axis=0, keepdims=True))
    p = e / jnp.sum(e, axis=0, keepdims=True)
    lb = jnp.zeros((1, WIDTH), F32)
    for l in range(1, layer + 1):
        lb = lb + p[l:l + 1, :]
    log_lb = jnp.log(jnp.maximum(lb, LB_TINY))
    log1m_lb = jnp.log1p(-lb)
    one_m_lb = 1.0 - lb
    ng = ng_ref[...]
    mall = mall_ref[...]

    def chunk(c, carry):
        r0 = pl.multiple_of(c * cn, cn)
        rows = pl.ds(r0, cn)
        q = _silu(q_ref[rows, :])
        fr = f_ref[rows, :]
        v = i_ref[rows, :].astype(BF16)
        log_f = _logaddexp(log_lb, log1m_lb + _log_sigmoid(fr))
        k = one_m_lb * _sigmoid(-fr)
        dall = _dot_exact_l(mall, log_f)
        b = dall[:cn]
        qb = q.astype(BF16)
        kb = k.astype(BF16)
        att = [masks_ref[0] * _dot_nt(qb[:, cs], kb[:, cs]) for cs in HEADS]
        for lev in range(1, nlev + 1):
            ex = jnp.exp(dall[lev * cn:(lev + 1) * cn])
            qe = (q * ex).astype(BF16)
            ke = (k * ex).astype(BF16)
            att = [a + masks_ref[lev] * _dot_nt(qe[:, cs], ke[:, cs]) for a, cs in zip(att, HEADS)]
        st = [st_ref[h] for h in range(N_HEADS)]
        qd = (q * jnp.exp(b)).astype(BF16)
        o = [_dot(a, v[:, cs]) + _dot_nt(qd[:, cs], s) for a, cs, s in zip(att, HEADS, st)]
        b_last = b[cn - 1:cn]
        k_dec = (k * jnp.exp(b_last - b)).astype(BF16)
        dec = jnp.exp(b_last)
        for h, cs in enumerate(HEADS):
            st_ref[h] = st[h] * dec[:, cs] + _dot_tn(v[:, cs], k_dec[:, cs])
        gate = _sigmoid(g_ref[rows, :])
        for h, cs in enumerate(HEADS):
            y_ref[rows, cs] = _rms_gate(o[h], ng[:, cs], gate[:, cs])
        return carry

    _run_chunks(chunk, nchunk)

    @pl.when(j == pl.num_programs(1) - 1)
    def _():
        for h in range(N_HEADS):
            s_ref[0, h] = st_ref[h].T


def _mixer_specs(T_total, row0, L):
    cn, LB, ncb = _seq_blocking(L)
    assert row0 % LB == 0
    rb0 = row0 // LB
    col = lambda cidx: pl.BlockSpec((LB, WIDTH), lambda b, j: (rb0 + b * ncb + j, cidx))
    yspec = pl.BlockSpec((LB, WIDTH), lambda b, j: (rb0 + b * ncb + j, 0))
    return cn, LB, ncb, rb0, col, yspec


def _state_spec(layer, trailing):
    return pl.BlockSpec((None, 1) + trailing, lambda b, j: (layer, b) + (0,) * len(trailing))


_STATE_DIMS = (N_HEADS, HEAD_DIM, HEAD_DIM)
_VEC_DIMS = (N_HEADS, 1, HEAD_DIM)


def _full(a):
    return pl.BlockSpec(a.shape, lambda b, j: (0,) * a.ndim)


def _alias_outputs(args, in_specs, prevs):
    aliases = {}
    for k, prev in enumerate(prevs):
        args.append(prev)
        in_specs.append(pl.BlockSpec(memory_space=pl.ANY))
        aliases[len(args) - 1] = k
    return aliases


def _shapes(prevs):
    return [jax.ShapeDtypeStruct(p.shape, p.dtype) for p in prevs]


def _hgrn(proj, a_col0, row0, B, L, lb_logits, norm_g, layer, state, prevs):
    cn, LB, ncb, rb0, col, yspec = _mixer_specs(proj.shape[0], row0, L)
    mall, masks = _hgrn_tables(cn)
    mall = jnp.asarray(mall, BF16)
    masks = jnp.asarray(masks)
    ng = jnp.tile(norm_g.reshape(1, HEAD_DIM).astype(F32), (1, N_HEADS))
    sspec = _state_spec(layer, _STATE_DIMS)
    args = [lb_logits, ng, mall, masks, proj, proj, proj, proj]
    in_specs = [_full(lb_logits), _full(ng), _full(mall), _full(masks)] + [col(a_col0 + t) for t in range(4)]
    if state is not None:
        args.append(state)
        in_specs.append(sspec)
    aliases = _alias_outputs(args, in_specs, prevs)
    return pl.pallas_call(
        functools.partial(_hgrn_kernel, layer=layer, cn=cn, nchunk=LB // cn,
                          has_state=state is not None, n_alias=len(prevs)),
        grid=(B, ncb),
        in_specs=in_specs,
        out_specs=[yspec, sspec],
        out_shape=_shapes(prevs),
        scratch_shapes=[pltpu.VMEM((N_HEADS, HEAD_DIM, HEAD_DIM), F32)],
        input_output_aliases=aliases,
        compiler_params=_params("arbitrary", "arbitrary"),
        name="hgrn2",
    )(*args)


def _head_cols(mat, base):
    return [mat[:, base + h:base + h + 1] for h in range(N_HEADS)]


def _head_rows(mat, base):
    return [mat[base + h:base + h + 1, :] for h in range(N_HEADS)]


def _mlstm_kernel(*refs, cn, nchunk, has_state, n_alias):
    refs = list(refs)
    gt_ref, gtt_ref, q_ref, k_ref, v_ref, o_ref = refs[:6]
    pos = 6
    if has_state:
        c0_ref, n0_ref, m0_ref = refs[pos:pos + 3]
        pos += 3
    pos += n_alias
    y_ref, c_ref, n_ref, m_ref = refs[pos:pos + 4]
    j = pl.program_id(1)

    @pl.when(j == 0)
    def _():
        if has_state:
            c_ref[...] = c0_ref[...]
            n_ref[...] = n0_ref[...]
            m_ref[...] = m0_ref[...]
        else:
            c_ref[...] = jnp.zeros(c_ref.shape, F32)
            n_ref[...] = jnp.zeros(n_ref.shape, F32)
            m_ref[...] = jnp.zeros(m_ref.shape, F32)

    incl, _, _ = _tri_masks(cn)
    tril = incl.astype(BF16)
    triu = jnp.logical_not(_tri_masks(cn)[1]).astype(BF16)

    def chunk(c, carry):
        r0 = pl.multiple_of(c * cn, cn)
        rows = pl.ds(r0, cn)
        gates = gt_ref[rows, :]
        gates_r = gtt_ref[c]
        f_c = _dot_exact_l(tril, _log_sigmoid(gates))
        f_r = _dot_exact_r(_log_sigmoid(gates_r), triu)
        Fc = _head_cols(f_c, N_HEADS)
        ic = _head_cols(gates, 0)
        Fr = _head_rows(f_r, N_HEADS)
        ir = _head_rows(gates_r, 0)
        q = [q_ref[rows, cs].astype(BF16) for cs in HEADS]
        k = [k_ref[rows, cs] * (HEAD_DIM ** -0.5) for cs in HEADS]
        kb = [x.astype(BF16) for x in k]
        v = [v_ref[rows, cs].astype(BF16) for cs in HEADS]
        C = [c_ref[0, h] for h in range(N_HEADS)]
        n = [n_ref[0, h] for h in range(N_HEADS)]
        m = [m_ref[0, h][:, 0:1] for h in range(N_HEADS)]
        qk = [_dot_nt(a, b) for a, b in zip(q, kb)]
        qc = [_dot(a, b) for a, b in zip(q, C)]
        raw = [a - b + d for a, b, d in zip(Fc, Fr, ir)]
        inter = [a + b for a, b in zip(Fc, m)]
        m_t = [jnp.maximum(a, jnp.max(jnp.where(incl, r, NEG_BIG), axis=-1, keepdims=True))
               for a, r in zip(inter, raw)]
        w_inter = [jnp.exp(a - b) for a, b in zip(inter, m_t)]
        s = [jnp.where(incl, a * jnp.exp(jnp.where(incl, r - mt, 0.0)), 0.0)
             for a, r, mt in zip(qk, raw, m_t)]
        num = [_dot(a, b) + w * d for a, b, w, d in zip(s, v, w_inter, qc)]
        den = [jnp.sum(a, axis=-1, keepdims=True)
               + w * jnp.sum(q_ref[rows, cs] * nn, axis=-1, keepdims=True)
               for a, w, cs, nn in zip(s, w_inter, HEADS, n)]
        hh = [a / jnp.maximum(jnp.abs(d), jnp.exp(-mt)) for a, d, mt in zip(num, den, m_t)]
        for h, cs in enumerate(HEADS):
            m_new = m_t[h][cn - 1:cn]
            w_c = jnp.exp(inter[h][cn - 1:cn] - m_new)
            w_s = jnp.exp(Fc[h][cn - 1:cn] - Fc[h] + ic[h] - m_new)
            ks = k[h] * w_s
            c_ref[0, h] = w_c * C[h] + _dot_tn(ks, v[h])
            n_ref[0, h] = w_c * n[h] + jnp.sum(ks, axis=0, keepdims=True)
            m_ref[0, h] = jnp.broadcast_to(m_new, (1, HEAD_DIM))
            y_ref[rows, cs] = _sigmoid(o_ref[rows, cs]) * hh[h]
        return carry

    _run_chunks(chunk, nchunk)


def _gates_t(gates, row0, B, L, cn):
    g = gates[row0:row0 + B * L, :4 * N_HEADS]
    return g.reshape(B * L // cn, cn, 4 * N_HEADS).transpose(0, 2, 1)


def _mlstm(proj, b_col0, gates, row0, B, L, layer, state, prevs):
    cn, LB, ncb, rb0, col, yspec = _mixer_specs(proj.shape[0], row0, L)
    cspec = _state_spec(layer, _STATE_DIMS)
    nspec = _state_spec(layer, _VEC_DIMS)
    gt = _gates_t(gates, row0, B, L, cn)
    args = [gates, gt, proj, proj, proj, proj]
    in_specs = [pl.BlockSpec((LB, LANES), lambda b, j: (rb0 + b * ncb + j, 0)),
                pl.BlockSpec((LB // cn, 4 * N_HEADS, cn), lambda b, j: (b * ncb + j, 0, 0))]
    in_specs += [col(b_col0 + t) for t in range(4)]
    if state is not None:
        args += list(state)
        in_specs += [cspec, nspec, nspec]
    aliases = _alias_outputs(args, in_specs, prevs)
    return pl.pallas_call(
        functools.partial(_mlstm_kernel, cn=cn, nchunk=LB // cn, has_state=state is not None,
                          n_alias=len(prevs)),
        grid=(B, ncb),
        in_specs=in_specs,
        out_specs=[yspec, cspec, nspec, nspec],
        out_shape=_shapes(prevs),
        input_output_aliases=aliases,
        compiler_params=_params("arbitrary", "arbitrary"),
        name="mlstm",
    )(*args)


_CONV_PAD = 8


def _gdn_kernel(*refs, cn, nchunk, has_state, n_alias):
    refs = list(refs)
    (gt_ref, gtt_ref, cw_ref, al_ref, dtb_ref, alc_ref, dtbc_ref, ng_ref,
     xq_ref, xk_ref, xv_ref, g_ref) = refs[:12]
    pos = 12
    if has_state:
        s0_ref, b0_ref = refs[pos:pos + 2]
        pos += 2
    pos += n_alias
    y_ref, s_ref, buf_ref, xp_ref, cv_ref = refs[pos:pos + 5]
    j = pl.program_id(1)
    LB = xq_ref.shape[0]
    hist = CONV_W - 1
    h0 = _CONV_PAD - hist

    @pl.when(j == 0)
    def _():
        if has_state:
            s_ref[...] = s0_ref[...]
            xp_ref[h0:_CONV_PAD, :] = b0_ref[0]
        else:
            s_ref[...] = jnp.zeros(s_ref.shape, F32)
            xp_ref[h0:_CONV_PAD, :] = jnp.zeros((hist, 3 * WIDTH), F32)

    xp_ref[_CONV_PAD:_CONV_PAD + LB, 0:WIDTH] = xq_ref[...]
    xp_ref[_CONV_PAD:_CONV_PAD + LB, WIDTH:2 * WIDTH] = xk_ref[...]
    xp_ref[_CONV_PAD:_CONV_PAD + LB, 2 * WIDTH:3 * WIDTH] = xv_ref[...]
    acc = xp_ref[h0:h0 + LB, :] * cw_ref[0:1, :]
    for t in range(1, CONV_W):
        acc = acc + xp_ref[h0 + t:h0 + t + LB, :] * cw_ref[t:t + 1, :]
    cv_ref[...] = _silu(acc)
    new_hist = xp_ref[_CONV_PAD + LB - hist:_CONV_PAD + LB, :]
    xp_ref[h0:_CONV_PAD, :] = new_hist
    buf_ref[0] = new_hist

    incl, strict, eye = _tri_masks(cn)
    tril = incl.astype(BF16)
    triu = jnp.logical_not(strict).astype(BF16)
    eye_f = eye.astype(F32)
    lane = lax.broadcasted_iota(jnp.int32, (1, LANES), 1)
    is_a = (lane >= 2 * N_HEADS) & (lane < 3 * N_HEADS)
    sub = lax.broadcasted_iota(jnp.int32, (4 * N_HEADS, 1), 0)
    is_a_r = (sub >= 2 * N_HEADS) & (sub < 3 * N_HEADS)
    neg_rate = -jnp.exp(al_ref[...])
    dtb = dtb_ref[...]
    neg_rate_r = -jnp.exp(alc_ref[...])
    dtb_r = dtbc_ref[...]
    ng = ng_ref[...]
    nsq = int(math.log2(cn))

    def chunk(c, carry):
        r0 = pl.multiple_of(c * cn, cn)
        rows = pl.ds(r0, cn)
        gates = gt_ref[rows, :]
        gates_r = gtt_ref[c]
        g_c = jnp.where(is_a, neg_rate * _softplus(gates + dtb), 0.0)
        g_r = jnp.where(is_a_r, neg_rate_r * _softplus(gates_r + dtb_r), 0.0)
        gam_c = _head_cols(_dot_exact_l(tril, g_c), 2 * N_HEADS)
        gam_r = _head_rows(_dot_exact_r(g_r, triu), 2 * N_HEADS)
        beta = _head_cols(_sigmoid(gates), 3 * N_HEADS)
        q, k, v = [], [], []
        for cs in HEADS:
            qr = cv_ref[rows, cs]
            kr = cv_ref[rows, slice(WIDTH + cs.start, WIDTH + cs.stop)]
            v.append(cv_ref[rows, slice(2 * WIDTH + cs.start, 2 * WIDTH + cs.stop)])
            q.append(qr * (lax.rsqrt(jnp.sum(qr * qr, axis=-1, keepdims=True) + NORM_EPS) * (HEAD_DIM ** -0.5)))
            k.append(kr * lax.rsqrt(jnp.sum(kr * kr, axis=-1, keepdims=True) + NORM_EPS))
        kb = [x.astype(BF16) for x in k]
        S = [s_ref[0, h] for h in range(N_HEADS)]
        kk = [_dot_nt(a, a) for a in kb]
        qk = [_dot_nt(a, b) for a, b in zip(q, kb)]
        dec = [jnp.exp(jnp.where(incl, a - b, 0.0)) for a, b in zip(gam_c, gam_r)]
        pw = [jnp.where(strict, -(bt * a * d), 0.0) for bt, a, d in zip(beta, kk, dec)]
        t_inv = [eye_f + a for a in pw]
        for _ in range(nsq - 1):
            pw = [_dot3(a, a) for a in pw]
            t_inv = [t + _dot3(t, a) for t, a in zip(t_inv, pw)]
        eg = [jnp.exp(a) for a in gam_c]
        sol_v = [_dot3(t, bt * x) for t, bt, x in zip(t_inv, beta, v)]
        sol_k = [_dot3(t, (bt * e) * x) for t, bt, e, x in zip(t_inv, beta, eg, k)]
        u = [a - _dot(b, s) for a, b, s in zip(sol_v, sol_k, S)]
        o = [_dot(jnp.where(incl, a * d, 0.0), uu) + _dot(qq * e, s)
             for a, d, uu, qq, e, s in zip(qk, dec, u, q, eg, S)]
        for h, cs in enumerate(HEADS):
            g_last = gam_c[h][cn - 1:cn]
            s_ref[0, h] = jnp.exp(g_last) * S[h] + _dot_tn(k[h] * jnp.exp(g_last - gam_c[h]), u[h])
            y_ref[rows, cs] = _rms_gate(o[h], ng, _silu(g_ref[rows, cs]))
        return carry

    _run_chunks(chunk, nchunk)


def _gdn(proj, c_col0, gates, row0, B, L, conv_w_t, a_log, dt_bias, norm_g, layer, state, prevs):
    cn, LB, ncb, rb0, col, yspec = _mixer_specs(proj.shape[0], row0, L)
    hist = CONV_W - 1
    sspec = _state_spec(layer, _STATE_DIMS)
    bspec = _state_spec(layer, (hist, 3 * WIDTH))
    a0, a1 = 2 * N_HEADS, 3 * N_HEADS
    al = jnp.zeros((1, LANES), F32).at[0, a0:a1].set(a_log.astype(F32))
    db = jnp.zeros((1, LANES), F32).at[0, a0:a1].set(dt_bias.astype(F32))
    alc = al[0, :4 * N_HEADS].reshape(4 * N_HEADS, 1)
    dbc = db[0, :4 * N_HEADS].reshape(4 * N_HEADS, 1)
    ng = norm_g.reshape(1, HEAD_DIM).astype(F32)
    gt = _gates_t(gates, row0, B, L, cn)
    args = [gates, gt, conv_w_t, al, db, alc, dbc, ng, proj, proj, proj, proj]
    in_specs = [pl.BlockSpec((LB, LANES), lambda b, j: (rb0 + b * ncb + j, 0)),
                pl.BlockSpec((LB // cn, 4 * N_HEADS, cn), lambda b, j: (b * ncb + j, 0, 0)),
                _full(conv_w_t), _full(al), _full(db), _full(alc), _full(dbc), _full(ng)]
    in_specs += [col(c_col0 + t) for t in range(4)]
    if state is not None:
        args += [state[0], state[1]]
        in_specs += [sspec, bspec]
    aliases = _alias_outputs(args, in_specs, prevs)
    return pl.pallas_call(
        functools.partial(_gdn_kernel, cn=cn, nchunk=LB // cn, has_state=state is not None,
                          n_alias=len(prevs)),
        grid=(B, ncb),
        in_specs=in_specs,
        out_specs=[yspec, sspec, bspec],
        out_shape=_shapes(prevs),
        scratch_shapes=[pltpu.VMEM((_CONV_PAD + LB, 3 * WIDTH), F32),
                        pltpu.VMEM((LB, 3 * WIDTH), F32)],
        input_output_aliases=aliases,
        compiler_params=_params("arbitrary", "arbitrary"),
        name="gdn",
    )(*args)


def _merge_kernel(ya_ref, yb_ref, yc_ref, ga_ref, gb_ref, gc_ref, wb_ref, o_ref, wbf_ref):
    @pl.when(pl.program_id(1) == 0)
    def _():
        wbf_ref[...] = wb_ref[...].astype(BF16)

    acc = None
    for jdx, (y_ref, g_ref) in enumerate(((ya_ref, ga_ref), (yb_ref, gb_ref), (yc_ref, gc_ref))):
        br = jnp.dot(y_ref[...].astype(BF16), wbf_ref[jdx], preferred_element_type=F32)
        term = g_ref[...].astype(F32) * br
        acc = term if acc is None else acc + term
    o_ref[...] = acc.astype(o_ref.dtype)


def _merge(ys, gate, w_branch, layer, tm=256, tn=1024):
    T, W = ys[0].shape
    D = w_branch.shape[-1]
    tn = min(tn, D)
    nb = D // tn
    yspec = pl.BlockSpec((tm, W), lambda n, m: (m, 0))
    gspec = lambda jdx: pl.BlockSpec((tm, tn), lambda n, m: (m, jdx * nb + n))
    return pl.pallas_call(
        _merge_kernel,
        grid=(nb, T // tm),
        in_specs=[yspec, yspec, yspec, gspec(0), gspec(1), gspec(2),
                  pl.BlockSpec((None, 3, W, tn), lambda n, m: (layer, 0, 0, n))],
        out_specs=pl.BlockSpec((tm, tn), lambda n, m: (m, n)),
        out_shape=jax.ShapeDtypeStruct((T, D), BF16),
        scratch_shapes=[pltpu.VMEM((3, W, tn), BF16)],
        compiler_params=_params("arbitrary", "arbitrary"),
        name="merge",
    )(ys[0], ys[1], ys[2], gate, gate, gate, w_branch)


def _moe_kernel(te_ref, nv_ref, tok_ref, dst_ref,
                h_hbm, roww_ref, wg_ref, wu_ref, wd_ref, out_hbm,
                xbuf, xb_ref, acc_ref, gsem, ssem, *, tm, nf, n_real):
    r = pl.program_id(0)
    f = pl.program_id(1)
    nv = nv_ref[0]
    valid = r < nv
    assert nf >= 2 and tm % nf == 0
    gather_rows = tm // nf
    scatter_rows = min(tm, 128)

    def gather_start(tile):
        base = tile * tm

        def body(i, carry):
            tok = tok_ref[base + i]
            pltpu.make_async_copy(h_hbm.at[pl.ds(tok, 1)], xbuf.at[pl.ds(i, 1)], gsem).start()
            return carry

        lax.fori_loop(0, tm, body, 0, unroll=8)

    def gather_wait():
        pltpu.make_async_copy(h_hbm.at[pl.ds(0, tm)], xbuf, gsem).wait()

    def scatter_wait():
        pltpu.make_async_copy(acc_ref, out_hbm.at[pl.ds(0, tm)], ssem).wait()

    @pl.when((f == 0) & (r == 0))
    def _():
        acc_ref[...] = jnp.zeros(acc_ref.shape, F32)
        for half in range(2):
            fill = pltpu.make_async_copy(acc_ref, out_hbm.at[pl.ds(n_real + half * tm, tm)], ssem)
            fill.start()
            fill.wait()
        gather_start(0)

    @pl.when((f == 0) & valid)
    def _():
        gather_wait()
        xb_ref[...] = xbuf[...].astype(BF16)

    @pl.when(valid)
    def _():
        nxt = jnp.minimum(r + 1, nv - 1)
        g0 = f * gather_rows
        for i in range(gather_rows):
            tok = tok_ref[nxt * tm + g0 + i]
            pltpu.make_async_copy(h_hbm.at[pl.ds(tok, 1)], xbuf.at[pl.ds(g0 + i, 1)], gsem).start()

        x = xb_ref[...]
        g = jnp.dot(x, wg_ref[...].astype(BF16), preferred_element_type=F32)
        u = jnp.dot(x, wu_ref[...].astype(BF16), preferred_element_type=F32)
        he = (_silu(g) * u).astype(BF16)
        wd = wd_ref[...].astype(BF16)

        @pl.when(f == 0)
        def _():
            @pl.when(r > 0)
            def _():
                scatter_wait()
            acc_ref[...] = jnp.dot(he, wd, preferred_element_type=F32)

        @pl.when((f > 0) & (f < nf - 1))
        def _():
            acc_ref[...] += jnp.dot(he, wd, preferred_element_type=F32)

        @pl.when(f == nf - 1)
        def _():
            for c in range(tm // scatter_rows):
                rows = slice(c * scatter_rows, (c + 1) * scatter_rows)
                part = jnp.dot(he[rows], wd, preferred_element_type=F32)
                acc_ref[rows, :] = (acc_ref[rows, :] + part) * roww_ref[rows, 0:1]
                for i in range(scatter_rows):
                    row = c * scatter_rows + i
                    dst = dst_ref[r * tm + row]
                    pltpu.make_async_copy(acc_ref.at[pl.ds(row, 1)], out_hbm.at[pl.ds(dst, 1)], ssem).start()

    @pl.when((f == nf - 1) & (r == nv - 1))
    def _():
        scatter_wait()
        gather_wait()


def _moe_plan(rinfo, T, tm):
    ids = rinfo[:, 0:TOP_K].astype(jnp.int32)
    wts = rinfo[:, TOP_K:2 * TOP_K]
    n_assign = TOP_K * T
    n_rows = n_assign + N_EXPERTS * tm
    n_tiles = n_rows // tm
    e_flat = ids.reshape(-1)
    onehot = (e_flat[:, None] == jnp.arange(N_EXPERTS, dtype=jnp.int32)[None, :]).astype(jnp.int32)
    rank = jnp.take_along_axis(jnp.cumsum(onehot, axis=0), e_flat[:, None], axis=1)[:, 0] - 1
    counts = jnp.sum(onehot, axis=0)
    padded = ((counts + tm - 1) // tm) * tm
    ends = jnp.cumsum(padded)
    starts = ends - padded
    dest = starts[e_flat] + rank
    upd = jnp.stack([jnp.arange(1, n_assign + 1, dtype=jnp.int32).astype(F32), wts.reshape(-1)], axis=1)
    rows = jnp.zeros((n_rows, 2), F32).at[dest].set(upd)
    row_a = rows[:, 0].astype(jnp.int32) - 1
    row_w = rows[:, 1]
    is_real = row_a >= 0
    tok = jnp.maximum(row_a, 0) // TOP_K
    slot = jnp.maximum(row_a, 0) % TOP_K
    p = jnp.arange(n_rows, dtype=jnp.int32)
    dump = n_assign + ((p // tm) % 2) * tm + (p % tm)
    row_tok = jnp.where(is_real, tok, 0)
    row_dst = jnp.where(is_real, slot * T + tok, dump)
    n_valid = (ends[-1] // tm).astype(jnp.int32)
    tile_start = jnp.arange(n_tiles, dtype=jnp.int32) * tm
    tile_e = jnp.minimum(jnp.searchsorted(ends, tile_start, side="right"), N_EXPERTS - 1).astype(jnp.int32)
    last_e = tile_e[jnp.maximum(n_valid - 1, 0)]
    tile_e = jnp.where(tile_start < ends[-1], tile_e, last_e)
    return tile_e, n_valid.reshape(1), row_tok, row_dst, jnp.broadcast_to(row_w[:, None], (n_rows, LANES))


def _moe(h2, rinfo, wg, wu, wd, layer, tm=MOE_TM, tf=MOE_TF):
    T, D = h2.shape
    FF = wg.shape[-1]
    nf = FF // tf
    tile_e, n_valid, row_tok, row_dst, row_w = _moe_plan(rinfo, T, tm)
    n_tiles = tile_e.shape[0]

    def f_eff(r, f, nv):
        return jnp.where(r < nv[0], f, nf - 1)

    grid_spec = pltpu.PrefetchScalarGridSpec(
        num_scalar_prefetch=4,
        grid=(n_tiles, nf),
        in_specs=[pl.BlockSpec(memory_space=pl.ANY),
                  pl.BlockSpec((tm, LANES), lambda r, f, te, nv, tok, dst: (r, 0)),
                  pl.BlockSpec((None, None, D, tf), lambda r, f, te, nv, tok, dst: (layer, te[r], 0, f_eff(r, f, nv))),
                  pl.BlockSpec((None, None, D, tf), lambda r, f, te, nv, tok, dst: (layer, te[r], 0, f_eff(r, f, nv))),
                  pl.BlockSpec((None, None, tf, D), lambda r, f, te, nv, tok, dst: (layer, te[r], f_eff(r, f, nv), 0))],
        out_specs=pl.BlockSpec(memory_space=pl.ANY),
        scratch_shapes=[pltpu.VMEM((tm, D), F32), pltpu.VMEM((tm, D), BF16), pltpu.VMEM((tm, D), F32),
                        pltpu.SemaphoreType.DMA(()), pltpu.SemaphoreType.DMA(())],
    )
    return pl.pallas_call(
        functools.partial(_moe_kernel, tm=tm, nf=nf, n_real=TOP_K * T),
        grid_spec=grid_spec,
        out_shape=jax.ShapeDtypeStruct((TOP_K * T + 2 * tm, D), F32),
        compiler_params=_params("arbitrary", "arbitrary"),
        name="moe_routed",
    )(tile_e, n_valid, row_tok, row_dst, h2, row_w, wg, wu, wd)


def kernel(x_prompt, x_sample, state_hgrn, state_mlstm_C, state_mlstm_n, state_mlstm_m, state_gdn, state_conv, c_prompt, c_sample, w_in, b_in, hgrn_lb_logits, hgrn_norm_g, gdn_conv_w, gdn_A_log, gdn_dt_bias, gdn_norm_g, w_branch, w_merge, b_merge, w_out, ln_mix_g, ln_mix_b, w_ada, b_ada, w_router, b_router, w_exp_gate, w_exp_up, w_exp_down, ln_ffn_g, ln_ffn_b):
    depth = w_in.shape[0]
    Bp, Lp, D = x_prompt.shape
    Bs, Ls, _ = x_sample.shape
    Tp, Ts = Bp * Lp, Bs * Ls
    T = Tp + Ts
    alpha = (2 * depth) ** 0.25
    tb = TOK_BLOCK
    assert Lp % tb == 0 and tb % Ls == 0 and Ts % tb == 0
    npb = Tp // tb
    spb = tb // Ls
    n_ab = 8 * WIDTH
    g0, c0 = n_ab, n_ab + 2 * N_HEADS
    c1 = c0 + 4 * WIDTH

    nc = Bp + Bs
    nc_pad = -(-nc // 16) * 16
    c_all = jnp.pad(jnp.concatenate([c_sample, c_prompt], axis=0), ((0, nc_pad - nc), (0, 0)))
    c_act = (c_all * jax.nn.sigmoid(c_all)).astype(BF16)

    wr = jnp.pad(w_router.astype(F32), ((0, 0), (0, LANES - N_EXPERTS)))
    br = jnp.pad(b_router.astype(F32), ((0, LANES - N_EXPERTS),)).reshape(1, LANES)
    mp_spec, ms_spec = _mod_specs(npb, Bs, Lp // tb, Bp, spb, D)
    tok_spec = pl.BlockSpec((tb, D), lambda i: (i, 0))
    vec_spec = pl.BlockSpec((None, 1, D), lambda i: (0, 0, 0))
    vec = lambda a, l: a[l].reshape(1, 1, D)
    zero_bias = jnp.zeros((1, D), F32)

    w_in_t = jnp.swapaxes(w_in, 1, 2)
    x = jnp.concatenate([x_prompt.reshape(Tp, D), x_sample.reshape(Ts, D)], axis=0)
    def state_bufs(B):
        z = lambda *s: jnp.zeros((depth, B) + s, F32)
        return (z(*_STATE_DIMS), [z(*_STATE_DIMS), z(*_VEC_DIMS), z(*_VEC_DIMS)],
                [z(*_STATE_DIMS), z(CONV_W - 1, 3 * WIDTH)])

    hg_p, ml_p, gd_p = state_bufs(Bp)
    hg_s, ml_s, gd_s = state_bufs(Bs)
    mlstm_n_in = state_mlstm_n.reshape(depth, Bs, N_HEADS, 1, HEAD_DIM)
    mlstm_m_in = jnp.broadcast_to(state_mlstm_m[..., None, None], (depth, Bs, N_HEADS, 1, HEAD_DIM))
    for l in range(depth):
        mod = _matmul(c_act, w_ada, (None, D, 1024), lambda n: (l, 0, n),
                      b_ada.reshape(depth, 1, 6 * D), (None, 1, 1024), lambda n: (l, 0, n),
                      6 * D, F32, tm=nc_pad, tn=1024, name="ada").reshape(nc_pad, 6, D)

        h = pl.pallas_call(
            functools.partial(_modulate_kernel, npb=npb, spb=spb, sh_idx=0, sc_idx=1),
            grid=(T // tb,),
            in_specs=[tok_spec, mp_spec, ms_spec],
            out_specs=tok_spec,
            out_shape=jax.ShapeDtypeStruct((T, D), BF16),
            compiler_params=_params("arbitrary"),
            name="modulate",
        )(x, mod, mod)

        b_in3 = b_in.reshape(depth, 1, -1)
        proj_ab = _matmul(h, w_in_t, (None, 512, D), lambda n: (l, n, 0),
                          b_in3, (None, 1, 512), lambda n: (l, 0, n),
                          n_ab, F32, tm=1024, tn=512, name="proj_ab", w_transposed=True)
        proj_c = _matmul(h, w_in_t[l, c0:c1, :], (512, D), lambda n: (n, 0),
                         b_in[l, c0:c1].reshape(1, -1), (1, 512), lambda n: (0, n),
                         4 * WIDTH, F32, tm=1024, tn=512, name="proj_c", w_transposed=True)
        w_g = jnp.pad(jnp.concatenate([w_in_t[l, g0:c0, :], w_in_t[l, c1:, :]], axis=0),
                      ((0, LANES - 4 * N_HEADS), (0, 0)))
        b_g = jnp.pad(jnp.concatenate([b_in[l, g0:c0], b_in[l, c1:]]), ((0, LANES - 4 * N_HEADS),))
        gates = _matmul(h, w_g, (LANES, D), lambda n: (n, 0), b_g.reshape(1, LANES), (1, LANES),
                        lambda n: (0, n), LANES, F32, tm=1024, tn=LANES, name="proj_gates", w_transposed=True)
        gate = _matmul(h, w_merge, (None, None, D, 512), lambda n: (l, n // (D // 512), 0, n % (D // 512)),
                       b_merge.reshape(depth, 3, 1, D), (None, None, 1, 512),
                       lambda n: (l, n // (D // 512), 0, n % (D // 512)),
                       3 * D, BF16, tm=1024, tn=512, name="merge_gates", act="sigmoid")

        conv_w_t = gdn_conv_w[l].T.astype(F32)
        y_init = jnp.zeros((T, WIDTH), F32)
        ya, hg_p = _hgrn(proj_ab, 0, 0, Bp, Lp, hgrn_lb_logits, hgrn_norm_g[l], l, None, (y_init, hg_p))
        ya, hg_s = _hgrn(proj_ab, 0, Tp, Bs, Ls, hgrn_lb_logits, hgrn_norm_g[l], l, state_hgrn, (ya, hg_s))
        yb, *ml_p = _mlstm(proj_ab, 4, gates, 0, Bp, Lp, l, None, (y_init, *ml_p))
        yb, *ml_s = _mlstm(proj_ab, 4, gates, Tp, Bs, Ls, l, (state_mlstm_C, mlstm_n_in, mlstm_m_in),
                           (yb, *ml_s))
        yc, *gd_p = _gdn(proj_c, 0, gates, 0, Bp, Lp, conv_w_t, gdn_A_log[l], gdn_dt_bias[l],
                         gdn_norm_g[l], l, None, (y_init, *gd_p))
        yc, *gd_s = _gdn(proj_c, 0, gates, Tp, Bs, Ls, conv_w_t, gdn_A_log[l], gdn_dt_bias[l],
                         gdn_norm_g[l], l, (state_gdn, state_conv), (yc, *gd_s))

        merged = _merge((ya, yb, yc), gate, w_branch, l)
        mix = _matmul(merged, w_out, (None, D, 512), lambda n: (l, 0, n),
                      zero_bias, (1, 512), lambda n: (0, n), D, F32, tm=1024, tn=512, name="out_proj")

        x, h2, rinfo = pl.pallas_call(
            functools.partial(_post_route_kernel, npb=npb, spb=spb, alpha=alpha),
            grid=(T // tb,),
            in_specs=[tok_spec, tok_spec, mp_spec, ms_spec, vec_spec, vec_spec,
                      pl.BlockSpec((D, LANES), lambda i: (0, 0)), pl.BlockSpec((1, LANES), lambda i: (0, 0))],
            out_specs=[tok_spec, tok_spec, pl.BlockSpec((tb, LANES), lambda i: (i, 0))],
            out_shape=[jax.ShapeDtypeStruct((T, D), F32), jax.ShapeDtypeStruct((T, D), F32),
                       jax.ShapeDtypeStruct((T, LANES), F32)],
            compiler_params=_params("arbitrary"),
            name="post_route",
        )(x, mix, mod, mod, vec(ln_mix_g, l), vec(ln_mix_b, l), wr, br)

        ffn = _moe(h2, rinfo, w_exp_gate, w_exp_up, w_exp_down, l)
        x = pl.pallas_call(
            functools.partial(_post_final_kernel, npb=npb, spb=spb, alpha=alpha),
            grid=(T // tb,),
            in_specs=[tok_spec, tok_spec, pl.BlockSpec((tb, D), lambda i: (T // tb + i, 0)),
                      mp_spec, ms_spec, vec_spec, vec_spec],
            out_specs=tok_spec,
            out_shape=jax.ShapeDtypeStruct((T, D), F32),
            compiler_params=_params("arbitrary"),
            name="post_final",
        )(x, ffn, ffn, mod, mod, vec(ln_ffn_g, l), vec(ln_ffn_b, l))

    as_vec = lambda n, B: n.reshape(depth, B, N_HEADS, HEAD_DIM)
    as_scalar = lambda m: m[:, :, :, 0, 0]
    return (x[:Tp].reshape(Bp, Lp, D), x[Tp:].reshape(Bs, Ls, D), hg_p, hg_s, ml_p[0], ml_s[0],
            as_vec(ml_p[1], Bp), as_vec(ml_s[1], Bs), as_scalar(ml_p[2]), as_scalar(ml_s[2]),
            gd_p[0], gd_s[0], gd_p[1], gd_s[1])
```

```python
import functools
import math

import numpy as np
import jax
import jax.numpy as jnp
from jax import lax
from jax.experimental import pallas as pl
from jax.experimental.pallas import tpu as pltpu

F32 = jnp.float32
BF16 = jnp.bfloat16
HIGHEST = lax.Precision.HIGHEST

HEAD_DIM = 128
N_HEADS = 8
WIDTH = N_HEADS * HEAD_DIM
CHUNK = 64
CONV_W = 4
N_EXPERTS = 16
N_GROUPS = 4
EXPERTS_PER_GROUP = N_EXPERTS // N_GROUPS
TOP_K = 2
LN_EPS = 1e-5
NORM_EPS = 1e-6
LB_TINY = 1e-20
NEG_BIG = -1e30
LANES = 128
VMEM_LIMIT = 56 * 1024 * 1024
TOK_BLOCK = 256
ROUTE_BLOCK = 128
MM_TM = 1024
MM_TN = 512
MOE_TM = 512
MOE_TF = 256
HEADS = tuple(slice(h * HEAD_DIM, (h + 1) * HEAD_DIM) for h in range(N_HEADS))


def _params(*sem):
    return pltpu.CompilerParams(dimension_semantics=sem, vmem_limit_bytes=VMEM_LIMIT)


def _dot(a, b):
    return jnp.dot(a.astype(BF16), b.astype(BF16), preferred_element_type=F32)


def _dot_nt(a, b):
    return lax.dot_general(a.astype(BF16), b.astype(BF16), (((1,), (1,)), ((), ())),
                           preferred_element_type=F32)


def _dot_tn(a, b):
    return lax.dot_general(a.astype(BF16), b.astype(BF16), (((0,), (0,)), ((), ())),
                           preferred_element_type=F32)


def _dot_hi(a, b):
    return jnp.dot(a, b, precision=HIGHEST, preferred_element_type=F32)


def _split2(x):
    x1 = x.astype(BF16)
    return x1, (x - x1.astype(F32)).astype(BF16)


def _dot_exact_l(a_bf, x):
    x1, x2 = _split2(x)
    d = lambda p: jnp.dot(a_bf, p, preferred_element_type=F32)
    return d(x1) + d(x2)


def _dot_exact_r(x, b_bf):
    x1, x2 = _split2(x)
    d = lambda p: jnp.dot(p, b_bf, preferred_element_type=F32)
    return d(x1) + d(x2)


def _dot3(a, b):
    a1, a2 = _split2(a)
    b1, b2 = _split2(b)
    d = lambda p, q: jnp.dot(p, q, preferred_element_type=F32)
    return d(a1, b1) + (d(a1, b2) + d(a2, b1))


def _sigmoid(x):
    return jax.nn.sigmoid(x)


def _silu(x):
    return x * jax.nn.sigmoid(x)


def _log_sigmoid(x):
    return jnp.minimum(x, 0.0) - jnp.log1p(jnp.exp(-jnp.abs(x)))


def _softplus(x):
    return jnp.maximum(x, 0.0) + jnp.log1p(jnp.exp(-jnp.abs(x)))


def _logaddexp(a, b):
    return jnp.maximum(a, b) + jnp.log1p(jnp.exp(-jnp.abs(a - b)))


def _tri_masks(cn):
    r = lax.broadcasted_iota(jnp.int32, (cn, cn), 0)
    c = lax.broadcasted_iota(jnp.int32, (cn, cn), 1)
    return r >= c, r > c, r == c


def _rms_gate(o, ng, gate):
    return o * lax.rsqrt(jnp.mean(o * o, axis=-1, keepdims=True) + NORM_EPS) * ng * gate


def _mm_kernel(x_ref, w_ref, b_ref, o_ref, wbf_ref, *, act, w_transposed):
    @pl.when(pl.program_id(1) == 0)
    def _():
        if w_transposed:
            tn = w_ref.shape[0]
            step = min(tn, 256)
            for c in range(0, tn, step):
                wbf_ref[:, c:c + step] = w_ref[c:c + step, :].T.astype(BF16)
        else:
            wbf_ref[...] = w_ref[...].astype(BF16)

    acc = jnp.dot(x_ref[...], wbf_ref[...], preferred_element_type=F32) + b_ref[...]
    if act == "sigmoid":
        acc = _sigmoid(acc)
    o_ref[...] = acc.astype(o_ref.dtype)


def _matmul(x, w, w_block, w_index, b, b_block, b_index, n_out, out_dtype, *, tm, tn, name, act=None,
            w_transposed=False):
    M, K = x.shape
    tm = min(tm, M)
    assert M % tm == 0 and n_out % tn == 0, (M, tm, n_out, tn)
    return pl.pallas_call(
        functools.partial(_mm_kernel, act=act, w_transposed=w_transposed),
        grid=(n_out // tn, M // tm),
        in_specs=[pl.BlockSpec((tm, K), lambda n, m: (m, 0)),
                  pl.BlockSpec(w_block, lambda n, m: w_index(n)),
                  pl.BlockSpec(b_block, lambda n, m: b_index(n))],
        out_specs=pl.BlockSpec((tm, tn), lambda n, m: (m, n)),
        out_shape=jax.ShapeDtypeStruct((M, n_out), out_dtype),
        scratch_shapes=[pltpu.VMEM((K, tn), BF16)],
        compiler_params=_params("arbitrary", "arbitrary"),
        name=name,
    )(x, w, b)


def _with_mod(npb, mp_ref, ms_ref, body):
    i = pl.program_id(0)

    @pl.when(i < npb)
    def _():
        body(mp_ref[...], 0)

    @pl.when(i >= npb)
    def _():
        body(ms_ref[...], 1)


def _group_specs(npb, tb, D):
    xp = pl.BlockSpec((tb, D), lambda i: (jnp.minimum(i, npb - 1), 0))
    xs = pl.BlockSpec((tb, D), lambda i: (jnp.maximum(i - npb, 0), 0))
    return xp, xs


def _mod_specs(npb, n_sample_seq, blocks_per_prompt_seq, n_prompt_seq, spb, D):
    mp = pl.BlockSpec((1, 6, D), lambda i: (n_sample_seq + jnp.minimum(i // blocks_per_prompt_seq,
                                                                        n_prompt_seq - 1), 0, 0))
    ms = pl.BlockSpec((spb, 6, D), lambda i: (jnp.maximum(i - npb, 0), 0, 0))
    return mp, ms


def _modulate_kernel(xp_ref, xs_ref, mp_ref, ms_ref, o_ref, *, npb, spb, sh_idx, sc_idx):
    tb, D = xp_ref.shape

    def body(m, group):
        x = (xp_ref, xs_ref)[group][...].reshape(spb, tb // spb, D)
        h = x * (1.0 + m[:, sc_idx:sc_idx + 1, :]) + m[:, sh_idx:sh_idx + 1, :]
        o_ref[...] = h.reshape(tb, D).astype(o_ref.dtype)

    _with_mod(npb, mp_ref, ms_ref, body)


def _layer_norm(z, g, b):
    mu = jnp.mean(z, axis=-1, keepdims=True)
    zc = z - mu
    var = jnp.mean(zc * zc, axis=-1, keepdims=True)
    return zc * lax.rsqrt(var + LN_EPS) * g + b


def _route(logits):
    lane = lax.broadcasted_iota(jnp.int32, logits.shape, 1)
    valid = lane < N_EXPERTS
    lg = jnp.where(valid, logits, NEG_BIG)
    ex = jnp.where(valid, jnp.exp(lg - jnp.max(lg, axis=-1, keepdims=True)), 0.0)
    probs = ex / jnp.sum(ex, axis=-1, keepdims=True)
    big = LANES
    best = None
    for g in range(N_GROUPS):
        in_g = (lane >= g * EXPERTS_PER_GROUP) & (lane < (g + 1) * EXPERTS_PER_GROUP)
        m1 = jnp.max(jnp.where(in_g, probs, -1.0), axis=-1, keepdims=True)
        i1 = jnp.min(jnp.where(in_g & (probs == m1), lane, big), axis=-1, keepdims=True)
        rest = in_g & (lane != i1)
        m2 = jnp.max(jnp.where(rest, probs, -1.0), axis=-1, keepdims=True)
        i2 = jnp.min(jnp.where(rest & (probs == m2), lane, big), axis=-1, keepdims=True)
        cand = (m1 + m2, i1, i2, m1, m2)
        if best is None:
            best = cand
        else:
            better = cand[0] > best[0]
            best = tuple(jnp.where(better, c, o) for c, o in zip(cand, best))
    _, i1, i2, m1, m2 = best
    tot = m1 + m2
    return (jnp.where(lane == 0, i1.astype(F32), 0.0) + jnp.where(lane == 1, i2.astype(F32), 0.0)
            + jnp.where(lane == 2, m1 / tot, 0.0) + jnp.where(lane == 3, m2 / tot, 0.0))


def _post_route_kernel(xp_ref, xs_ref, y_ref, mp_ref, ms_ref, g_ref, b_ref, wr_ref, br_ref,
                       xop_ref, xos_ref, h_ref, r_ref, *, npb, spb, alpha):
    tb, D = xp_ref.shape

    def body(m, group):
        x = (xp_ref, xs_ref)[group][...].reshape(spb, tb // spb, D)
        y = y_ref[...].reshape(spb, tb // spb, D)
        xn = _layer_norm(alpha * x + m[:, 2:3, :] * y, g_ref[...], b_ref[...])
        (xop_ref, xos_ref)[group][...] = xn.reshape(tb, D)
        h2 = (xn * (1.0 + m[:, 4:5, :]) + m[:, 3:4, :]).reshape(tb, D)
        h_ref[...] = h2
        r_ref[...] = _route(_dot_hi(h2, wr_ref[...]) + br_ref[...])

    _with_mod(npb, mp_ref, ms_ref, body)


def _post_final_kernel(xp_ref, xs_ref, y0_ref, y1_ref, mp_ref, ms_ref, g_ref, b_ref,
                       xop_ref, xos_ref, *, npb, spb, alpha):
    tb, D = xp_ref.shape

    def body(m, group):
        x = (xp_ref, xs_ref)[group][...].reshape(spb, tb // spb, D)
        y = (y0_ref[...] + y1_ref[...]).reshape(spb, tb // spb, D)
        xn = _layer_norm(alpha * x + m[:, 5:6, :] * y, g_ref[...], b_ref[...])
        (xop_ref, xos_ref)[group][...] = xn.reshape(tb, D)

    _with_mod(npb, mp_ref, ms_ref, body)


def _seq_blocking(L):
    cn = math.gcd(L, CHUNK)
    lb = min(L, 512)
    assert L % lb == 0 and lb % cn == 0
    return cn, lb, L // lb


def _run_chunks(chunk, nchunk):
    if nchunk == 1:
        chunk(0, 0)
    else:
        lax.fori_loop(0, nchunk, chunk, 0)


def _hgrn_tables(cn):
    nlev = int(math.log2(cn))
    assert 1 << nlev == cn
    r = np.arange(cn)
    mats = [np.tril(np.ones((cn, cn)))]
    masks = [np.eye(cn)]
    j = r[None, :]
    for lev in range(nlev):
        s = cn >> (lev + 1)
        blk = 2 * s
        pos = r % blk
        ref = r - pos + s - 1
        second = pos >= s
        m_second = (j > ref[:, None]) & (j <= r[:, None])
        m_first = (j > r[:, None]) & (j <= ref[:, None])
        mats.append(np.where(second[:, None], m_second, m_first))
        same = (r[:, None] // blk) == (r[None, :] // blk)
        masks.append(same & second[:, None] & (~second)[None, :])
    return (np.concatenate(mats, 0).astype(np.float32), np.stack(masks).astype(np.float32))


def _hgrn_kernel(*refs, layer, cn, nchunk, has_state, n_alias):
    refs = list(refs)
    lbl_ref, ng_ref, mall_ref, masks_ref, q_ref, f_ref, i_ref, g_ref = refs[:8]
    pos = 8
    s0_ref = None
    if has_state:
        s0_ref = refs[pos]
        pos += 1
    pos += n_alias
    y_ref, s_ref, st_ref = refs[pos:pos + 3]
    j = pl.program_id(1)
    nlev = masks_ref.shape[0] - 1

    @pl.when(j == 0)
    def _():
        for h in range(N_HEADS):
            if has_state:
                st_ref[h] = s0_ref[0, h].T
            else:
                st_ref[h] = jnp.zeros((HEAD_DIM, HEAD_DIM), F32)

    logits = lbl_ref[...]
    e = jnp.exp(logits - jnp.max(logits, axis=0, keepdims=True))
    p = e / jnp.sum(e, axis=0, keepdims=True)
    lb = jnp.zeros((1, WIDTH), F32)
    for l in range(1, layer + 1):
        lb = lb + p[l:l + 1, :]
    log_lb = jnp.log(jnp.maximum(lb, LB_TINY))
    log1m_lb = jnp.log1p(-lb)
    one_m_lb = 1.0 - lb
    ng = ng_ref[...]
    mall = mall_ref[...]

    def chunk(c, carry):
        r0 = pl.multiple_of(c * cn, cn)
        rows = pl.ds(r0, cn)
        q = _silu(q_ref[rows, :])
        fr = f_ref[rows, :]
        v = i_ref[rows, :].astype(BF16)
        log_f = _logaddexp(log_lb, log1m_lb + _log_sigmoid(fr))
        k = one_m_lb * _sigmoid(-fr)
        dall = _dot_exact_l(mall, log_f)
        b = dall[:cn]
        qb = q.astype(BF16)
        kb = k.astype(BF16)
        att = [masks_ref[0] * _dot_nt(qb[:, cs], kb[:, cs]) for cs in HEADS]
        for lev in range(1, nlev + 1):
            ex = jnp.exp(dall[lev * cn:(lev + 1) * cn])
            qe = (q * ex).astype(BF16)
            ke = (k * ex).astype(BF16)
            att = [a + masks_ref[lev] * _dot_nt(qe[:, cs], ke[:, cs]) for a, cs in zip(att, HEADS)]
        st = [st_ref[h] for h in range(N_HEADS)]
        qd = (q * jnp.exp(b)).astype(BF16)
        o = [_dot(a, v[:, cs]) + _dot_nt(qd[:, cs], s) for a, cs, s in zip(att, HEADS, st)]
        b_last = b[cn - 1:cn]
        k_dec = (k * jnp.exp(b_last - b)).astype(BF16)
        dec = jnp.exp(b_last)
        for h, cs in enumerate(HEADS):
            st_ref[h] = st[h] * dec[:, cs] + _dot_tn(v[:, cs], k_dec[:, cs])
        gate = _sigmoid(g_ref[rows, :])
        for h, cs in enumerate(HEADS):
            y_ref[rows, cs] = _rms_gate(o[h], ng[:, cs], gate[:, cs])
        return carry

    _run_chunks(chunk, nchunk)

    @pl.when(j == pl.num_programs(1) - 1)
    def _():
        for h in range(N_HEADS):
            s_ref[0, h] = st_ref[h].T


def _mixer_specs(T_total, row0, L):
    cn, LB, ncb = _seq_blocking(L)
    assert row0 % LB == 0
    rb0 = row0 // LB
    col = lambda cidx: pl.BlockSpec((LB, WIDTH), lambda b, j: (rb0 + b * ncb + j, cidx))
    yspec = pl.BlockSpec((LB, WIDTH), lambda b, j: (rb0 + b * ncb + j, 0))
    return cn, LB, ncb, rb0, col, yspec


def _state_spec(layer, trailing):
    return pl.BlockSpec((None, 1) + trailing, lambda b, j: (layer, b) + (0,) * len(trailing))


_STATE_DIMS = (N_HEADS, HEAD_DIM, HEAD_DIM)
_VEC_DIMS = (N_HEADS, 1, HEAD_DIM)


def _full(a):
    return pl.BlockSpec(a.shape, lambda b, j: (0,) * a.ndim)


def _alias_outputs(args, in_specs, prevs):
    aliases = {}
    for k, prev in enumerate(prevs):
        args.append(prev)
        in_specs.append(pl.BlockSpec(memory_space=pl.ANY))
        aliases[len(args) - 1] = k
    return aliases


def _shapes(prevs):
    return [jax.ShapeDtypeStruct(p.shape, p.dtype) for p in prevs]


def _hgrn(proj, a_col0, row0, B, L, lb_logits, norm_g, layer, state, prevs):
    cn, LB, ncb, rb0, col, yspec = _mixer_specs(proj.shape[0], row0, L)
    mall, masks = _hgrn_tables(cn)
    mall = jnp.asarray(mall, BF16)
    masks = jnp.asarray(masks)
    ng = jnp.tile(norm_g.reshape(1, HEAD_DIM).astype(F32), (1, N_HEADS))
    sspec = _state_spec(layer, _STATE_DIMS)
    args = [lb_logits, ng, mall, masks, proj, proj, proj, proj]
    in_specs = [_full(lb_logits), _full(ng), _full(mall), _full(masks)] + [col(a_col0 + t) for t in range(4)]
    if state is not None:
        args.append(state)
        in_specs.append(sspec)
    aliases = _alias_outputs(args, in_specs, prevs)
    return pl.pallas_call(
        functools.partial(_hgrn_kernel, layer=layer, cn=cn, nchunk=LB // cn,
                          has_state=state is not None, n_alias=len(prevs)),
        grid=(B, ncb),
        in_specs=in_specs,
        out_specs=[yspec, sspec],
        out_shape=_shapes(prevs),
        scratch_shapes=[pltpu.VMEM((N_HEADS, HEAD_DIM, HEAD_DIM), F32)],
        input_output_aliases=aliases,
        compiler_params=_params("arbitrary", "arbitrary"),
        name="hgrn2",
    )(*args)


def _head_cols(mat, base):
    return [mat[:, base + h:base + h + 1] for h in range(N_HEADS)]


def _head_rows(mat, base):
    return [mat[base + h:base + h + 1, :] for h in range(N_HEADS)]


def _mlstm_kernel(*refs, cn, nchunk, has_state, n_alias):
    refs = list(refs)
    gt_ref, gtt_ref, q_ref, k_ref, v_ref, o_ref = refs[:6]
    pos = 6
    if has_state:
        c0_ref, n0_ref, m0_ref = refs[pos:pos + 3]
        pos += 3
    pos += n_alias
    y_ref, c_ref, n_ref, m_ref = refs[pos:pos + 4]
    j = pl.program_id(1)

    @pl.when(j == 0)
    def _():
        if has_state:
            c_ref[...] = c0_ref[...]
            n_ref[...] = n0_ref[...]
            m_ref[...] = m0_ref[...]
        else:
            c_ref[...] = jnp.zeros(c_ref.shape, F32)
            n_ref[...] = jnp.zeros(n_ref.shape, F32)
            m_ref[...] = jnp.zeros(m_ref.shape, F32)

    incl, _, _ = _tri_masks(cn)
    tril = incl.astype(BF16)
    triu = jnp.logical_not(_tri_masks(cn)[1]).astype(BF16)

    def chunk(c, carry):
        r0 = pl.multiple_of(c * cn, cn)
        rows = pl.ds(r0, cn)
        gates = gt_ref[rows, :]
        gates_r = gtt_ref[c]
        f_c = _dot_exact_l(tril, _log_sigmoid(gates))
        f_r = _dot_exact_r(_log_sigmoid(gates_r), triu)
        Fc = _head_cols(f_c, N_HEADS)
        ic = _head_cols(gates, 0)
        Fr = _head_rows(f_r, N_HEADS)
        ir = _head_rows(gates_r, 0)
        q = [q_ref[rows, cs].astype(BF16) for cs in HEADS]
        k = [k_ref[rows, cs] * (HEAD_DIM ** -0.5) for cs in HEADS]
        kb = [x.astype(BF16) for x in k]
        v = [v_ref[rows, cs].astype(BF16) for cs in HEADS]
        C = [c_ref[0, h] for h in range(N_HEADS)]
        n = [n_ref[0, h] for h in range(N_HEADS)]
        m = [m_ref[0, h][:, 0:1] for h in range(N_HEADS)]
        qk = [_dot_nt(a, b) for a, b in zip(q, kb)]
        qc = [_dot(a, b) for a, b in zip(q, C)]
        raw = [a - b + d for a, b, d in zip(Fc, Fr, ir)]
        inter = [a + b for a, b in zip(Fc, m)]
        m_t = [jnp.maximum(a, jnp.max(jnp.where(incl, r, NEG_BIG), axis=-1, keepdims=True))
               for a, r in zip(inter, raw)]
        w_inter = [jnp.exp(a - b) for a, b in zip(inter, m_t)]
        s = [jnp.where(incl, a * jnp.exp(jnp.where(incl, r - mt, 0.0)), 0.0)
             for a, r, mt in zip(qk, raw, m_t)]
        num = [_dot(a, b) + w * d for a, b, w, d in zip(s, v, w_inter, qc)]
        den = [jnp.sum(a, axis=-1, keepdims=True)
               + w * jnp.sum(q_ref[rows, cs] * nn, axis=-1, keepdims=True)
               for a, w, cs, nn in zip(s, w_inter, HEADS, n)]
        hh = [a / jnp.maximum(jnp.abs(d), jnp.exp(-mt)) for a, d, mt in zip(num, den, m_t)]
        for h, cs in enumerate(HEADS):
            m_new = m_t[h][cn - 1:cn]
            w_c = jnp.exp(inter[h][cn - 1:cn] - m_new)
            w_s = jnp.exp(Fc[h][cn - 1:cn] - Fc[h] + ic[h] - m_new)
            ks = k[h] * w_s
            c_ref[0, h] = w_c * C[h] + _dot_tn(ks, v[h])
            n_ref[0, h] = w_c * n[h] + jnp.sum(ks, axis=0, keepdims=True)
            m_ref[0, h] = jnp.broadcast_to(m_new, (1, HEAD_DIM))
            y_ref[rows, cs] = _sigmoid(o_ref[rows, cs]) * hh[h]
        return carry

    _run_chunks(chunk, nchunk)


def _gates_t(gates, row0, B, L, cn):
    g = gates[row0:row0 + B * L, :4 * N_HEADS]
    return g.reshape(B * L // cn, cn, 4 * N_HEADS).transpose(0, 2, 1)


def _mlstm(proj, b_col0, gates, row0, B, L, layer, state, prevs):
    cn, LB, ncb, rb0, col, yspec = _mixer_specs(proj.shape[0], row0, L)
    cspec = _state_spec(layer, _STATE_DIMS)
    nspec = _state_spec(layer, _VEC_DIMS)
    gt = _gates_t(gates, row0, B, L, cn)
    args = [gates, gt, proj, proj, proj, proj]
    in_specs = [pl.BlockSpec((LB, LANES), lambda b, j: (rb0 + b * ncb + j, 0)),
                pl.BlockSpec((LB // cn, 4 * N_HEADS, cn), lambda b, j: (b * ncb + j, 0, 0))]
    in_specs += [col(b_col0 + t) for t in range(4)]
    if state is not None:
        args += list(state)
        in_specs += [cspec, nspec, nspec]
    aliases = _alias_outputs(args, in_specs, prevs)
    return pl.pallas_call(
        functools.partial(_mlstm_kernel, cn=cn, nchunk=LB // cn, has_state=state is not None,
                          n_alias=len(prevs)),
        grid=(B, ncb),
        in_specs=in_specs,
        out_specs=[yspec, cspec, nspec, nspec],
        out_shape=_shapes(prevs),
        input_output_aliases=aliases,
        compiler_params=_params("arbitrary", "arbitrary"),
        name="mlstm",
    )(*args)


_CONV_PAD = 8


def _gdn_kernel(*refs, cn, nchunk, has_state, n_alias):
    refs = list(refs)
    (gt_ref, gtt_ref, cw_ref, al_ref, dtb_ref, alc_ref, dtbc_ref, ng_ref,
     xq_ref, xk_ref, xv_ref, g_ref) = refs[:12]
    pos = 12
    if has_state:
        s0_ref, b0_ref = refs[pos:pos + 2]
        pos += 2
    pos += n_alias
    y_ref, s_ref, buf_ref, xp_ref, cv_ref = refs[pos:pos + 5]
    j = pl.program_id(1)
    LB = xq_ref.shape[0]
    hist = CONV_W - 1
    h0 = _CONV_PAD - hist

    @pl.when(j == 0)
    def _():
        if has_state:
            s_ref[...] = s0_ref[...]
            xp_ref[h0:_CONV_PAD, :] = b0_ref[0]
        else:
            s_ref[...] = jnp.zeros(s_ref.shape, F32)
            xp_ref[h0:_CONV_PAD, :] = jnp.zeros((hist, 3 * WIDTH), F32)

    xp_ref[_CONV_PAD:_CONV_PAD + LB, 0:WIDTH] = xq_ref[...]
    xp_ref[_CONV_PAD:_CONV_PAD + LB, WIDTH:2 * WIDTH] = xk_ref[...]
    xp_ref[_CONV_PAD:_CONV_PAD + LB, 2 * WIDTH:3 * WIDTH] = xv_ref[...]
    acc = xp_ref[h0:h0 + LB, :] * cw_ref[0:1, :]
    for t in range(1, CONV_W):
        acc = acc + xp_ref[h0 + t:h0 + t + LB, :] * cw_ref[t:t + 1, :]
    cv_ref[...] = _silu(acc)
    new_hist = xp_ref[_CONV_PAD + LB - hist:_CONV_PAD + LB, :]
    xp_ref[h0:_CONV_PAD, :] = new_hist
    buf_ref[0] = new_hist

    incl, strict, eye = _tri_masks(cn)
    tril = incl.astype(BF16)
    triu = jnp.logical_not(strict).astype(BF16)
    eye_f = eye.astype(F32)
    lane = lax.broadcasted_iota(jnp.int32, (1, LANES), 1)
    is_a = (lane >= 2 * N_HEADS) & (lane < 3 * N_HEADS)
    sub = lax.broadcasted_iota(jnp.int32, (4 * N_HEADS, 1), 0)
    is_a_r = (sub >= 2 * N_HEADS) & (sub < 3 * N_HEADS)
    neg_rate = -jnp.exp(al_ref[...])
    dtb = dtb_ref[...]
    neg_rate_r = -jnp.exp(alc_ref[...])
    dtb_r = dtbc_ref[...]
    ng = ng_ref[...]
    nsq = int(math.log2(cn))

    def chunk(c, carry):
        r0 = pl.multiple_of(c * cn, cn)
        rows = pl.ds(r0, cn)
        gates = gt_ref[rows, :]
        gates_r = gtt_ref[c]
        g_c = jnp.where(is_a, neg_rate * _softplus(gates + dtb), 0.0)
        g_r = jnp.where(is_a_r, neg_rate_r * _softplus(gates_r + dtb_r), 0.0)
        gam_c = _head_cols(_dot_exact_l(tril, g_c), 2 * N_HEADS)
        gam_r = _head_rows(_dot_exact_r(g_r, triu), 2 * N_HEADS)
        beta = _head_cols(_sigmoid(gates), 3 * N_HEADS)
        q, k, v = [], [], []
        for cs in HEADS:
            qr = cv_ref[rows, cs]
            kr = cv_ref[rows, slice(WIDTH + cs.start, WIDTH + cs.stop)]
            v.append(cv_ref[rows, slice(2 * WIDTH + cs.start, 2 * WIDTH + cs.stop)])
            q.append(qr * (lax.rsqrt(jnp.sum(qr * qr, axis=-1, keepdims=True) + NORM_EPS) * (HEAD_DIM ** -0.5)))
            k.append(kr * lax.rsqrt(jnp.sum(kr * kr, axis=-1, keepdims=True) + NORM_EPS))
        kb = [x.astype(BF16) for x in k]
        S = [s_ref[0, h] for h in range(N_HEADS)]
        kk = [_dot_nt(a, a) for a in kb]
        qk = [_dot_nt(a, b) for a, b in zip(q, kb)]
        dec = [jnp.exp(jnp.where(incl, a - b, 0.0)) for a, b in zip(gam_c, gam_r)]
        pw = [jnp.where(strict, -(bt * a * d), 0.0) for bt, a, d in zip(beta, kk, dec)]
        t_inv = [eye_f + a for a in pw]
        for _ in range(nsq - 1):
            pw = [_dot3(a, a) for a in pw]
            t_inv = [t + _dot3(t, a) for t, a in zip(t_inv, pw)]
        eg = [jnp.exp(a) for a in gam_c]
        sol_v = [_dot3(t, bt * x) for t, bt, x in zip(t_inv, beta, v)]
        sol_k = [_dot3(t, (bt * e) * x) for t, bt, e, x in zip(t_inv, beta, eg, k)]
        u = [a - _dot(b, s) for a, b, s in zip(sol_v, sol_k, S)]
        o = [_dot(jnp.where(incl, a * d, 0.0), uu) + _dot(qq * e, s)
             for a, d, uu, qq, e, s in zip(qk, dec, u, q, eg, S)]
        for h, cs in enumerate(HEADS):
            g_last = gam_c[h][cn - 1:cn]
            s_ref[0, h] = jnp.exp(g_last) * S[h] + _dot_tn(k[h] * jnp.exp(g_last - gam_c[h]), u[h])
            y_ref[rows, cs] = _rms_gate(o[h], ng, _silu(g_ref[rows, cs]))
        return carry

    _run_chunks(chunk, nchunk)


def _gdn(proj, c_col0, gates, row0, B, L, conv_w_t, a_log, dt_bias, norm_g, layer, state, prevs):
    cn, LB, ncb, rb0, col, yspec = _mixer_specs(proj.shape[0], row0, L)
    hist = CONV_W - 1
    sspec = _state_spec(layer, _STATE_DIMS)
    bspec = _state_spec(layer, (hist, 3 * WIDTH))
    a0, a1 = 2 * N_HEADS, 3 * N_HEADS
    al = jnp.zeros((1, LANES), F32).at[0, a0:a1].set(a_log.astype(F32))
    db = jnp.zeros((1, LANES), F32).at[0, a0:a1].set(dt_bias.astype(F32))
    alc = al[0, :4 * N_HEADS].reshape(4 * N_HEADS, 1)
    dbc = db[0, :4 * N_HEADS].reshape(4 * N_HEADS, 1)
    ng = norm_g.reshape(1, HEAD_DIM).astype(F32)
    gt = _gates_t(gates, row0, B, L, cn)
    args = [gates, gt, conv_w_t, al, db, alc, dbc, ng, proj, proj, proj, proj]
    in_specs = [pl.BlockSpec((LB, LANES), lambda b, j: (rb0 + b * ncb + j, 0)),
                pl.BlockSpec((LB // cn, 4 * N_HEADS, cn), lambda b, j: (b * ncb + j, 0, 0)),
                _full(conv_w_t), _full(al), _full(db), _full(alc), _full(dbc), _full(ng)]
    in_specs += [col(c_col0 + t) for t in range(4)]
    if state is not None:
        args += [state[0], state[1]]
        in_specs += [sspec, bspec]
    aliases = _alias_outputs(args, in_specs, prevs)
    return pl.pallas_call(
        functools.partial(_gdn_kernel, cn=cn, nchunk=LB // cn, has_state=state is not None,
                          n_alias=len(prevs)),
        grid=(B, ncb),
        in_specs=in_specs,
        out_specs=[yspec, sspec, bspec],
        out_shape=_shapes(prevs),
        scratch_shapes=[pltpu.VMEM((_CONV_PAD + LB, 3 * WIDTH), F32),
                        pltpu.VMEM((LB, 3 * WIDTH), F32)],
        input_output_aliases=aliases,
        compiler_params=_params("arbitrary", "arbitrary"),
        name="gdn",
    )(*args)


def _merge_kernel(ya_ref, yb_ref, yc_ref, ga_ref, gb_ref, gc_ref, wb_ref, o_ref, wbf_ref):
    @pl.when(pl.program_id(1) == 0)
    def _():
        wbf_ref[...] = wb_ref[...].astype(BF16)

    acc = None
    for jdx, (y_ref, g_ref) in enumerate(((ya_ref, ga_ref), (yb_ref, gb_ref), (yc_ref, gc_ref))):
        br = jnp.dot(y_ref[...].astype(BF16), wbf_ref[jdx], preferred_element_type=F32)
        term = g_ref[...].astype(F32) * br
        acc = term if acc is None else acc + term
    o_ref[...] = acc.astype(o_ref.dtype)


def _merge(ys, gate, w_branch, layer, tm=256, tn=1024):
    T, W = ys[0].shape
    D = w_branch.shape[-1]
    tn = min(tn, D)
    nb = D // tn
    yspec = pl.BlockSpec((tm, W), lambda n, m: (m, 0))
    gspec = lambda jdx: pl.BlockSpec((tm, tn), lambda n, m: (m, jdx * nb + n))
    return pl.pallas_call(
        _merge_kernel,
        grid=(nb, T // tm),
        in_specs=[yspec, yspec, yspec, gspec(0), gspec(1), gspec(2),
                  pl.BlockSpec((None, 3, W, tn), lambda n, m: (layer, 0, 0, n))],
        out_specs=pl.BlockSpec((tm, tn), lambda n, m: (m, n)),
        out_shape=jax.ShapeDtypeStruct((T, D), BF16),
        scratch_shapes=[pltpu.VMEM((3, W, tn), BF16)],
        compiler_params=_params("arbitrary", "arbitrary"),
        name="merge",
    )(ys[0], ys[1], ys[2], gate, gate, gate, w_branch)


def _moe_kernel(te_ref, nv_ref, tok_ref, dst_ref,
                h_hbm, roww_ref, wg_ref, wu_ref, wd_ref, out_hbm,
                xbuf, xb_ref, acc_ref, gsem, ssem, *, tm, nf, n_real):
    r = pl.program_id(0)
    f = pl.program_id(1)
    nv = nv_ref[0]
    valid = r < nv
    assert nf >= 2 and tm % nf == 0
    gather_rows = tm // nf
    scatter_rows = min(tm, 128)

    def gather_start(tile):
        base = tile * tm

        def body(i, carry):
            tok = tok_ref[base + i]
            pltpu.make_async_copy(h_hbm.at[pl.ds(tok, 1)], xbuf.at[pl.ds(i, 1)], gsem).start()
            return carry

        lax.fori_loop(0, tm, body, 0, unroll=8)

    def gather_wait():
        pltpu.make_async_copy(h_hbm.at[pl.ds(0, tm)], xbuf, gsem).wait()

    def scatter_wait():
        pltpu.make_async_copy(acc_ref, out_hbm.at[pl.ds(0, tm)], ssem).wait()

    @pl.when((f == 0) & (r == 0))
    def _():
        acc_ref[...] = jnp.zeros(acc_ref.shape, F32)
        for half in range(2):
            fill = pltpu.make_async_copy(acc_ref, out_hbm.at[pl.ds(n_real + half * tm, tm)], ssem)
            fill.start()
            fill.wait()
        gather_start(0)

    @pl.when((f == 0) & valid)
    def _():
        gather_wait()
        xb_ref[...] = xbuf[...].astype(BF16)

    @pl.when(valid)
    def _():
        nxt = jnp.minimum(r + 1, nv - 1)
        g0 = f * gather_rows
        for i in range(gather_rows):
            tok = tok_ref[nxt * tm + g0 + i]
            pltpu.make_async_copy(h_hbm.at[pl.ds(tok, 1)], xbuf.at[pl.ds(g0 + i, 1)], gsem).start()

        x = xb_ref[...]
        g = jnp.dot(x, wg_ref[...].astype(BF16), preferred_element_type=F32)
        u = jnp.dot(x, wu_ref[...].astype(BF16), preferred_element_type=F32)
        he = (_silu(g) * u).astype(BF16)
        wd = wd_ref[...].astype(BF16)

        @pl.when(f == 0)
        def _():
            @pl.when(r > 0)
            def _():
                scatter_wait()
            acc_ref[...] = jnp.dot(he, wd, preferred_element_type=F32)

        @pl.when((f > 0) & (f < nf - 1))
        def _():
            acc_ref[...] += jnp.dot(he, wd, preferred_element_type=F32)

        @pl.when(f == nf - 1)
        def _():
            for c in range(tm // scatter_rows):
                rows = slice(c * scatter_rows, (c + 1) * scatter_rows)
                part = jnp.dot(he[rows], wd, preferred_element_type=F32)
                acc_ref[rows, :] = (acc_ref[rows, :] + part) * roww_ref[rows, 0:1]
                for i in range(scatter_rows):
                    row = c * scatter_rows + i
                    dst = dst_ref[r * tm + row]
                    pltpu.make_async_copy(acc_ref.at[pl.ds(row, 1)], out_hbm.at[pl.ds(dst, 1)], ssem).start()

    @pl.when((f == nf - 1) & (r == nv - 1))
    def _():
        scatter_wait()
        gather_wait()


def _moe_plan(rinfo, T, tm):
    ids = rinfo[:, 0:TOP_K].astype(jnp.int32)
    wts = rinfo[:, TOP_K:2 * TOP_K]
    n_assign = TOP_K * T
    n_rows = n_assign + N_EXPERTS * tm
    n_tiles = n_rows // tm
    e_flat = ids.reshape(-1)
    onehot = (e_flat[:, None] == jnp.arange(N_EXPERTS, dtype=jnp.int32)[None, :]).astype(jnp.int32)
    rank = jnp.take_along_axis(jnp.cumsum(onehot, axis=0), e_flat[:, None], axis=1)[:, 0] - 1
    counts = jnp.sum(onehot, axis=0)
    padded = ((counts + tm - 1) // tm) * tm
    ends = jnp.cumsum(padded)
    starts = ends - padded
    dest = starts[e_flat] + rank
    upd = jnp.stack([jnp.arange(1, n_assign + 1, dtype=jnp.int32).astype(F32), wts.reshape(-1)], axis=1)
    rows = jnp.zeros((n_rows, 2), F32).at[dest].set(upd)
    row_a = rows[:, 0].astype(jnp.int32) - 1
    row_w = rows[:, 1]
    is_real = row_a >= 0
    tok = jnp.maximum(row_a, 0) // TOP_K
    slot = jnp.maximum(row_a, 0) % TOP_K
    p = jnp.arange(n_rows, dtype=jnp.int32)
    dump = n_assign + ((p // tm) % 2) * tm + (p % tm)
    row_tok = jnp.where(is_real, tok, 0)
    row_dst = jnp.where(is_real, slot * T + tok, dump)
    n_valid = (ends[-1] // tm).astype(jnp.int32)
    tile_start = jnp.arange(n_tiles, dtype=jnp.int32) * tm
    tile_e = jnp.minimum(jnp.searchsorted(ends, tile_start, side="right"), N_EXPERTS - 1).astype(jnp.int32)
    last_e = tile_e[jnp.maximum(n_valid - 1, 0)]
    tile_e = jnp.where(tile_start < ends[-1], tile_e, last_e)
    return tile_e, n_valid.reshape(1), row_tok, row_dst, jnp.broadcast_to(row_w[:, None], (n_rows, LANES))


def _moe(h2, rinfo, wg, wu, wd, layer, tm=MOE_TM, tf=MOE_TF):
    T, D = h2.shape
    FF = wg.shape[-1]
    nf = FF // tf
    tile_e, n_valid, row_tok, row_dst, row_w = _moe_plan(rinfo, T, tm)
    n_tiles = tile_e.shape[0]

    def f_eff(r, f, nv):
        return jnp.where(r < nv[0], f, nf - 1)

    grid_spec = pltpu.PrefetchScalarGridSpec(
        num_scalar_prefetch=4,
        grid=(n_tiles, nf),
        in_specs=[pl.BlockSpec(memory_space=pl.ANY),
                  pl.BlockSpec((tm, LANES), lambda r, f, te, nv, tok, dst: (r, 0)),
                  pl.BlockSpec((None, None, D, tf), lambda r, f, te, nv, tok, dst: (layer, te[r], 0, f_eff(r, f, nv))),
                  pl.BlockSpec((None, None, D, tf), lambda r, f, te, nv, tok, dst: (layer, te[r], 0, f_eff(r, f, nv))),
                  pl.BlockSpec((None, None, tf, D), lambda r, f, te, nv, tok, dst: (layer, te[r], f_eff(r, f, nv), 0))],
        out_specs=pl.BlockSpec(memory_space=pl.ANY),
        scratch_shapes=[pltpu.VMEM((tm, D), F32), pltpu.VMEM((tm, D), BF16), pltpu.VMEM((tm, D), F32),
                        pltpu.SemaphoreType.DMA(()), pltpu.SemaphoreType.DMA(())],
    )
    return pl.pallas_call(
        functools.partial(_moe_kernel, tm=tm, nf=nf, n_real=TOP_K * T),
        grid_spec=grid_spec,
        out_shape=jax.ShapeDtypeStruct((TOP_K * T + 2 * tm, D), F32),
        compiler_params=_params("arbitrary", "arbitrary"),
        name="moe_routed",
    )(tile_e, n_valid, row_tok, row_dst, h2, row_w, wg, wu, wd)


def kernel(x_prompt, x_sample, state_hgrn, state_mlstm_C, state_mlstm_n, state_mlstm_m, state_gdn, state_conv, c_prompt, c_sample, w_in, b_in, hgrn_lb_logits, hgrn_norm_g, gdn_conv_w, gdn_A_log, gdn_dt_bias, gdn_norm_g, w_branch, w_merge, b_merge, w_out, ln_mix_g, ln_mix_b, w_ada, b_ada, w_router, b_router, w_exp_gate, w_exp_up, w_exp_down, ln_ffn_g, ln_ffn_b):
    depth = w_in.shape[0]
    Bp, Lp, D = x_prompt.shape
    Bs, Ls, _ = x_sample.shape
    Tp, Ts = Bp * Lp, Bs * Ls
    T = Tp + Ts
    alpha = (2 * depth) ** 0.25

    def tok_specs(tb):
        assert Lp % tb == 0 and tb % Ls == 0 and Ts % tb == 0
        npb, spb = Tp // tb, tb // Ls
        mp, ms = _mod_specs(npb, Bs, Lp // tb, Bp, spb, D)
        xp_s, xs_s = _group_specs(npb, tb, D)
        return dict(npb=npb, spb=spb, mp=mp, ms=ms, xp=xp_s, xs=xs_s, grid=(T // tb,),
                    tok=pl.BlockSpec((tb, D), lambda i: (i, 0)),
                    tok_hi=pl.BlockSpec((tb, D), lambda i: (T // tb + i, 0)),
                    lanes=pl.BlockSpec((tb, LANES), lambda i: (i, 0)))

    ts = tok_specs(TOK_BLOCK)
    tr = tok_specs(ROUTE_BLOCK)
    n_ab = 8 * WIDTH
    g0, c0 = n_ab, n_ab + 2 * N_HEADS
    c1 = c0 + 4 * WIDTH

    nc = Bp + Bs
    nc_pad = -(-nc // 16) * 16
    c_all = jnp.pad(jnp.concatenate([c_sample, c_prompt], axis=0), ((0, nc_pad - nc), (0, 0)))
    c_act = (c_all * jax.nn.sigmoid(c_all)).astype(BF16)

    wr = jnp.pad(w_router.astype(F32), ((0, 0), (0, LANES - N_EXPERTS)))
    br = jnp.pad(b_router.astype(F32), ((0, LANES - N_EXPERTS),)).reshape(1, LANES)
    vec_spec = pl.BlockSpec((None, 1, D), lambda i: (0, 0, 0))
    vec = lambda a, l: a[l].reshape(1, 1, D)
    zero_bias = jnp.zeros((1, D), F32)

    w_in_t = jnp.swapaxes(w_in, 1, 2)
    tn = min(MM_TN, D)
    tm = MM_TM if T % MM_TM == 0 else TOK_BLOCK
    xp, xs = x_prompt.reshape(Tp, D), x_sample.reshape(Ts, D)
    x_shapes = [jax.ShapeDtypeStruct((Tp, D), F32), jax.ShapeDtypeStruct((Ts, D), F32)]
    def state_bufs(B):
        z = lambda *s: jnp.zeros((depth, B) + s, F32)
        return (z(*_STATE_DIMS), [z(*_STATE_DIMS), z(*_VEC_DIMS), z(*_VEC_DIMS)],
                [z(*_STATE_DIMS), z(CONV_W - 1, 3 * WIDTH)])

    hg_p, ml_p, gd_p = state_bufs(Bp)
    hg_s, ml_s, gd_s = state_bufs(Bs)
    mlstm_n_in = state_mlstm_n.reshape(depth, Bs, N_HEADS, 1, HEAD_DIM)
    mlstm_m_in = jnp.broadcast_to(state_mlstm_m[..., None, None], (depth, Bs, N_HEADS, 1, HEAD_DIM))
    for l in range(depth):
        mod = _matmul(c_act, w_ada, (None, D, 1024), lambda n: (l, 0, n),
                      b_ada.reshape(depth, 1, 6 * D), (None, 1, 1024), lambda n: (l, 0, n),
                      6 * D, F32, tm=nc_pad, tn=1024, name="ada").reshape(nc_pad, 6, D)

        h = pl.pallas_call(
            functools.partial(_modulate_kernel, npb=ts["npb"], spb=ts["spb"], sh_idx=0, sc_idx=1),
            grid=ts["grid"],
            in_specs=[ts["xp"], ts["xs"], ts["mp"], ts["ms"]],
            out_specs=ts["tok"],
            out_shape=jax.ShapeDtypeStruct((T, D), BF16),
            compiler_params=_params("arbitrary"),
            name="modulate",
        )(xp, xs, mod, mod)

        b_in3 = b_in.reshape(depth, 1, -1)
        proj_ab = _matmul(h, w_in_t, (None, tn, D), lambda n: (l, n, 0),
                          b_in3, (None, 1, tn), lambda n: (l, 0, n),
                          n_ab, F32, tm=tm, tn=tn, name="proj_ab", w_transposed=True)
        proj_c = _matmul(h, w_in_t[l, c0:c1, :], (tn, D), lambda n: (n, 0),
                         b_in[l, c0:c1].reshape(1, -1), (1, tn), lambda n: (0, n),
                         4 * WIDTH, F32, tm=tm, tn=tn, name="proj_c", w_transposed=True)
        w_g = jnp.pad(jnp.concatenate([w_in_t[l, g0:c0, :], w_in_t[l, c1:, :]], axis=0),
                      ((0, LANES - 4 * N_HEADS), (0, 0)))
        b_g = jnp.pad(jnp.concatenate([b_in[l, g0:c0], b_in[l, c1:]]), ((0, LANES - 4 * N_HEADS),))
        gates = _matmul(h, w_g, (LANES, D), lambda n: (n, 0), b_g.reshape(1, LANES), (1, LANES),
                        lambda n: (0, n), LANES, F32, tm=tm, tn=LANES, name="proj_gates", w_transposed=True)
        gate = _matmul(h, w_merge, (None, None, D, tn), lambda n: (l, n // (D // tn), 0, n % (D // tn)),
                       b_merge.reshape(depth, 3, 1, D), (None, None, 1, tn),
                       lambda n: (l, n // (D // tn), 0, n % (D // tn)),
                       3 * D, BF16, tm=tm, tn=tn, name="merge_gates", act="sigmoid")

        conv_w_t = gdn_conv_w[l].T.astype(F32)
        y_init = jnp.zeros((T, WIDTH), F32)
        ya, hg_p = _hgrn(proj_ab, 0, 0, Bp, Lp, hgrn_lb_logits, hgrn_norm_g[l], l, None, (y_init, hg_p))
        ya, hg_s = _hgrn(proj_ab, 0, Tp, Bs, Ls, hgrn_lb_logits, hgrn_norm_g[l], l, state_hgrn, (ya, hg_s))
        yb, *ml_p = _mlstm(proj_ab, 4, gates, 0, Bp, Lp, l, None, (y_init, *ml_p))
        yb, *ml_s = _mlstm(proj_ab, 4, gates, Tp, Bs, Ls, l, (state_mlstm_C, mlstm_n_in, mlstm_m_in),
                           (yb, *ml_s))
        yc, *gd_p = _gdn(proj_c, 0, gates, 0, Bp, Lp, conv_w_t, gdn_A_log[l], gdn_dt_bias[l],
                         gdn_norm_g[l], l, None, (y_init, *gd_p))
        yc, *gd_s = _gdn(proj_c, 0, gates, Tp, Bs, Ls, conv_w_t, gdn_A_log[l], gdn_dt_bias[l],
                         gdn_norm_g[l], l, (state_gdn, state_conv), (yc, *gd_s))

        merged = _merge((ya, yb, yc), gate, w_branch, l)
        mix = _matmul(merged, w_out, (None, D, tn), lambda n: (l, 0, n),
                      zero_bias, (1, tn), lambda n: (0, n), D, F32, tm=tm, tn=tn, name="out_proj")

        xp, xs, h2, rinfo = pl.pallas_call(
            functools.partial(_post_route_kernel, npb=tr["npb"], spb=tr["spb"], alpha=alpha),
            grid=tr["grid"],
            in_specs=[tr["xp"], tr["xs"], tr["tok"], tr["mp"], tr["ms"], vec_spec, vec_spec,
                      pl.BlockSpec((D, LANES), lambda i: (0, 0)), pl.BlockSpec((1, LANES), lambda i: (0, 0))],
            out_specs=[tr["xp"], tr["xs"], tr["tok"], tr["lanes"]],
            out_shape=[x_shapes[0], x_shapes[1], jax.ShapeDtypeStruct((T, D), F32),
                       jax.ShapeDtypeStruct((T, LANES), F32)],
            compiler_params=_params("arbitrary"),
            name="post_route",
        )(xp, xs, mix, mod, mod, vec(ln_mix_g, l), vec(ln_mix_b, l), wr, br)

        ffn = _moe(h2, rinfo, w_exp_gate, w_exp_up, w_exp_down, l)
        xp, xs = pl.pallas_call(
            functools.partial(_post_final_kernel, npb=tr["npb"], spb=tr["spb"], alpha=alpha),
            grid=tr["grid"],
            in_specs=[tr["xp"], tr["xs"], tr["tok"], tr["tok_hi"], tr["mp"], tr["ms"], vec_spec, vec_spec],
            out_specs=[tr["xp"], tr["xs"]],
            out_shape=x_shapes,
            compiler_params=_params("arbitrary"),
            name="post_final",
        )(xp, xs, ffn, ffn, mod, mod, vec(ln_ffn_g, l), vec(ln_ffn_b, l))

    as_vec = lambda n, B: n.reshape(depth, B, N_HEADS, HEAD_DIM)
    as_scalar = lambda m: m[:, :, :, 0, 0]
    return (xp.reshape(Bp, Lp, D), xs.reshape(Bs, Ls, D), hg_p, hg_s, ml_p[0], ml_s[0],
            as_vec(ml_p[1], Bp), as_vec(ml_s[1], Bs), as_scalar(ml_p[2]), as_scalar(ml_s[2]),
            gd_p[0], gd_s[0], gd_p[1], gd_s[1])
```

```python
import functools
import math

import numpy as np
import jax
import jax.numpy as jnp
from jax import lax
from jax.experimental import pallas as pl
from jax.experimental.pallas import tpu as pltpu

F32 = jnp.float32
BF16 = jnp.bfloat16
HIGHEST = lax.Precision.HIGHEST

HEAD_DIM = 128
N_HEADS = 8
WIDTH = N_HEADS * HEAD_DIM
CHUNK = 64
CONV_W = 4
N_EXPERTS = 16
N_GROUPS = 4
EXPERTS_PER_GROUP = N_EXPERTS // N_GROUPS
TOP_K = 2
LN_EPS = 1e-5
NORM_EPS = 1e-6
LB_TINY = 1e-20
NEG_BIG = -1e30
LANES = 128
VMEM_LIMIT = 56 * 1024 * 1024
TOK_BLOCK = 256
ROUTE_BLOCK = 128
MM_TM = 1024
MM_TN = 512
MOE_TM = 256
MOE_SLICES = 8
HEADS = tuple(slice(h * HEAD_DIM, (h + 1) * HEAD_DIM) for h in range(N_HEADS))


def _params(*sem):
    return pltpu.CompilerParams(dimension_semantics=sem, vmem_limit_bytes=VMEM_LIMIT)


def _dot(a, b):
    return jnp.dot(a.astype(BF16), b.astype(BF16), preferred_element_type=F32)


def _dot_nt(a, b):
    return lax.dot_general(a.astype(BF16), b.astype(BF16), (((1,), (1,)), ((), ())),
                           preferred_element_type=F32)


def _dot_tn(a, b):
    return lax.dot_general(a.astype(BF16), b.astype(BF16), (((0,), (0,)), ((), ())),
                           preferred_element_type=F32)


def _dot_hi(a, b):
    return jnp.dot(a, b, precision=HIGHEST, preferred_element_type=F32)


def _split2(x):
    x1 = x.astype(BF16)
    return x1, (x - x1.astype(F32)).astype(BF16)


def _dot_exact_l(a_bf, x):
    x1, x2 = _split2(x)
    d = lambda p: jnp.dot(a_bf, p, preferred_element_type=F32)
    return d(x1) + d(x2)


def _dot_exact_r(x, b_bf):
    x1, x2 = _split2(x)
    d = lambda p: jnp.dot(p, b_bf, preferred_element_type=F32)
    return d(x1) + d(x2)


def _dot3(a, b):
    a1, a2 = _split2(a)
    b1, b2 = _split2(b)
    d = lambda p, q: jnp.dot(p, q, preferred_element_type=F32)
    return d(a1, b1) + (d(a1, b2) + d(a2, b1))


def _sigmoid(x):
    return jax.nn.sigmoid(x)


def _silu(x):
    return x * jax.nn.sigmoid(x)


def _log_sigmoid(x):
    return jnp.minimum(x, 0.0) - jnp.log1p(jnp.exp(-jnp.abs(x)))


def _softplus(x):
    return jnp.maximum(x, 0.0) + jnp.log1p(jnp.exp(-jnp.abs(x)))


def _logaddexp(a, b):
    return jnp.maximum(a, b) + jnp.log1p(jnp.exp(-jnp.abs(a - b)))


def _tri_masks(cn):
    r = lax.broadcasted_iota(jnp.int32, (cn, cn), 0)
    c = lax.broadcasted_iota(jnp.int32, (cn, cn), 1)
    return r >= c, r > c, r == c


def _rms_gate(o, ng, gate):
    return o * lax.rsqrt(jnp.mean(o * o, axis=-1, keepdims=True) + NORM_EPS) * ng * gate


def _mm_kernel(x_ref, w_ref, b_ref, o_ref, wbf_ref, *, act, w_transposed):
    @pl.when(pl.program_id(1) == 0)
    def _():
        if w_transposed:
            tn = w_ref.shape[0]
            step = min(tn, 256)
            for c in range(0, tn, step):
                wbf_ref[:, c:c + step] = w_ref[c:c + step, :].T.astype(BF16)
        else:
            wbf_ref[...] = w_ref[...].astype(BF16)

    acc = jnp.dot(x_ref[...], wbf_ref[...], preferred_element_type=F32) + b_ref[...]
    if act == "sigmoid":
        acc = _sigmoid(acc)
    o_ref[...] = acc.astype(o_ref.dtype)


def _matmul(x, w, w_block, w_index, b, b_block, b_index, n_out, out_dtype, *, tm, tn, name, act=None,
            w_transposed=False):
    M, K = x.shape
    tm = min(tm, M)
    assert M % tm == 0 and n_out % tn == 0, (M, tm, n_out, tn)
    return pl.pallas_call(
        functools.partial(_mm_kernel, act=act, w_transposed=w_transposed),
        grid=(n_out // tn, M // tm),
        in_specs=[pl.BlockSpec((tm, K), lambda n, m: (m, 0)),
                  pl.BlockSpec(w_block, lambda n, m: w_index(n)),
                  pl.BlockSpec(b_block, lambda n, m: b_index(n))],
        out_specs=pl.BlockSpec((tm, tn), lambda n, m: (m, n)),
        out_shape=jax.ShapeDtypeStruct((M, n_out), out_dtype),
        scratch_shapes=[pltpu.VMEM((K, tn), BF16)],
        compiler_params=_params("arbitrary", "arbitrary"),
        name=name,
    )(x, w, b)


def _with_mod(npb, mp_ref, ms_ref, body):
    i = pl.program_id(0)

    @pl.when(i < npb)
    def _():
        body(mp_ref[...], 0)

    @pl.when(i >= npb)
    def _():
        body(ms_ref[...], 1)


def _group_specs(npb, tb, D):
    xp = pl.BlockSpec((tb, D), lambda i: (jnp.minimum(i, npb - 1), 0))
    xs = pl.BlockSpec((tb, D), lambda i: (jnp.maximum(i - npb, 0), 0))
    return xp, xs


def _mod_specs(npb, n_sample_seq, blocks_per_prompt_seq, n_prompt_seq, spb, D):
    mp = pl.BlockSpec((1, 6, D), lambda i: (n_sample_seq + jnp.minimum(i // blocks_per_prompt_seq,
                                                                        n_prompt_seq - 1), 0, 0))
    ms = pl.BlockSpec((spb, 6, D), lambda i: (jnp.maximum(i - npb, 0), 0, 0))
    return mp, ms


def _modulate_kernel(xp_ref, xs_ref, mp_ref, ms_ref, o_ref, *, npb, spb, sh_idx, sc_idx):
    tb, D = xp_ref.shape

    def body(m, group):
        x = (xp_ref, xs_ref)[group][...].reshape(spb, tb // spb, D)
        h = x * (1.0 + m[:, sc_idx:sc_idx + 1, :]) + m[:, sh_idx:sh_idx + 1, :]
        o_ref[...] = h.reshape(tb, D).astype(o_ref.dtype)

    _with_mod(npb, mp_ref, ms_ref, body)


def _layer_norm(z, g, b):
    mu = jnp.mean(z, axis=-1, keepdims=True)
    zc = z - mu
    var = jnp.mean(zc * zc, axis=-1, keepdims=True)
    return zc * lax.rsqrt(var + LN_EPS) * g + b


def _route(logits):
    lane = lax.broadcasted_iota(jnp.int32, logits.shape, 1)
    valid = lane < N_EXPERTS
    lg = jnp.where(valid, logits, NEG_BIG)
    ex = jnp.where(valid, jnp.exp(lg - jnp.max(lg, axis=-1, keepdims=True)), 0.0)
    probs = ex / jnp.sum(ex, axis=-1, keepdims=True)
    big = LANES
    best = None
    for g in range(N_GROUPS):
        in_g = (lane >= g * EXPERTS_PER_GROUP) & (lane < (g + 1) * EXPERTS_PER_GROUP)
        m1 = jnp.max(jnp.where(in_g, probs, -1.0), axis=-1, keepdims=True)
        i1 = jnp.min(jnp.where(in_g & (probs == m1), lane, big), axis=-1, keepdims=True)
        rest = in_g & (lane != i1)
        m2 = jnp.max(jnp.where(rest, probs, -1.0), axis=-1, keepdims=True)
        i2 = jnp.min(jnp.where(rest & (probs == m2), lane, big), axis=-1, keepdims=True)
        cand = (m1 + m2, i1, i2, m1, m2)
        if best is None:
            best = cand
        else:
            better = cand[0] > best[0]
            best = tuple(jnp.where(better, c, o) for c, o in zip(cand, best))
    _, i1, i2, m1, m2 = best
    tot = m1 + m2
    return (jnp.where(lane == 0, i1.astype(F32), 0.0) + jnp.where(lane == 1, i2.astype(F32), 0.0)
            + jnp.where(lane == 2, m1 / tot, 0.0) + jnp.where(lane == 3, m2 / tot, 0.0))


def _post_route_kernel(xp_ref, xs_ref, y_ref, mp_ref, ms_ref, g_ref, b_ref, wr_ref, br_ref,
                       xop_ref, xos_ref, h_ref, r_ref, *, npb, spb, alpha):
    tb, D = xp_ref.shape

    def body(m, group):
        x = (xp_ref, xs_ref)[group][...].reshape(spb, tb // spb, D)
        y = y_ref[...].reshape(spb, tb // spb, D)
        xn = _layer_norm(alpha * x + m[:, 2:3, :] * y, g_ref[...], b_ref[...])
        (xop_ref, xos_ref)[group][...] = xn.reshape(tb, D)
        h2 = (xn * (1.0 + m[:, 4:5, :]) + m[:, 3:4, :]).reshape(tb, D)
        h_ref[...] = h2
        r_ref[...] = _route(_dot_hi(h2, wr_ref[...]) + br_ref[...])

    _with_mod(npb, mp_ref, ms_ref, body)


def _post_final_kernel(xp_ref, xs_ref, y0_ref, y1_ref, mp_ref, ms_ref, g_ref, b_ref,
                       xop_ref, xos_ref, *, npb, spb, alpha):
    tb, D = xp_ref.shape

    def body(m, group):
        x = (xp_ref, xs_ref)[group][...].reshape(spb, tb // spb, D)
        y = (y0_ref[...] + y1_ref[...]).reshape(spb, tb // spb, D)
        xn = _layer_norm(alpha * x + m[:, 5:6, :] * y, g_ref[...], b_ref[...])
        (xop_ref, xos_ref)[group][...] = xn.reshape(tb, D)

    _with_mod(npb, mp_ref, ms_ref, body)


def _seq_blocking(L):
    cn = math.gcd(L, CHUNK)
    lb = min(L, 512)
    assert L % lb == 0 and lb % cn == 0
    return cn, lb, L // lb


def _run_chunks(chunk, nchunk):
    if nchunk == 1:
        chunk(0, 0)
    else:
        lax.fori_loop(0, nchunk, chunk, 0)


def _hgrn_tables(cn):
    nlev = int(math.log2(cn))
    assert 1 << nlev == cn
    r = np.arange(cn)
    mats = [np.tril(np.ones((cn, cn)))]
    masks = [np.eye(cn)]
    j = r[None, :]
    for lev in range(nlev):
        s = cn >> (lev + 1)
        blk = 2 * s
        pos = r % blk
        ref = r - pos + s - 1
        second = pos >= s
        m_second = (j > ref[:, None]) & (j <= r[:, None])
        m_first = (j > r[:, None]) & (j <= ref[:, None])
        mats.append(np.where(second[:, None], m_second, m_first))
        same = (r[:, None] // blk) == (r[None, :] // blk)
        masks.append(same & second[:, None] & (~second)[None, :])
    return (np.concatenate(mats, 0).astype(np.float32), np.stack(masks).astype(np.float32))


def _hgrn_kernel(*refs, layer, cn, nchunk, has_state, n_alias):
    refs = list(refs)
    lbl_ref, ng_ref, mall_ref, masks_ref, q_ref, f_ref, i_ref, g_ref = refs[:8]
    pos = 8
    s0_ref = None
    if has_state:
        s0_ref = refs[pos]
        pos += 1
    pos += n_alias
    y_ref, s_ref, st_ref = refs[pos:pos + 3]
    j = pl.program_id(1)
    nlev = masks_ref.shape[0] - 1

    @pl.when(j == 0)
    def _():
        for h in range(N_HEADS):
            if has_state:
                st_ref[h] = s0_ref[0, h].T
            else:
                st_ref[h] = jnp.zeros((HEAD_DIM, HEAD_DIM), F32)

    logits = lbl_ref[...]
    e = jnp.exp(logits - jnp.max(logits, axis=0, keepdims=True))
    p = e / jnp.sum(e, axis=0, keepdims=True)
    lb = jnp.zeros((1, WIDTH), F32)
    for l in range(1, layer + 1):
        lb = lb + p[l:l + 1, :]
    log_lb = jnp.log(jnp.maximum(lb, LB_TINY))
    log1m_lb = jnp.log1p(-lb)
    one_m_lb = 1.0 - lb
    ng = ng_ref[...]
    mall = mall_ref[...]

    def chunk(c, carry):
        r0 = pl.multiple_of(c * cn, cn)
        rows = pl.ds(r0, cn)
        q = _silu(q_ref[rows, :])
        fr = f_ref[rows, :]
        v = i_ref[rows, :].astype(BF16)
        log_f = _logaddexp(log_lb, log1m_lb + _log_sigmoid(fr))
        k = one_m_lb * _sigmoid(-fr)
        dall = _dot_exact_l(mall, log_f)
        b = dall[:cn]
        qb = q.astype(BF16)
        kb = k.astype(BF16)
        att = [masks_ref[0] * _dot_nt(qb[:, cs], kb[:, cs]) for cs in HEADS]
        for lev in range(1, nlev + 1):
            ex = jnp.exp(dall[lev * cn:(lev + 1) * cn])
            qe = (q * ex).astype(BF16)
            ke = (k * ex).astype(BF16)
            att = [a + masks_ref[lev] * _dot_nt(qe[:, cs], ke[:, cs]) for a, cs in zip(att, HEADS)]
        st = [st_ref[h] for h in range(N_HEADS)]
        qd = (q * jnp.exp(b)).astype(BF16)
        o = [_dot(a, v[:, cs]) + _dot_nt(qd[:, cs], s) for a, cs, s in zip(att, HEADS, st)]
        b_last = b[cn - 1:cn]
        k_dec = (k * jnp.exp(b_last - b)).astype(BF16)
        dec = jnp.exp(b_last)
        for h, cs in enumerate(HEADS):
            st_ref[h] = st[h] * dec[:, cs] + _dot_tn(v[:, cs], k_dec[:, cs])
        gate = _sigmoid(g_ref[rows, :])
        for h, cs in enumerate(HEADS):
            y_ref[rows, cs] = _rms_gate(o[h], ng[:, cs], gate[:, cs])
        return carry

    _run_chunks(chunk, nchunk)

    @pl.when(j == pl.num_programs(1) - 1)
    def _():
        for h in range(N_HEADS):
            s_ref[0, h] = st_ref[h].T


def _mixer_specs(T_total, row0, L):
    cn, LB, ncb = _seq_blocking(L)
    assert row0 % LB == 0
    rb0 = row0 // LB
    col = lambda cidx: pl.BlockSpec((LB, WIDTH), lambda b, j: (rb0 + b * ncb + j, cidx))
    yspec = pl.BlockSpec((LB, WIDTH), lambda b, j: (rb0 + b * ncb + j, 0))
    return cn, LB, ncb, rb0, col, yspec


def _state_spec(layer, trailing):
    return pl.BlockSpec((None, 1) + trailing, lambda b, j: (layer, b) + (0,) * len(trailing))


_STATE_DIMS = (N_HEADS, HEAD_DIM, HEAD_DIM)
_VEC_DIMS = (N_HEADS, 1, HEAD_DIM)


def _full(a):
    return pl.BlockSpec(a.shape, lambda b, j: (0,) * a.ndim)


def _alias_outputs(args, in_specs, prevs):
    aliases = {}
    for k, prev in enumerate(prevs):
        args.append(prev)
        in_specs.append(pl.BlockSpec(memory_space=pl.ANY))
        aliases[len(args) - 1] = k
    return aliases


def _shapes(prevs):
    return [jax.ShapeDtypeStruct(p.shape, p.dtype) for p in prevs]


def _hgrn(proj, a_col0, row0, B, L, lb_logits, norm_g, layer, state, prevs):
    cn, LB, ncb, rb0, col, yspec = _mixer_specs(proj.shape[0], row0, L)
    mall, masks = _hgrn_tables(cn)
    mall = jnp.asarray(mall, BF16)
    masks = jnp.asarray(masks)
    ng = jnp.tile(norm_g.reshape(1, HEAD_DIM).astype(F32), (1, N_HEADS))
    sspec = _state_spec(layer, _STATE_DIMS)
    args = [lb_logits, ng, mall, masks, proj, proj, proj, proj]
    in_specs = [_full(lb_logits), _full(ng), _full(mall), _full(masks)] + [col(a_col0 + t) for t in range(4)]
    if state is not None:
        args.append(state)
        in_specs.append(sspec)
    aliases = _alias_outputs(args, in_specs, prevs)
    return pl.pallas_call(
        functools.partial(_hgrn_kernel, layer=layer, cn=cn, nchunk=LB // cn,
                          has_state=state is not None, n_alias=len(prevs)),
        grid=(B, ncb),
        in_specs=in_specs,
        out_specs=[yspec, sspec],
        out_shape=_shapes(prevs),
        scratch_shapes=[pltpu.VMEM((N_HEADS, HEAD_DIM, HEAD_DIM), F32)],
        input_output_aliases=aliases,
        compiler_params=_params("arbitrary", "arbitrary"),
        name="hgrn2",
    )(*args)


def _head_cols(mat, base):
    return [mat[:, base + h:base + h + 1] for h in range(N_HEADS)]


def _head_rows(mat, base):
    return [mat[base + h:base + h + 1, :] for h in range(N_HEADS)]


def _mlstm_kernel(*refs, cn, nchunk, has_state, n_alias):
    refs = list(refs)
    gt_ref, gtt_ref, q_ref, k_ref, v_ref, o_ref = refs[:6]
    pos = 6
    if has_state:
        c0_ref, n0_ref, m0_ref = refs[pos:pos + 3]
        pos += 3
    pos += n_alias
    y_ref, c_ref, n_ref, m_ref = refs[pos:pos + 4]
    j = pl.program_id(1)

    @pl.when(j == 0)
    def _():
        if has_state:
            c_ref[...] = c0_ref[...]
            n_ref[...] = n0_ref[...]
            m_ref[...] = m0_ref[...]
        else:
            c_ref[...] = jnp.zeros(c_ref.shape, F32)
            n_ref[...] = jnp.zeros(n_ref.shape, F32)
            m_ref[...] = jnp.zeros(m_ref.shape, F32)

    incl, _, _ = _tri_masks(cn)
    tril = incl.astype(BF16)
    triu = jnp.logical_not(_tri_masks(cn)[1]).astype(BF16)

    def chunk(c, carry):
        r0 = pl.multiple_of(c * cn, cn)
        rows = pl.ds(r0, cn)
        gates = gt_ref[rows, :]
        gates_r = gtt_ref[c]
        f_c = _dot_exact_l(tril, _log_sigmoid(gates))
        f_r = _dot_exact_r(_log_sigmoid(gates_r), triu)
        Fc = _head_cols(f_c, N_HEADS)
        ic = _head_cols(gates, 0)
        Fr = _head_rows(f_r, N_HEADS)
        ir = _head_rows(gates_r, 0)
        q = [q_ref[rows, cs].astype(BF16) for cs in HEADS]
        k = [k_ref[rows, cs] * (HEAD_DIM ** -0.5) for cs in HEADS]
        kb = [x.astype(BF16) for x in k]
        v = [v_ref[rows, cs].astype(BF16) for cs in HEADS]
        C = [c_ref[0, h] for h in range(N_HEADS)]
        n = [n_ref[0, h] for h in range(N_HEADS)]
        m = [m_ref[0, h][:, 0:1] for h in range(N_HEADS)]
        qk = [_dot_nt(a, b) for a, b in zip(q, kb)]
        qc = [_dot(a, b) for a, b in zip(q, C)]
        raw = [a - b + d for a, b, d in zip(Fc, Fr, ir)]
        inter = [a + b for a, b in zip(Fc, m)]
        m_t = [jnp.maximum(a, jnp.max(jnp.where(incl, r, NEG_BIG), axis=-1, keepdims=True))
               for a, r in zip(inter, raw)]
        w_inter = [jnp.exp(a - b) for a, b in zip(inter, m_t)]
        s = [jnp.where(incl, a * jnp.exp(jnp.where(incl, r - mt, 0.0)), 0.0)
             for a, r, mt in zip(qk, raw, m_t)]
        num = [_dot(a, b) + w * d for a, b, w, d in zip(s, v, w_inter, qc)]
        den = [jnp.sum(a, axis=-1, keepdims=True)
               + w * jnp.sum(q_ref[rows, cs] * nn, axis=-1, keepdims=True)
               for a, w, cs, nn in zip(s, w_inter, HEADS, n)]
        hh = [a / jnp.maximum(jnp.abs(d), jnp.exp(-mt)) for a, d, mt in zip(num, den, m_t)]
        for h, cs in enumerate(HEADS):
            m_new = m_t[h][cn - 1:cn]
            w_c = jnp.exp(inter[h][cn - 1:cn] - m_new)
            w_s = jnp.exp(Fc[h][cn - 1:cn] - Fc[h] + ic[h] - m_new)
            ks = k[h] * w_s
            c_ref[0, h] = w_c * C[h] + _dot_tn(ks, v[h])
            n_ref[0, h] = w_c * n[h] + jnp.sum(ks, axis=0, keepdims=True)
            m_ref[0, h] = jnp.broadcast_to(m_new, (1, HEAD_DIM))
            y_ref[rows, cs] = _sigmoid(o_ref[rows, cs]) * hh[h]
        return carry

    _run_chunks(chunk, nchunk)


def _gates_t(gates, row0, B, L, cn):
    g = gates[row0:row0 + B * L, :4 * N_HEADS]
    return g.reshape(B * L // cn, cn, 4 * N_HEADS).transpose(0, 2, 1)


def _mlstm(proj, b_col0, gates, row0, B, L, layer, state, prevs):
    cn, LB, ncb, rb0, col, yspec = _mixer_specs(proj.shape[0], row0, L)
    cspec = _state_spec(layer, _STATE_DIMS)
    nspec = _state_spec(layer, _VEC_DIMS)
    gt = _gates_t(gates, row0, B, L, cn)
    args = [gates, gt, proj, proj, proj, proj]
    in_specs = [pl.BlockSpec((LB, LANES), lambda b, j: (rb0 + b * ncb + j, 0)),
                pl.BlockSpec((LB // cn, 4 * N_HEADS, cn), lambda b, j: (b * ncb + j, 0, 0))]
    in_specs += [col(b_col0 + t) for t in range(4)]
    if state is not None:
        args += list(state)
        in_specs += [cspec, nspec, nspec]
    aliases = _alias_outputs(args, in_specs, prevs)
    return pl.pallas_call(
        functools.partial(_mlstm_kernel, cn=cn, nchunk=LB // cn, has_state=state is not None,
                          n_alias=len(prevs)),
        grid=(B, ncb),
        in_specs=in_specs,
        out_specs=[yspec, cspec, nspec, nspec],
        out_shape=_shapes(prevs),
        input_output_aliases=aliases,
        compiler_params=_params("arbitrary", "arbitrary"),
        name="mlstm",
    )(*args)


_CONV_PAD = 8


def _gdn_kernel(*refs, cn, nchunk, has_state, n_alias):
    refs = list(refs)
    (gt_ref, gtt_ref, cw_ref, al_ref, dtb_ref, alc_ref, dtbc_ref, ng_ref,
     xq_ref, xk_ref, xv_ref, g_ref) = refs[:12]
    pos = 12
    if has_state:
        s0_ref, b0_ref = refs[pos:pos + 2]
        pos += 2
    pos += n_alias
    y_ref, s_ref, buf_ref, xp_ref, cv_ref = refs[pos:pos + 5]
    j = pl.program_id(1)
    LB = xq_ref.shape[0]
    hist = CONV_W - 1
    h0 = _CONV_PAD - hist

    @pl.when(j == 0)
    def _():
        if has_state:
            s_ref[...] = s0_ref[...]
            xp_ref[h0:_CONV_PAD, :] = b0_ref[0]
        else:
            s_ref[...] = jnp.zeros(s_ref.shape, F32)
            xp_ref[h0:_CONV_PAD, :] = jnp.zeros((hist, 3 * WIDTH), F32)

    xp_ref[_CONV_PAD:_CONV_PAD + LB, 0:WIDTH] = xq_ref[...]
    xp_ref[_CONV_PAD:_CONV_PAD + LB, WIDTH:2 * WIDTH] = xk_ref[...]
    xp_ref[_CONV_PAD:_CONV_PAD + LB, 2 * WIDTH:3 * WIDTH] = xv_ref[...]
    acc = xp_ref[h0:h0 + LB, :] * cw_ref[0:1, :]
    for t in range(1, CONV_W):
        acc = acc + xp_ref[h0 + t:h0 + t + LB, :] * cw_ref[t:t + 1, :]
    cv_ref[...] = _silu(acc)
    new_hist = xp_ref[_CONV_PAD + LB - hist:_CONV_PAD + LB, :]
    xp_ref[h0:_CONV_PAD, :] = new_hist
    buf_ref[0] = new_hist

    incl, strict, eye = _tri_masks(cn)
    tril = incl.astype(BF16)
    triu = jnp.logical_not(strict).astype(BF16)
    eye_f = eye.astype(F32)
    lane = lax.broadcasted_iota(jnp.int32, (1, LANES), 1)
    is_a = (lane >= 2 * N_HEADS) & (lane < 3 * N_HEADS)
    sub = lax.broadcasted_iota(jnp.int32, (4 * N_HEADS, 1), 0)
    is_a_r = (sub >= 2 * N_HEADS) & (sub < 3 * N_HEADS)
    neg_rate = -jnp.exp(al_ref[...])
    dtb = dtb_ref[...]
    neg_rate_r = -jnp.exp(alc_ref[...])
    dtb_r = dtbc_ref[...]
    ng = ng_ref[...]
    nsq = int(math.log2(cn))

    def chunk(c, carry):
        r0 = pl.multiple_of(c * cn, cn)
        rows = pl.ds(r0, cn)
        gates = gt_ref[rows, :]
        gates_r = gtt_ref[c]
        g_c = jnp.where(is_a, neg_rate * _softplus(gates + dtb), 0.0)
        g_r = jnp.where(is_a_r, neg_rate_r * _softplus(gates_r + dtb_r), 0.0)
        gam_c = _head_cols(_dot_exact_l(tril, g_c), 2 * N_HEADS)
        gam_r = _head_rows(_dot_exact_r(g_r, triu), 2 * N_HEADS)
        beta = _head_cols(_sigmoid(gates), 3 * N_HEADS)
        q, k, v = [], [], []
        for cs in HEADS:
            qr = cv_ref[rows, cs]
            kr = cv_ref[rows, slice(WIDTH + cs.start, WIDTH + cs.stop)]
            v.append(cv_ref[rows, slice(2 * WIDTH + cs.start, 2 * WIDTH + cs.stop)])
            q.append(qr * (lax.rsqrt(jnp.sum(qr * qr, axis=-1, keepdims=True) + NORM_EPS) * (HEAD_DIM ** -0.5)))
            k.append(kr * lax.rsqrt(jnp.sum(kr * kr, axis=-1, keepdims=True) + NORM_EPS))
        kb = [x.astype(BF16) for x in k]
        S = [s_ref[0, h] for h in range(N_HEADS)]
        kk = [_dot_nt(a, a) for a in kb]
        qk = [_dot_nt(a, b) for a, b in zip(q, kb)]
        dec = [jnp.exp(jnp.where(incl, a - b, 0.0)) for a, b in zip(gam_c, gam_r)]
        pw = [jnp.where(strict, -(bt * a * d), 0.0) for bt, a, d in zip(beta, kk, dec)]
        t_inv = [eye_f + a for a in pw]
        for _ in range(nsq - 1):
            pw = [_dot3(a, a) for a in pw]
            t_inv = [t + _dot3(t, a) for t, a in zip(t_inv, pw)]
        eg = [jnp.exp(a) for a in gam_c]
        sol_v = [_dot3(t, bt * x) for t, bt, x in zip(t_inv, beta, v)]
        sol_k = [_dot3(t, (bt * e) * x) for t, bt, e, x in zip(t_inv, beta, eg, k)]
        u = [a - _dot(b, s) for a, b, s in zip(sol_v, sol_k, S)]
        o = [_dot(jnp.where(incl, a * d, 0.0), uu) + _dot(qq * e, s)
             for a, d, uu, qq, e, s in zip(qk, dec, u, q, eg, S)]
        for h, cs in enumerate(HEADS):
            g_last = gam_c[h][cn - 1:cn]
            s_ref[0, h] = jnp.exp(g_last) * S[h] + _dot_tn(k[h] * jnp.exp(g_last - gam_c[h]), u[h])
            y_ref[rows, cs] = _rms_gate(o[h], ng, _silu(g_ref[rows, cs]))
        return carry

    _run_chunks(chunk, nchunk)


def _gdn(proj, c_col0, gates, row0, B, L, conv_w_t, a_log, dt_bias, norm_g, layer, state, prevs):
    cn, LB, ncb, rb0, col, yspec = _mixer_specs(proj.shape[0], row0, L)
    hist = CONV_W - 1
    sspec = _state_spec(layer, _STATE_DIMS)
    bspec = _state_spec(layer, (hist, 3 * WIDTH))
    a0, a1 = 2 * N_HEADS, 3 * N_HEADS
    al = jnp.zeros((1, LANES), F32).at[0, a0:a1].set(a_log.astype(F32))
    db = jnp.zeros((1, LANES), F32).at[0, a0:a1].set(dt_bias.astype(F32))
    alc = al[0, :4 * N_HEADS].reshape(4 * N_HEADS, 1)
    dbc = db[0, :4 * N_HEADS].reshape(4 * N_HEADS, 1)
    ng = norm_g.reshape(1, HEAD_DIM).astype(F32)
    gt = _gates_t(gates, row0, B, L, cn)
    args = [gates, gt, conv_w_t, al, db, alc, dbc, ng, proj, proj, proj, proj]
    in_specs = [pl.BlockSpec((LB, LANES), lambda b, j: (rb0 + b * ncb + j, 0)),
                pl.BlockSpec((LB // cn, 4 * N_HEADS, cn), lambda b, j: (b * ncb + j, 0, 0)),
                _full(conv_w_t), _full(al), _full(db), _full(alc), _full(dbc), _full(ng)]
    in_specs += [col(c_col0 + t) for t in range(4)]
    if state is not None:
        args += [state[0], state[1]]
        in_specs += [sspec, bspec]
    aliases = _alias_outputs(args, in_specs, prevs)
    return pl.pallas_call(
        functools.partial(_gdn_kernel, cn=cn, nchunk=LB // cn, has_state=state is not None,
                          n_alias=len(prevs)),
        grid=(B, ncb),
        in_specs=in_specs,
        out_specs=[yspec, sspec, bspec],
        out_shape=_shapes(prevs),
        scratch_shapes=[pltpu.VMEM((_CONV_PAD + LB, 3 * WIDTH), F32),
                        pltpu.VMEM((LB, 3 * WIDTH), F32)],
        input_output_aliases=aliases,
        compiler_params=_params("arbitrary", "arbitrary"),
        name="gdn",
    )(*args)


def _merge_kernel(ya_ref, yb_ref, yc_ref, ga_ref, gb_ref, gc_ref, wb_ref, o_ref, wbf_ref):
    @pl.when(pl.program_id(1) == 0)
    def _():
        wbf_ref[...] = wb_ref[...].astype(BF16)

    acc = None
    for jdx, (y_ref, g_ref) in enumerate(((ya_ref, ga_ref), (yb_ref, gb_ref), (yc_ref, gc_ref))):
        br = jnp.dot(y_ref[...].astype(BF16), wbf_ref[jdx], preferred_element_type=F32)
        term = g_ref[...].astype(F32) * br
        acc = term if acc is None else acc + term
    o_ref[...] = acc.astype(o_ref.dtype)


def _merge(ys, gate, w_branch, layer, tm=256, tn=1024):
    T, W = ys[0].shape
    D = w_branch.shape[-1]
    tn = min(tn, D)
    nb = D // tn
    yspec = pl.BlockSpec((tm, W), lambda n, m: (m, 0))
    gspec = lambda jdx: pl.BlockSpec((tm, tn), lambda n, m: (m, jdx * nb + n))
    return pl.pallas_call(
        _merge_kernel,
        grid=(nb, T // tm),
        in_specs=[yspec, yspec, yspec, gspec(0), gspec(1), gspec(2),
                  pl.BlockSpec((None, 3, W, tn), lambda n, m: (layer, 0, 0, n))],
        out_specs=pl.BlockSpec((tm, tn), lambda n, m: (m, n)),
        out_shape=jax.ShapeDtypeStruct((T, D), BF16),
        scratch_shapes=[pltpu.VMEM((3, W, tn), BF16)],
        compiler_params=_params("arbitrary", "arbitrary"),
        name="merge",
    )(ys[0], ys[1], ys[2], gate, gate, gate, w_branch)


def _moe_kernel(it_tile_ref, it_slice_ref, it_flag_ref, te_ref, nv_ref, tok_ref, dst_ref,
                h_hbm, roww_ref, wg_ref, wu_ref, wd_ref, out_hbm,
                xbuf, xb_ref, acc_ref, wgb_ref, wub_ref, wdb_ref, gsem, ssem, *, tm, n_real):
    i = pl.program_id(0)
    r = it_tile_ref[i]
    s = it_slice_ref[i]
    flag = it_flag_ref[i]
    nv = nv_ref[0]
    dk = wg_ref.shape[0]
    fk = wd_ref.shape[0]
    scatter_rows = min(tm, 128)

    def gather_start(tile):
        base = tile * tm

        def body(i, carry):
            tok = tok_ref[base + i]
            pltpu.make_async_copy(h_hbm.at[pl.ds(tok, 1)], xbuf.at[pl.ds(i, 1)], gsem).start()
            return carry

        lax.fori_loop(0, tm, body, 0, unroll=8)

    def gather_wait():
        pltpu.make_async_copy(h_hbm.at[pl.ds(0, tm)], xbuf, gsem).wait()

    def scatter_wait():
        pltpu.make_async_copy(acc_ref, out_hbm.at[pl.ds(0, tm)], ssem).wait()

    @pl.when(i == 0)
    def _():
        acc_ref[...] = jnp.zeros(acc_ref.shape, F32)
        for half in range(2):
            fill = pltpu.make_async_copy(acc_ref, out_hbm.at[pl.ds(n_real + half * tm, tm)], ssem)
            fill.start()
            fill.wait()
        gather_start(0)

    @pl.when((flag & 1) != 0)
    def _():
        k0 = pl.multiple_of(s * dk, dk)
        wgb_ref[pl.ds(k0, dk), :] = wg_ref[...].astype(BF16)
        wub_ref[pl.ds(k0, dk), :] = wu_ref[...].astype(BF16)
        f0 = pl.multiple_of(s * fk, fk)
        wdb_ref[pl.ds(f0, fk), :] = wd_ref[...].astype(BF16)

    @pl.when((flag & 2) != 0)
    def _():
        gather_wait()
        xb_ref[...] = xbuf[...].astype(BF16)
        nxt = jnp.minimum(r + 1, nv - 1)
        for j in range(tm):
            tok = tok_ref[nxt * tm + j]
            pltpu.make_async_copy(h_hbm.at[pl.ds(tok, 1)], xbuf.at[pl.ds(j, 1)], gsem).start()

        x = xb_ref[...]
        g = jnp.dot(x, wgb_ref[...], preferred_element_type=F32)
        u = jnp.dot(x, wub_ref[...], preferred_element_type=F32)
        he = (_silu(g) * u).astype(BF16)

        @pl.when(r > 0)
        def _():
            scatter_wait()

        for c in range(tm // scatter_rows):
            rows = slice(c * scatter_rows, (c + 1) * scatter_rows)
            y = jnp.dot(he[rows], wdb_ref[...], preferred_element_type=F32)
            acc_ref[rows, :] = y * roww_ref[rows, 0:1]
            for j in range(scatter_rows):
                row = c * scatter_rows + j
                dst = dst_ref[r * tm + row]
                pltpu.make_async_copy(acc_ref.at[pl.ds(row, 1)], out_hbm.at[pl.ds(dst, 1)], ssem).start()

        @pl.when(r == nv - 1)
        def _():
            scatter_wait()
            gather_wait()


def _moe_plan(rinfo, T, tm, ns):
    ids = rinfo[:, 0:TOP_K].astype(jnp.int32)
    wts = rinfo[:, TOP_K:2 * TOP_K]
    n_assign = TOP_K * T
    n_rows = n_assign + N_EXPERTS * tm
    n_tiles = n_rows // tm
    e_flat = ids.reshape(-1)
    onehot = (e_flat[:, None] == jnp.arange(N_EXPERTS, dtype=jnp.int32)[None, :]).astype(jnp.int32)
    rank = jnp.take_along_axis(jnp.cumsum(onehot, axis=0), e_flat[:, None], axis=1)[:, 0] - 1
    counts = jnp.sum(onehot, axis=0)
    padded = ((counts + tm - 1) // tm) * tm
    ends = jnp.cumsum(padded)
    starts = ends - padded
    dest = starts[e_flat] + rank
    upd = jnp.stack([jnp.arange(1, n_assign + 1, dtype=jnp.int32).astype(F32), wts.reshape(-1)], axis=1)
    rows = jnp.zeros((n_rows, 2), F32).at[dest].set(upd)
    row_a = rows[:, 0].astype(jnp.int32) - 1
    row_w = rows[:, 1]
    is_real = row_a >= 0
    tok = jnp.maximum(row_a, 0) // TOP_K
    slot = jnp.maximum(row_a, 0) % TOP_K
    p = jnp.arange(n_rows, dtype=jnp.int32)
    dump = n_assign + ((p // tm) % 2) * tm + (p % tm)
    row_tok = jnp.where(is_real, tok, 0)
    row_dst = jnp.where(is_real, slot * T + tok, dump)
    n_valid = (ends[-1] // tm).astype(jnp.int32)
    tile_start = jnp.arange(n_tiles, dtype=jnp.int32) * tm
    tile_e = jnp.minimum(jnp.searchsorted(ends, tile_start, side="right"), N_EXPERTS - 1).astype(jnp.int32)
    valid_t = tile_start < ends[-1]
    last_e = tile_e[jnp.maximum(n_valid - 1, 0)]
    tile_e = jnp.where(valid_t, tile_e, last_e)

    prev_e = jnp.concatenate([jnp.full((1,), -1, jnp.int32), tile_e[:-1]])
    first = valid_t & (tile_e != prev_e)
    cnt = jnp.where(valid_t, jnp.where(first, ns, 1), 0).astype(jnp.int32)
    item_end = jnp.cumsum(cnt)
    item_start = item_end - cnt
    n_items = n_tiles + N_EXPERTS * (ns - 1)
    it = jnp.arange(n_items, dtype=jnp.int32)
    live = it < item_end[-1]
    it_tile = jnp.minimum(jnp.searchsorted(item_end, it, side="right"), n_tiles - 1).astype(jnp.int32)
    it_tile = jnp.where(live, it_tile, jnp.maximum(n_valid - 1, 0))
    k = it - item_start[it_tile]
    loads = live & first[it_tile]
    it_slice = jnp.where(loads, k, ns - 1).astype(jnp.int32)
    computes = live & (~first[it_tile] | (k == ns - 1))
    it_flag = loads.astype(jnp.int32) + 2 * computes.astype(jnp.int32)
    return (it_tile, it_slice, it_flag, tile_e, n_valid.reshape(1), row_tok, row_dst,
            jnp.broadcast_to(row_w[:, None], (n_rows, LANES)))


def _moe(h2, rinfo, wg, wu, wd, layer, tm=MOE_TM, ns=MOE_SLICES):
    T, D = h2.shape
    FF = wg.shape[-1]
    assert D % ns == 0 and FF % ns == 0
    dk, fk = D // ns, FF // ns
    it_tile, it_slice, it_flag, tile_e, n_valid, row_tok, row_dst, row_w = _moe_plan(rinfo, T, tm, ns)
    w_idx = lambda i, itt, its, itf, te, nv, tok, dst: (layer, te[itt[i]], its[i], 0)
    grid_spec = pltpu.PrefetchScalarGridSpec(
        num_scalar_prefetch=7,
        grid=(it_tile.shape[0],),
        in_specs=[pl.BlockSpec(memory_space=pl.ANY),
                  pl.BlockSpec((tm, LANES), lambda i, itt, its, itf, te, nv, tok, dst: (itt[i], 0)),
                  pl.BlockSpec((None, None, dk, FF), w_idx),
                  pl.BlockSpec((None, None, dk, FF), w_idx),
                  pl.BlockSpec((None, None, fk, D), w_idx)],
        out_specs=pl.BlockSpec(memory_space=pl.ANY),
        scratch_shapes=[pltpu.VMEM((tm, D), F32), pltpu.VMEM((tm, D), BF16), pltpu.VMEM((tm, D), F32),
                        pltpu.VMEM((D, FF), BF16), pltpu.VMEM((D, FF), BF16), pltpu.VMEM((FF, D), BF16),
                        pltpu.SemaphoreType.DMA(()), pltpu.SemaphoreType.DMA(())],
    )
    return pl.pallas_call(
        functools.partial(_moe_kernel, tm=tm, n_real=TOP_K * T),
        grid_spec=grid_spec,
        out_shape=jax.ShapeDtypeStruct((TOP_K * T + 2 * tm, D), F32),
        compiler_params=_params("arbitrary"),
        name="moe_routed",
    )(it_tile, it_slice, it_flag, tile_e, n_valid, row_tok, row_dst, h2, row_w, wg, wu, wd)


def kernel(x_prompt, x_sample, state_hgrn, state_mlstm_C, state_mlstm_n, state_mlstm_m, state_gdn, state_conv, c_prompt, c_sample, w_in, b_in, hgrn_lb_logits, hgrn_norm_g, gdn_conv_w, gdn_A_log, gdn_dt_bias, gdn_norm_g, w_branch, w_merge, b_merge, w_out, ln_mix_g, ln_mix_b, w_ada, b_ada, w_router, b_router, w_exp_gate, w_exp_up, w_exp_down, ln_ffn_g, ln_ffn_b):
    depth = w_in.shape[0]
    Bp, Lp, D = x_prompt.shape
    Bs, Ls, _ = x_sample.shape
    Tp, Ts = Bp * Lp, Bs * Ls
    T = Tp + Ts
    alpha = (2 * depth) ** 0.25

    def tok_specs(tb):
        assert Lp % tb == 0 and tb % Ls == 0 and Ts % tb == 0
        npb, spb = Tp // tb, tb // Ls
        mp, ms = _mod_specs(npb, Bs, Lp // tb, Bp, spb, D)
        xp_s, xs_s = _group_specs(npb, tb, D)
        return dict(npb=npb, spb=spb, mp=mp, ms=ms, xp=xp_s, xs=xs_s, grid=(T // tb,),
                    tok=pl.BlockSpec((tb, D), lambda i: (i, 0)),
                    tok_hi=pl.BlockSpec((tb, D), lambda i: (T // tb + i, 0)),
                    lanes=pl.BlockSpec((tb, LANES), lambda i: (i, 0)))

    ts = tok_specs(TOK_BLOCK)
    tr = tok_specs(ROUTE_BLOCK)
    n_ab = 8 * WIDTH
    g0, c0 = n_ab, n_ab + 2 * N_HEADS
    c1 = c0 + 4 * WIDTH

    nc = Bp + Bs
    nc_pad = -(-nc // 16) * 16
    c_all = jnp.pad(jnp.concatenate([c_sample, c_prompt], axis=0), ((0, nc_pad - nc), (0, 0)))
    c_act = (c_all * jax.nn.sigmoid(c_all)).astype(BF16)

    wr = jnp.pad(w_router.astype(F32), ((0, 0), (0, LANES - N_EXPERTS)))
    br = jnp.pad(b_router.astype(F32), ((0, LANES - N_EXPERTS),)).reshape(1, LANES)
    vec_spec = pl.BlockSpec((None, 1, D), lambda i: (0, 0, 0))
    vec = lambda a, l: a[l].reshape(1, 1, D)
    zero_bias = jnp.zeros((1, D), F32)

    w_in_t = jnp.swapaxes(w_in, 1, 2)
    tn = min(MM_TN, D)
    tm = MM_TM if T % MM_TM == 0 else TOK_BLOCK
    xp, xs = x_prompt.reshape(Tp, D), x_sample.reshape(Ts, D)
    x_shapes = [jax.ShapeDtypeStruct((Tp, D), F32), jax.ShapeDtypeStruct((Ts, D), F32)]
    def state_bufs(B):
        z = lambda *s: jnp.zeros((depth, B) + s, F32)
        return (z(*_STATE_DIMS), [z(*_STATE_DIMS), z(*_VEC_DIMS), z(*_VEC_DIMS)],
                [z(*_STATE_DIMS), z(CONV_W - 1, 3 * WIDTH)])

    hg_p, ml_p, gd_p = state_bufs(Bp)
    hg_s, ml_s, gd_s = state_bufs(Bs)
    mlstm_n_in = state_mlstm_n.reshape(depth, Bs, N_HEADS, 1, HEAD_DIM)
    mlstm_m_in = jnp.broadcast_to(state_mlstm_m[..., None, None], (depth, Bs, N_HEADS, 1, HEAD_DIM))
    for l in range(depth):
        mod = _matmul(c_act, w_ada, (None, D, 1024), lambda n: (l, 0, n),
                      b_ada.reshape(depth, 1, 6 * D), (None, 1, 1024), lambda n: (l, 0, n),
                      6 * D, F32, tm=nc_pad, tn=1024, name="ada").reshape(nc_pad, 6, D)

        h = pl.pallas_call(
            functools.partial(_modulate_kernel, npb=ts["npb"], spb=ts["spb"], sh_idx=0, sc_idx=1),
            grid=ts["grid"],
            in_specs=[ts["xp"], ts["xs"], ts["mp"], ts["ms"]],
            out_specs=ts["tok"],
            out_shape=jax.ShapeDtypeStruct((T, D), BF16),
            compiler_params=_params("arbitrary"),
            name="modulate",
        )(xp, xs, mod, mod)

        b_in3 = b_in.reshape(depth, 1, -1)
        proj_ab = _matmul(h, w_in_t, (None, tn, D), lambda n: (l, n, 0),
                          b_in3, (None, 1, tn), lambda n: (l, 0, n),
                          n_ab, F32, tm=tm, tn=tn, name="proj_ab", w_transposed=True)
        proj_c = _matmul(h, w_in_t[l, c0:c1, :], (tn, D), lambda n: (n, 0),
                         b_in[l, c0:c1].reshape(1, -1), (1, tn), lambda n: (0, n),
                         4 * WIDTH, F32, tm=tm, tn=tn, name="proj_c", w_transposed=True)
        w_g = jnp.pad(jnp.concatenate([w_in_t[l, g0:c0, :], w_in_t[l, c1:, :]], axis=0),
                      ((0, LANES - 4 * N_HEADS), (0, 0)))
        b_g = jnp.pad(jnp.concatenate([b_in[l, g0:c0], b_in[l, c1:]]), ((0, LANES - 4 * N_HEADS),))
        gates = _matmul(h, w_g, (LANES, D), lambda n: (n, 0), b_g.reshape(1, LANES), (1, LANES),
                        lambda n: (0, n), LANES, F32, tm=tm, tn=LANES, name="proj_gates", w_transposed=True)
        gate = _matmul(h, w_merge, (None, None, D, tn), lambda n: (l, n // (D // tn), 0, n % (D // tn)),
                       b_merge.reshape(depth, 3, 1, D), (None, None, 1, tn),
                       lambda n: (l, n // (D // tn), 0, n % (D // tn)),
                       3 * D, BF16, tm=tm, tn=tn, name="merge_gates", act="sigmoid")

        conv_w_t = gdn_conv_w[l].T.astype(F32)
        y_init = jnp.zeros((T, WIDTH), F32)
        ya, hg_p = _hgrn(proj_ab, 0, 0, Bp, Lp, hgrn_lb_logits, hgrn_norm_g[l], l, None, (y_init, hg_p))
        ya, hg_s = _hgrn(proj_ab, 0, Tp, Bs, Ls, hgrn_lb_logits, hgrn_norm_g[l], l, state_hgrn, (ya, hg_s))
        yb, *ml_p = _mlstm(proj_ab, 4, gates, 0, Bp, Lp, l, None, (y_init, *ml_p))
        yb, *ml_s = _mlstm(proj_ab, 4, gates, Tp, Bs, Ls, l, (state_mlstm_C, mlstm_n_in, mlstm_m_in),
                           (yb, *ml_s))
        yc, *gd_p = _gdn(proj_c, 0, gates, 0, Bp, Lp, conv_w_t, gdn_A_log[l], gdn_dt_bias[l],
                         gdn_norm_g[l], l, None, (y_init, *gd_p))
        yc, *gd_s = _gdn(proj_c, 0, gates, Tp, Bs, Ls, conv_w_t, gdn_A_log[l], gdn_dt_bias[l],
                         gdn_norm_g[l], l, (state_gdn, state_conv), (yc, *gd_s))

        merged = _merge((ya, yb, yc), gate, w_branch, l)
        mix = _matmul(merged, w_out, (None, D, tn), lambda n: (l, 0, n),
                      zero_bias, (1, tn), lambda n: (0, n), D, F32, tm=tm, tn=tn, name="out_proj")

        xp, xs, h2, rinfo = pl.pallas_call(
            functools.partial(_post_route_kernel, npb=tr["npb"], spb=tr["spb"], alpha=alpha),
            grid=tr["grid"],
            in_specs=[tr["xp"], tr["xs"], tr["tok"], tr["mp"], tr["ms"], vec_spec, vec_spec,
                      pl.BlockSpec((D, LANES), lambda i: (0, 0)), pl.BlockSpec((1, LANES), lambda i: (0, 0))],
            out_specs=[tr["xp"], tr["xs"], tr["tok"], tr["lanes"]],
            out_shape=[x_shapes[0], x_shapes[1], jax.ShapeDtypeStruct((T, D), F32),
                       jax.ShapeDtypeStruct((T, LANES), F32)],
            compiler_params=_params("arbitrary"),
            name="post_route",
        )(xp, xs, mix, mod, mod, vec(ln_mix_g, l), vec(ln_mix_b, l), wr, br)

        ffn = _moe(h2, rinfo, w_exp_gate, w_exp_up, w_exp_down, l)
        xp, xs = pl.pallas_call(
            functools.partial(_post_final_kernel, npb=tr["npb"], spb=tr["spb"], alpha=alpha),
            grid=tr["grid"],
            in_specs=[tr["xp"], tr["xs"], tr["tok"], tr["tok_hi"], tr["mp"], tr["ms"], vec_spec, vec_spec],
            out_specs=[tr["xp"], tr["xs"]],
            out_shape=x_shapes,
            compiler_params=_params("arbitrary"),
            name="post_final",
        )(xp, xs, ffn, ffn, mod, mod, vec(ln_ffn_g, l), vec(ln_ffn_b, l))

    as_vec = lambda n, B: n.reshape(depth, B, N_HEADS, HEAD_DIM)
    as_scalar = lambda m: m[:, :, :, 0, 0]
    return (xp.reshape(Bp, Lp, D), xs.reshape(Bs, Ls, D), hg_p, hg_s, ml_p[0], ml_s[0],
            as_vec(ml_p[1], Bp), as_vec(ml_s[1], Bs), as_scalar(ml_p[2]), as_scalar(ml_s[2]),
            gd_p[0], gd_s[0], gd_p[1], gd_s[1])
```

```python
import functools
import math

import numpy as np
import jax
import jax.numpy as jnp
from jax import lax
from jax.experimental import pallas as pl
from jax.experimental.pallas import tpu as pltpu

F32 = jnp.float32
BF16 = jnp.bfloat16

HEAD_DIM = 128
N_HEADS = 8
WIDTH = N_HEADS * HEAD_DIM
CHUNK = 64
CONV_W = 4
N_EXPERTS = 16
N_GROUPS = 4
EXPERTS_PER_GROUP = N_EXPERTS // N_GROUPS
TOP_K = 2
LN_EPS = 1e-5
NORM_EPS = 1e-6
LB_TINY = 1e-20
NEG_BIG = -1e30
LANES = 128
VMEM_LIMIT = 56 * 1024 * 1024
TOK_BLOCK = 256
ROUTE_BLOCK = 128
MM_TM = 1024
MM_TN = 512
SHORT_SEQS_PER_STEP = 8
MOE_TM = 256
MOE_SLICES = 8
HEADS = tuple(slice(h * HEAD_DIM, (h + 1) * HEAD_DIM) for h in range(N_HEADS))


def _params(*sem):
    return pltpu.CompilerParams(dimension_semantics=sem, vmem_limit_bytes=VMEM_LIMIT)


def _dot(a, b):
    return jnp.dot(a.astype(BF16), b.astype(BF16), preferred_element_type=F32)


def _dot_nt(a, b):
    return lax.dot_general(a.astype(BF16), b.astype(BF16), (((1,), (1,)), ((), ())),
                           preferred_element_type=F32)


def _dot_tn(a, b):
    return lax.dot_general(a.astype(BF16), b.astype(BF16), (((0,), (0,)), ((), ())),
                           preferred_element_type=F32)


def _split2(x):
    x1 = x.astype(BF16)
    return x1, (x - x1.astype(F32)).astype(BF16)


def _dot_exact_l(a_bf, x):
    x1, x2 = _split2(x)
    d = lambda p: jnp.dot(a_bf, p, preferred_element_type=F32)
    return d(x1) + d(x2)


def _dot_exact_r(x, b_bf):
    x1, x2 = _split2(x)
    d = lambda p: jnp.dot(p, b_bf, preferred_element_type=F32)
    return d(x1) + d(x2)


def _dot3(a, b):
    a1, a2 = _split2(a)
    b1, b2 = _split2(b)
    d = lambda p, q: jnp.dot(p, q, preferred_element_type=F32)
    return d(a1, b1) + (d(a1, b2) + d(a2, b1))


def _sigmoid(x):
    return jax.nn.sigmoid(x)


def _silu(x):
    return x * jax.nn.sigmoid(x)


def _log_sigmoid(x):
    return jnp.minimum(x, 0.0) - jnp.log1p(jnp.exp(-jnp.abs(x)))


def _softplus(x):
    return jnp.maximum(x, 0.0) + jnp.log1p(jnp.exp(-jnp.abs(x)))


def _logaddexp(a, b):
    return jnp.maximum(a, b) + jnp.log1p(jnp.exp(-jnp.abs(a - b)))


def _tri_masks(cn):
    r = lax.broadcasted_iota(jnp.int32, (cn, cn), 0)
    c = lax.broadcasted_iota(jnp.int32, (cn, cn), 1)
    return r >= c, r > c, r == c


def _rms_gate(o, ng, gate):
    return o * lax.rsqrt(jnp.mean(o * o, axis=-1, keepdims=True) + NORM_EPS) * ng * gate


def _mm_kernel(x_ref, w_ref, b_ref, o_ref, wbf_ref, *, act, w_transposed):
    @pl.when(pl.program_id(1) == 0)
    def _():
        if w_transposed:
            tn = w_ref.shape[0]
            step = min(tn, 256)
            for c in range(0, tn, step):
                wbf_ref[:, c:c + step] = w_ref[c:c + step, :].T.astype(BF16)
        else:
            wbf_ref[...] = w_ref[...].astype(BF16)

    acc = jnp.dot(x_ref[...], wbf_ref[...], preferred_element_type=F32) + b_ref[...]
    if act == "sigmoid":
        acc = _sigmoid(acc)
    o_ref[...] = acc.astype(o_ref.dtype)


def _matmul(x, w, w_block, w_index, b, b_block, b_index, n_out, out_dtype, *, tm, tn, name, act=None,
            w_transposed=False):
    M, K = x.shape
    tm = min(tm, M)
    assert M % tm == 0 and n_out % tn == 0, (M, tm, n_out, tn)
    return pl.pallas_call(
        functools.partial(_mm_kernel, act=act, w_transposed=w_transposed),
        grid=(n_out // tn, M // tm),
        in_specs=[pl.BlockSpec((tm, K), lambda n, m: (m, 0)),
                  pl.BlockSpec(w_block, lambda n, m: w_index(n)),
                  pl.BlockSpec(b_block, lambda n, m: b_index(n))],
        out_specs=pl.BlockSpec((tm, tn), lambda n, m: (m, n)),
        out_shape=jax.ShapeDtypeStruct((M, n_out), out_dtype),
        scratch_shapes=[pltpu.VMEM((K, tn), BF16)],
        compiler_params=_params("arbitrary", "arbitrary"),
        name=name,
    )(x, w, b)


def _with_mod(npb, mp_ref, ms_ref, body):
    i = pl.program_id(0)

    @pl.when(i < npb)
    def _():
        body(mp_ref[...], 0)

    @pl.when(i >= npb)
    def _():
        body(ms_ref[...], 1)


def _group_specs(npb, tb, D):
    xp = pl.BlockSpec((tb, D), lambda i: (jnp.minimum(i, npb - 1), 0))
    xs = pl.BlockSpec((tb, D), lambda i: (jnp.maximum(i - npb, 0), 0))
    return xp, xs


def _mod_specs(npb, n_sample_seq, blocks_per_prompt_seq, n_prompt_seq, spb, D):
    mp = pl.BlockSpec((1, 6, D), lambda i: (n_sample_seq + jnp.minimum(i // blocks_per_prompt_seq,
                                                                        n_prompt_seq - 1), 0, 0))
    ms = pl.BlockSpec((spb, 6, D), lambda i: (jnp.maximum(i - npb, 0), 0, 0))
    return mp, ms


def _modulate_kernel(xp_ref, xs_ref, mp_ref, ms_ref, o_ref, *, npb, spb, sh_idx, sc_idx):
    tb, D = xp_ref.shape

    def body(m, group):
        x = (xp_ref, xs_ref)[group][...].reshape(spb, tb // spb, D)
        h = x * (1.0 + m[:, sc_idx:sc_idx + 1, :]) + m[:, sh_idx:sh_idx + 1, :]
        o_ref[...] = h.reshape(tb, D).astype(o_ref.dtype)

    _with_mod(npb, mp_ref, ms_ref, body)


def _layer_norm(z, g, b):
    mu = jnp.mean(z, axis=-1, keepdims=True)
    zc = z - mu
    var = jnp.mean(zc * zc, axis=-1, keepdims=True)
    return zc * lax.rsqrt(var + LN_EPS) * g + b


def _route(logits):
    lane = lax.broadcasted_iota(jnp.int32, logits.shape, 1)
    valid = lane < N_EXPERTS
    lg = jnp.where(valid, logits, NEG_BIG)
    ex = jnp.where(valid, jnp.exp(lg - jnp.max(lg, axis=-1, keepdims=True)), 0.0)
    probs = ex / jnp.sum(ex, axis=-1, keepdims=True)
    big = LANES
    best = None
    for g in range(N_GROUPS):
        in_g = (lane >= g * EXPERTS_PER_GROUP) & (lane < (g + 1) * EXPERTS_PER_GROUP)
        m1 = jnp.max(jnp.where(in_g, probs, -1.0), axis=-1, keepdims=True)
        i1 = jnp.min(jnp.where(in_g & (probs == m1), lane, big), axis=-1, keepdims=True)
        rest = in_g & (lane != i1)
        m2 = jnp.max(jnp.where(rest, probs, -1.0), axis=-1, keepdims=True)
        i2 = jnp.min(jnp.where(rest & (probs == m2), lane, big), axis=-1, keepdims=True)
        cand = (m1 + m2, i1, i2, m1, m2)
        if best is None:
            best = cand
        else:
            better = cand[0] > best[0]
            best = tuple(jnp.where(better, c, o) for c, o in zip(cand, best))
    _, i1, i2, m1, m2 = best
    tot = m1 + m2
    return (jnp.where(lane == 0, i1.astype(F32), 0.0) + jnp.where(lane == 1, i2.astype(F32), 0.0)
            + jnp.where(lane == 2, m1 / tot, 0.0) + jnp.where(lane == 3, m2 / tot, 0.0))


def _post_route_kernel(xp_ref, xs_ref, y_ref, mp_ref, ms_ref, g_ref, b_ref, wr_ref, br_ref,
                       xop_ref, xos_ref, h_ref, r_ref, *, npb, spb, alpha):
    tb, D = xp_ref.shape

    def body(m, group):
        x = (xp_ref, xs_ref)[group][...].reshape(spb, tb // spb, D)
        y = y_ref[...].reshape(spb, tb // spb, D)
        xn = _layer_norm(alpha * x + m[:, 2:3, :] * y, g_ref[...], b_ref[...])
        (xop_ref, xos_ref)[group][...] = xn.reshape(tb, D)
        h2 = (xn * (1.0 + m[:, 4:5, :]) + m[:, 3:4, :]).reshape(tb, D)
        h_ref[...] = h2
        r_ref[...] = _route(_dot3(h2, wr_ref[...]) + br_ref[...])

    _with_mod(npb, mp_ref, ms_ref, body)


def _post_final_kernel(xp_ref, xs_ref, y0_ref, y1_ref, mp_ref, ms_ref, g_ref, b_ref,
                       xop_ref, xos_ref, *, npb, spb, alpha):
    tb, D = xp_ref.shape

    def body(m, group):
        x = (xp_ref, xs_ref)[group][...].reshape(spb, tb // spb, D)
        y = (y0_ref[...] + y1_ref[...]).reshape(spb, tb // spb, D)
        xn = _layer_norm(alpha * x + m[:, 5:6, :] * y, g_ref[...], b_ref[...])
        (xop_ref, xos_ref)[group][...] = xn.reshape(tb, D)

    _with_mod(npb, mp_ref, ms_ref, body)


def _seq_blocking(L):
    cn = math.gcd(L, CHUNK)
    lb = min(L, 512)
    assert L % lb == 0 and lb % cn == 0
    return cn, lb, L // lb


def _run_chunks(chunk, nchunk):
    if nchunk == 1:
        chunk(0, 0)
    else:
        lax.fori_loop(0, nchunk, chunk, 0)


def _hgrn_tables(cn):
    nlev = int(math.log2(cn))
    assert 1 << nlev == cn
    r = np.arange(cn)
    mats = [np.tril(np.ones((cn, cn)))]
    masks = [np.eye(cn)]
    j = r[None, :]
    for lev in range(nlev):
        s = cn >> (lev + 1)
        blk = 2 * s
        pos = r % blk
        ref = r - pos + s - 1
        second = pos >= s
        m_second = (j > ref[:, None]) & (j <= r[:, None])
        m_first = (j > r[:, None]) & (j <= ref[:, None])
        mats.append(np.where(second[:, None], m_second, m_first))
        same = (r[:, None] // blk) == (r[None, :] // blk)
        masks.append(same & second[:, None] & (~second)[None, :])
    return (np.concatenate(mats, 0).astype(np.float32), np.stack(masks).astype(np.float32))


def _hgrn_kernel(*refs, layer, cn, nchunk, nseq, has_state, n_alias):
    refs = list(refs)
    lbl_ref, ng_ref, mall_ref, masks_ref, q_ref, f_ref, i_ref, g_ref = refs[:8]
    pos = 8
    s0_ref = None
    if has_state:
        s0_ref = refs[pos]
        pos += 1
    pos += n_alias
    y_ref, s_ref, st_ref = refs[pos:pos + 3]
    j = pl.program_id(1)
    nlev = masks_ref.shape[0] - 1

    seqs = range(nseq)
    units = [(sq, h) for sq in seqs for h in range(N_HEADS)]
    seq_rows = q_ref.shape[0] // nseq

    @pl.when(j == 0)
    def _():
        for n, (sq, h) in enumerate(units):
            if has_state:
                st_ref[n] = s0_ref[sq, h].T
            else:
                st_ref[n] = jnp.zeros((HEAD_DIM, HEAD_DIM), F32)

    logits = lbl_ref[...]
    e = jnp.exp(logits - jnp.max(logits, axis=0, keepdims=True))
    p = e / jnp.sum(e, axis=0, keepdims=True)
    lb = jnp.zeros((1, WIDTH), F32)
    for l in range(1, layer + 1):
        lb = lb + p[l:l + 1, :]
    log_lb = jnp.log(jnp.maximum(lb, LB_TINY))
    log1m_lb = jnp.log1p(-lb)
    one_m_lb = 1.0 - lb
    ng = ng_ref[...]
    mall = mall_ref[...]

    def chunk(c, carry):
        r0 = pl.multiple_of(c * cn, cn)
        rows = [pl.ds(r0 + sq * seq_rows, cn) for sq in seqs]
        q = [_silu(q_ref[rw, :]) for rw in rows]
        fr = [f_ref[rw, :] for rw in rows]
        v = [i_ref[rw, :].astype(BF16) for rw in rows]
        log_f = [_logaddexp(log_lb, log1m_lb + _log_sigmoid(x)) for x in fr]
        k = [one_m_lb * _sigmoid(-x) for x in fr]
        dall = [_dot_exact_l(mall, x) for x in log_f]
        b = [x[:cn] for x in dall]
        qb = [x.astype(BF16) for x in q]
        kb = [x.astype(BF16) for x in k]
        att = [masks_ref[0] * _dot_nt(qb[sq][:, HEADS[h]], kb[sq][:, HEADS[h]]) for sq, h in units]
        for lev in range(1, nlev + 1):
            ex = [jnp.exp(x[lev * cn:(lev + 1) * cn]) for x in dall]
            qe = [(x * e).astype(BF16) for x, e in zip(q, ex)]
            ke = [(x * e).astype(BF16) for x, e in zip(k, ex)]
            att = [a + masks_ref[lev] * _dot_nt(qe[sq][:, HEADS[h]], ke[sq][:, HEADS[h]])
                   for a, (sq, h) in zip(att, units)]
        st = [st_ref[n] for n in range(len(units))]
        qd = [(x * jnp.exp(bb)).astype(BF16) for x, bb in zip(q, b)]
        o = [_dot(a, v[sq][:, HEADS[h]]) + _dot_nt(qd[sq][:, HEADS[h]], s)
             for a, (sq, h), s in zip(att, units, st)]
        b_last = [bb[cn - 1:cn] for bb in b]
        k_dec = [(x * jnp.exp(bl - bb)).astype(BF16) for x, bl, bb in zip(k, b_last, b)]
        dec = [jnp.exp(bl) for bl in b_last]
        for n, (sq, h) in enumerate(units):
            cs = HEADS[h]
            st_ref[n] = st[n] * dec[sq][:, cs] + _dot_tn(v[sq][:, cs], k_dec[sq][:, cs])
        gate = [_sigmoid(g_ref[rw, :]) for rw in rows]
        for n, (sq, h) in enumerate(units):
            cs = HEADS[h]
            y_ref[rows[sq], cs] = _rms_gate(o[n], ng[:, cs], gate[sq][:, cs])
        return carry

    _run_chunks(chunk, nchunk)

    @pl.when(j == pl.num_programs(1) - 1)
    def _():
        for n, (sq, h) in enumerate(units):
            s_ref[sq, h] = st_ref[n].T


def _mixer_specs(T_total, row0, B, L):
    cn, LB, ncb = _seq_blocking(L)
    nseq = math.gcd(B, SHORT_SEQS_PER_STEP) if ncb == 1 and LB <= CHUNK else 1
    rows = nseq * LB
    assert row0 % rows == 0
    rb0 = row0 // rows
    col = lambda cidx: pl.BlockSpec((rows, WIDTH), lambda b, j: (rb0 + b * ncb + j, cidx))
    yspec = pl.BlockSpec((rows, WIDTH), lambda b, j: (rb0 + b * ncb + j, 0))
    gspec = pl.BlockSpec((rows, LANES), lambda b, j: (rb0 + b * ncb + j, 0))
    gtspec = pl.BlockSpec((rows // cn, 4 * N_HEADS, cn), lambda b, j: (b * ncb + j, 0, 0))
    return cn, LB, ncb, nseq, col, yspec, gspec, gtspec


def _state_spec(layer, trailing, nseq):
    return pl.BlockSpec((None, nseq) + trailing, lambda b, j: (layer, b) + (0,) * len(trailing))


_STATE_DIMS = (N_HEADS, HEAD_DIM, HEAD_DIM)
_VEC_DIMS = (N_HEADS, 1, HEAD_DIM)


def _full(a):
    return pl.BlockSpec(a.shape, lambda b, j: (0,) * a.ndim)


def _alias_outputs(args, in_specs, prevs):
    aliases = {}
    for k, prev in enumerate(prevs):
        args.append(prev)
        in_specs.append(pl.BlockSpec(memory_space=pl.ANY))
        aliases[len(args) - 1] = k
    return aliases


def _shapes(prevs):
    return [jax.ShapeDtypeStruct(p.shape, p.dtype) for p in prevs]


def _hgrn(proj, a_col0, row0, B, L, lb_logits, norm_g, layer, state, prevs):
    cn, LB, ncb, nseq, col, yspec, _, _ = _mixer_specs(proj.shape[0], row0, B, L)
    mall, masks = _hgrn_tables(cn)
    mall = jnp.asarray(mall, BF16)
    masks = jnp.asarray(masks)
    ng = jnp.tile(norm_g.reshape(1, HEAD_DIM).astype(F32), (1, N_HEADS))
    sspec = _state_spec(layer, _STATE_DIMS, nseq)
    args = [lb_logits, ng, mall, masks, proj, proj, proj, proj]
    in_specs = [_full(lb_logits), _full(ng), _full(mall), _full(masks)] + [col(a_col0 + t) for t in range(4)]
    if state is not None:
        args.append(state)
        in_specs.append(sspec)
    aliases = _alias_outputs(args, in_specs, prevs)
    return pl.pallas_call(
        functools.partial(_hgrn_kernel, layer=layer, cn=cn, nchunk=LB // cn, nseq=nseq,
                          has_state=state is not None, n_alias=len(prevs)),
        grid=(B // nseq, ncb),
        in_specs=in_specs,
        out_specs=[yspec, sspec],
        out_shape=_shapes(prevs),
        scratch_shapes=[pltpu.VMEM((nseq * N_HEADS, HEAD_DIM, HEAD_DIM), F32)],
        input_output_aliases=aliases,
        compiler_params=_params("arbitrary", "arbitrary"),
        name="hgrn2",
    )(*args)


def _head_cols(mat, base):
    return [mat[:, base + h:base + h + 1] for h in range(N_HEADS)]


def _head_rows(mat, base):
    return [mat[base + h:base + h + 1, :] for h in range(N_HEADS)]


def _mlstm_kernel(*refs, cn, nchunk, nseq, has_state, n_alias):
    refs = list(refs)
    gt_ref, gtt_ref, q_ref, k_ref, v_ref, o_ref = refs[:6]
    pos = 6
    if has_state:
        c0_ref, n0_ref, m0_ref = refs[pos:pos + 3]
        pos += 3
    pos += n_alias
    y_ref, c_ref, n_ref, m_ref = refs[pos:pos + 4]
    j = pl.program_id(1)

    @pl.when(j == 0)
    def _():
        if has_state:
            c_ref[...] = c0_ref[...]
            n_ref[...] = n0_ref[...]
            m_ref[...] = m0_ref[...]
        else:
            c_ref[...] = jnp.zeros(c_ref.shape, F32)
            n_ref[...] = jnp.zeros(n_ref.shape, F32)
            m_ref[...] = jnp.zeros(m_ref.shape, F32)

    incl, _, _ = _tri_masks(cn)
    tril = incl.astype(BF16)
    triu = jnp.logical_not(_tri_masks(cn)[1]).astype(BF16)

    seqs = range(nseq)
    units = [(sq, h) for sq in seqs for h in range(N_HEADS)]
    lb = q_ref.shape[0] // nseq

    def chunk(c, carry):
        r0 = pl.multiple_of(c * cn, cn)
        rows = [pl.ds(r0 + sq * lb, cn) for sq in seqs]
        gates = [gt_ref[rw, :] for rw in rows]
        gates_r = [gtt_ref[sq * nchunk + c] for sq in seqs]
        f_c = [_dot_exact_l(tril, _log_sigmoid(g)) for g in gates]
        f_r = [_dot_exact_r(_log_sigmoid(g), triu) for g in gates_r]
        Fc = [x for sq in seqs for x in _head_cols(f_c[sq], N_HEADS)]
        ic = [x for sq in seqs for x in _head_cols(gates[sq], 0)]
        Fr = [x for sq in seqs for x in _head_rows(f_r[sq], N_HEADS)]
        ir = [x for sq in seqs for x in _head_rows(gates_r[sq], 0)]
        q = [q_ref[rows[sq], HEADS[h]].astype(BF16) for sq, h in units]
        k = [k_ref[rows[sq], HEADS[h]] * (HEAD_DIM ** -0.5) for sq, h in units]
        kb = [x.astype(BF16) for x in k]
        v = [v_ref[rows[sq], HEADS[h]].astype(BF16) for sq, h in units]
        C = [c_ref[sq, h] for sq, h in units]
        n = [n_ref[sq, h] for sq, h in units]
        m = [m_ref[sq, h][:, 0:1] for sq, h in units]
        qk = [_dot_nt(a, b) for a, b in zip(q, kb)]
        qc = [_dot(a, b) for a, b in zip(q, C)]
        raw = [a - b + d for a, b, d in zip(Fc, Fr, ir)]
        inter = [a + b for a, b in zip(Fc, m)]
        m_t = [jnp.maximum(a, jnp.max(jnp.where(incl, r, NEG_BIG), axis=-1, keepdims=True))
               for a, r in zip(inter, raw)]
        w_inter = [jnp.exp(a - b) for a, b in zip(inter, m_t)]
        s = [jnp.where(incl, a * jnp.exp(jnp.where(incl, r - mt, 0.0)), 0.0)
             for a, r, mt in zip(qk, raw, m_t)]
        num = [_dot(a, b) + w * d for a, b, w, d in zip(s, v, w_inter, qc)]
        den = [jnp.sum(a, axis=-1, keepdims=True)
               + w * jnp.sum(q_ref[rows[sq], HEADS[h]] * nn, axis=-1, keepdims=True)
               for a, w, (sq, h), nn in zip(s, w_inter, units, n)]
        hh = [a / jnp.maximum(jnp.abs(d), jnp.exp(-mt)) for a, d, mt in zip(num, den, m_t)]
        for u, (sq, h) in enumerate(units):
            m_new = m_t[u][cn - 1:cn]
            w_c = jnp.exp(inter[u][cn - 1:cn] - m_new)
            w_s = jnp.exp(Fc[u][cn - 1:cn] - Fc[u] + ic[u] - m_new)
            ks = k[u] * w_s
            c_ref[sq, h] = w_c * C[u] + _dot_tn(ks, v[u])
            n_ref[sq, h] = w_c * n[u] + jnp.sum(ks, axis=0, keepdims=True)
            m_ref[sq, h] = jnp.broadcast_to(m_new, (1, HEAD_DIM))
            y_ref[rows[sq], HEADS[h]] = _sigmoid(o_ref[rows[sq], HEADS[h]]) * hh[u]
        return carry

    _run_chunks(chunk, nchunk)


def _gates_t(gates, row0, B, L, cn):
    g = gates[row0:row0 + B * L, :4 * N_HEADS]
    return g.reshape(B * L // cn, cn, 4 * N_HEADS).transpose(0, 2, 1)


def _mlstm(proj, b_col0, gates, row0, B, L, layer, state, prevs):
    cn, LB, ncb, nseq, col, yspec, gspec, gtspec = _mixer_specs(proj.shape[0], row0, B, L)
    cspec = _state_spec(layer, _STATE_DIMS, nseq)
    nspec = _state_spec(layer, _VEC_DIMS, nseq)
    gt = _gates_t(gates, row0, B, L, cn)
    args = [gates, gt, proj, proj, proj, proj]
    in_specs = [gspec, gtspec] + [col(b_col0 + t) for t in range(4)]
    if state is not None:
        args += list(state)
        in_specs += [cspec, nspec, nspec]
    aliases = _alias_outputs(args, in_specs, prevs)
    return pl.pallas_call(
        functools.partial(_mlstm_kernel, cn=cn, nchunk=LB // cn, nseq=nseq, has_state=state is not None,
                          n_alias=len(prevs)),
        grid=(B // nseq, ncb),
        in_specs=in_specs,
        out_specs=[yspec, cspec, nspec, nspec],
        out_shape=_shapes(prevs),
        input_output_aliases=aliases,
        compiler_params=_params("arbitrary", "arbitrary"),
        name="mlstm",
    )(*args)


_CONV_PAD = 8


def _gdn_kernel(*refs, cn, nchunk, nseq, has_state, n_alias):
    refs = list(refs)
    (gt_ref, gtt_ref, cw_ref, al_ref, dtb_ref, alc_ref, dtbc_ref, ng_ref,
     xq_ref, xk_ref, xv_ref, g_ref) = refs[:12]
    pos = 12
    if has_state:
        s0_ref, b0_ref = refs[pos:pos + 2]
        pos += 2
    pos += n_alias
    y_ref, s_ref, buf_ref, xp_ref, cv_ref = refs[pos:pos + 5]
    j = pl.program_id(1)
    lb = xq_ref.shape[0] // nseq
    hist = CONV_W - 1
    h0 = _CONV_PAD - hist
    seqs = range(nseq)
    units = [(sq, h) for sq in seqs for h in range(N_HEADS)]

    @pl.when(j == 0)
    def _():
        if has_state:
            s_ref[...] = s0_ref[...]
        else:
            s_ref[...] = jnp.zeros(s_ref.shape, F32)

    for sq in seqs:
        blk = slice(sq * lb, (sq + 1) * lb)

        def start_history(sq=sq):
            xp_ref[h0:_CONV_PAD, :] = b0_ref[sq] if has_state else jnp.zeros((hist, 3 * WIDTH), F32)

        if nseq > 1:
            start_history()
        else:
            pl.when(j == 0)(start_history)
        xp_ref[_CONV_PAD:_CONV_PAD + lb, 0:WIDTH] = xq_ref[blk, :]
        xp_ref[_CONV_PAD:_CONV_PAD + lb, WIDTH:2 * WIDTH] = xk_ref[blk, :]
        xp_ref[_CONV_PAD:_CONV_PAD + lb, 2 * WIDTH:3 * WIDTH] = xv_ref[blk, :]
        acc = xp_ref[h0:h0 + lb, :] * cw_ref[0:1, :]
        for t in range(1, CONV_W):
            acc = acc + xp_ref[h0 + t:h0 + t + lb, :] * cw_ref[t:t + 1, :]
        cv_ref[blk, :] = _silu(acc)
        new_hist = xp_ref[_CONV_PAD + lb - hist:_CONV_PAD + lb, :]
        xp_ref[h0:_CONV_PAD, :] = new_hist
        buf_ref[sq] = new_hist

    incl, strict, eye = _tri_masks(cn)
    tril = incl.astype(BF16)
    triu = jnp.logical_not(strict).astype(BF16)
    eye_f = eye.astype(F32)
    lane = lax.broadcasted_iota(jnp.int32, (1, LANES), 1)
    is_a = (lane >= 2 * N_HEADS) & (lane < 3 * N_HEADS)
    sub = lax.broadcasted_iota(jnp.int32, (4 * N_HEADS, 1), 0)
    is_a_r = (sub >= 2 * N_HEADS) & (sub < 3 * N_HEADS)
    neg_rate = -jnp.exp(al_ref[...])
    dtb = dtb_ref[...]
    neg_rate_r = -jnp.exp(alc_ref[...])
    dtb_r = dtbc_ref[...]
    ng = ng_ref[...]
    nsq = int(math.log2(cn))

    def chunk(c, carry):
        r0 = pl.multiple_of(c * cn, cn)
        rows = [pl.ds(r0 + sq * lb, cn) for sq in seqs]
        gam_c, gam_r, beta = [], [], []
        for sq in seqs:
            gates = gt_ref[rows[sq], :]
            gates_r = gtt_ref[sq * nchunk + c]
            g_c = jnp.where(is_a, neg_rate * _softplus(gates + dtb), 0.0)
            g_r = jnp.where(is_a_r, neg_rate_r * _softplus(gates_r + dtb_r), 0.0)
            gam_c += _head_cols(_dot_exact_l(tril, g_c), 2 * N_HEADS)
            gam_r += _head_rows(_dot_exact_r(g_r, triu), 2 * N_HEADS)
            beta += _head_cols(_sigmoid(gates), 3 * N_HEADS)
        q, k, v = [], [], []
        for sq, h in units:
            cs = HEADS[h]
            qr = cv_ref[rows[sq], cs]
            kr = cv_ref[rows[sq], slice(WIDTH + cs.start, WIDTH + cs.stop)]
            v.append(cv_ref[rows[sq], slice(2 * WIDTH + cs.start, 2 * WIDTH + cs.stop)])
            q.append(qr * (lax.rsqrt(jnp.sum(qr * qr, axis=-1, keepdims=True) + NORM_EPS) * (HEAD_DIM ** -0.5)))
            k.append(kr * lax.rsqrt(jnp.sum(kr * kr, axis=-1, keepdims=True) + NORM_EPS))
        kb = [x.astype(BF16) for x in k]
        S = [s_ref[sq, h] for sq, h in units]
        kk = [_dot_nt(a, a) for a in kb]
        qk = [_dot_nt(a, b) for a, b in zip(q, kb)]
        dec = [jnp.exp(jnp.where(incl, a - b, 0.0)) for a, b in zip(gam_c, gam_r)]
        pw = [jnp.where(strict, -(bt * a * d), 0.0) for bt, a, d in zip(beta, kk, dec)]
        t_inv = [eye_f + a for a in pw]
        for _ in range(nsq - 1):
            pw = [_dot3(a, a) for a in pw]
            t_inv = [t + _dot3(t, a) for t, a in zip(t_inv, pw)]
        eg = [jnp.exp(a) for a in gam_c]
        sol_v = [_dot3(t, bt * x) for t, bt, x in zip(t_inv, beta, v)]
        sol_k = [_dot3(t, (bt * e) * x) for t, bt, e, x in zip(t_inv, beta, eg, k)]
        u = [a - _dot(b, s) for a, b, s in zip(sol_v, sol_k, S)]
        o = [_dot(jnp.where(incl, a * d, 0.0), uu) + _dot(qq * e, s)
             for a, d, uu, qq, e, s in zip(qk, dec, u, q, eg, S)]
        for n, (sq, h) in enumerate(units):
            g_last = gam_c[n][cn - 1:cn]
            s_ref[sq, h] = jnp.exp(g_last) * S[n] + _dot_tn(k[n] * jnp.exp(g_last - gam_c[n]), u[n])
            y_ref[rows[sq], HEADS[h]] = _rms_gate(o[n], ng, _silu(g_ref[rows[sq], HEADS[h]]))
        return carry

    _run_chunks(chunk, nchunk)


def _gdn(proj, c_col0, gates, row0, B, L, conv_w_t, a_log, dt_bias, norm_g, layer, state, prevs):
    cn, LB, ncb, nseq, col, yspec, gspec, gtspec = _mixer_specs(proj.shape[0], row0, B, L)
    hist = CONV_W - 1
    sspec = _state_spec(layer, _STATE_DIMS, nseq)
    bspec = _state_spec(layer, (hist, 3 * WIDTH), nseq)
    a0, a1 = 2 * N_HEADS, 3 * N_HEADS
    al = jnp.zeros((1, LANES), F32).at[0, a0:a1].set(a_log.astype(F32))
    db = jnp.zeros((1, LANES), F32).at[0, a0:a1].set(dt_bias.astype(F32))
    alc = al[0, :4 * N_HEADS].reshape(4 * N_HEADS, 1)
    dbc = db[0, :4 * N_HEADS].reshape(4 * N_HEADS, 1)
    ng = norm_g.reshape(1, HEAD_DIM).astype(F32)
    gt = _gates_t(gates, row0, B, L, cn)
    args = [gates, gt, conv_w_t, al, db, alc, dbc, ng, proj, proj, proj, proj]
    in_specs = [gspec, gtspec, _full(conv_w_t), _full(al), _full(db), _full(alc), _full(dbc), _full(ng)]
    in_specs += [col(c_col0 + t) for t in range(4)]
    if state is not None:
        args += [state[0], state[1]]
        in_specs += [sspec, bspec]
    aliases = _alias_outputs(args, in_specs, prevs)
    return pl.pallas_call(
        functools.partial(_gdn_kernel, cn=cn, nchunk=LB // cn, nseq=nseq, has_state=state is not None,
                          n_alias=len(prevs)),
        grid=(B // nseq, ncb),
        in_specs=in_specs,
        out_specs=[yspec, sspec, bspec],
        out_shape=_shapes(prevs),
        scratch_shapes=[pltpu.VMEM((_CONV_PAD + LB, 3 * WIDTH), F32),
                        pltpu.VMEM((nseq * LB, 3 * WIDTH), F32)],
        input_output_aliases=aliases,
        compiler_params=_params("arbitrary", "arbitrary"),
        name="gdn",
    )(*args)


def _merge_kernel(ya_ref, yb_ref, yc_ref, ga_ref, gb_ref, gc_ref, wb_ref, o_ref, wbf_ref):
    @pl.when(pl.program_id(1) == 0)
    def _():
        wbf_ref[...] = wb_ref[...].astype(BF16)

    acc = None
    for jdx, (y_ref, g_ref) in enumerate(((ya_ref, ga_ref), (yb_ref, gb_ref), (yc_ref, gc_ref))):
        br = jnp.dot(y_ref[...].astype(BF16), wbf_ref[jdx], preferred_element_type=F32)
        term = g_ref[...].astype(F32) * br
        acc = term if acc is None else acc + term
    o_ref[...] = acc.astype(o_ref.dtype)


def _merge(ys, gate, w_branch, layer, tm=256, tn=1024):
    T, W = ys[0].shape
    D = w_branch.shape[-1]
    tn = min(tn, D)
    nb = D // tn
    yspec = pl.BlockSpec((tm, W), lambda n, m: (m, 0))
    gspec = lambda jdx: pl.BlockSpec((tm, tn), lambda n, m: (m, jdx * nb + n))
    return pl.pallas_call(
        _merge_kernel,
        grid=(nb, T // tm),
        in_specs=[yspec, yspec, yspec, gspec(0), gspec(1), gspec(2),
                  pl.BlockSpec((None, 3, W, tn), lambda n, m: (layer, 0, 0, n))],
        out_specs=pl.BlockSpec((tm, tn), lambda n, m: (m, n)),
        out_shape=jax.ShapeDtypeStruct((T, D), BF16),
        scratch_shapes=[pltpu.VMEM((3, W, tn), BF16)],
        compiler_params=_params("arbitrary", "arbitrary"),
        name="merge",
    )(ys[0], ys[1], ys[2], gate, gate, gate, w_branch)


def _moe_kernel(it_tile_ref, it_slice_ref, it_flag_ref, te_ref, nv_ref, tok_ref, dst_ref,
                h_hbm, roww_ref, wg_ref, wu_ref, wd_ref, out_hbm,
                xbuf, xb_ref, acc_ref, wgb_ref, wub_ref, wdb_ref, gsem, ssem, *, tm, n_real):
    i = pl.program_id(0)
    r = it_tile_ref[i]
    s = it_slice_ref[i]
    flag = it_flag_ref[i]
    nv = nv_ref[0]
    dk = wg_ref.shape[0]
    fk = wd_ref.shape[0]
    scatter_rows = min(tm, 128)

    def gather_start(tile):
        base = tile * tm

        def body(i, carry):
            tok = tok_ref[base + i]
            pltpu.make_async_copy(h_hbm.at[pl.ds(tok, 1)], xbuf.at[pl.ds(i, 1)], gsem).start()
            return carry

        lax.fori_loop(0, tm, body, 0, unroll=8)

    def gather_wait():
        pltpu.make_async_copy(h_hbm.at[pl.ds(0, tm)], xbuf, gsem).wait()

    def scatter_wait():
        pltpu.make_async_copy(acc_ref, out_hbm.at[pl.ds(0, tm)], ssem).wait()

    @pl.when(i == 0)
    def _():
        acc_ref[...] = jnp.zeros(acc_ref.shape, F32)
        for half in range(2):
            fill = pltpu.make_async_copy(acc_ref, out_hbm.at[pl.ds(n_real + half * tm, tm)], ssem)
            fill.start()
            fill.wait()
        gather_start(0)

    @pl.when((flag & 1) != 0)
    def _():
        k0 = pl.multiple_of(s * dk, dk)
        wgb_ref[pl.ds(k0, dk), :] = wg_ref[...].astype(BF16)
        wub_ref[pl.ds(k0, dk), :] = wu_ref[...].astype(BF16)
        f0 = pl.multiple_of(s * fk, fk)
        wdb_ref[pl.ds(f0, fk), :] = wd_ref[...].astype(BF16)

    @pl.when((flag & 2) != 0)
    def _():
        gather_wait()
        xb_ref[...] = xbuf[...].astype(BF16)
        nxt = jnp.minimum(r + 1, nv - 1)
        for j in range(tm):
            tok = tok_ref[nxt * tm + j]
            pltpu.make_async_copy(h_hbm.at[pl.ds(tok, 1)], xbuf.at[pl.ds(j, 1)], gsem).start()

        x = xb_ref[...]
        g = jnp.dot(x, wgb_ref[...], preferred_element_type=F32)
        u = jnp.dot(x, wub_ref[...], preferred_element_type=F32)
        he = (_silu(g) * u).astype(BF16)

        @pl.when(r > 0)
        def _():
            scatter_wait()

        for c in range(tm // scatter_rows):
            rows = slice(c * scatter_rows, (c + 1) * scatter_rows)
            y = jnp.dot(he[rows], wdb_ref[...], preferred_element_type=F32)
            acc_ref[rows, :] = y * roww_ref[rows, 0:1]
            for j in range(scatter_rows):
                row = c * scatter_rows + j
                dst = dst_ref[r * tm + row]
                pltpu.make_async_copy(acc_ref.at[pl.ds(row, 1)], out_hbm.at[pl.ds(dst, 1)], ssem).start()

        @pl.when(r == nv - 1)
        def _():
            scatter_wait()
            gather_wait()


def _moe_plan(rinfo, T, tm, ns):
    ids = rinfo[:, 0:TOP_K].astype(jnp.int32)
    wts = rinfo[:, TOP_K:2 * TOP_K]
    n_assign = TOP_K * T
    n_rows = n_assign + N_EXPERTS * tm
    n_tiles = n_rows // tm
    e_flat = ids.reshape(-1)
    onehot = (e_flat[:, None] == jnp.arange(N_EXPERTS, dtype=jnp.int32)[None, :]).astype(jnp.int32)
    rank = jnp.take_along_axis(jnp.cumsum(onehot, axis=0), e_flat[:, None], axis=1)[:, 0] - 1
    counts = jnp.sum(onehot, axis=0)
    padded = ((counts + tm - 1) // tm) * tm
    ends = jnp.cumsum(padded)
    starts = ends - padded
    dest = starts[e_flat] + rank
    upd = jnp.stack([jnp.arange(1, n_assign + 1, dtype=jnp.int32).astype(F32), wts.reshape(-1)], axis=1)
    rows = jnp.zeros((n_rows, 2), F32).at[dest].set(upd)
    row_a = rows[:, 0].astype(jnp.int32) - 1
    row_w = rows[:, 1]
    is_real = row_a >= 0
    tok = jnp.maximum(row_a, 0) // TOP_K
    slot = jnp.maximum(row_a, 0) % TOP_K
    p = jnp.arange(n_rows, dtype=jnp.int32)
    dump = n_assign + ((p // tm) % 2) * tm + (p % tm)
    row_tok = jnp.where(is_real, tok, 0)
    row_dst = jnp.where(is_real, slot * T + tok, dump)
    n_valid = (ends[-1] // tm).astype(jnp.int32)
    tile_start = jnp.arange(n_tiles, dtype=jnp.int32) * tm
    tile_e = jnp.minimum(jnp.searchsorted(ends, tile_start, side="right"), N_EXPERTS - 1).astype(jnp.int32)
    valid_t = tile_start < ends[-1]
    last_e = tile_e[jnp.maximum(n_valid - 1, 0)]
    tile_e = jnp.where(valid_t, tile_e, last_e)

    prev_e = jnp.concatenate([jnp.full((1,), -1, jnp.int32), tile_e[:-1]])
    first = valid_t & (tile_e != prev_e)
    cnt = jnp.where(valid_t, jnp.where(first, ns, 1), 0).astype(jnp.int32)
    item_end = jnp.cumsum(cnt)
    item_start = item_end - cnt
    n_items = n_tiles + N_EXPERTS * (ns - 1)
    it = jnp.arange(n_items, dtype=jnp.int32)
    live = it < item_end[-1]
    it_tile = jnp.minimum(jnp.searchsorted(item_end, it, side="right"), n_tiles - 1).astype(jnp.int32)
    it_tile = jnp.where(live, it_tile, jnp.maximum(n_valid - 1, 0))
    k = it - item_start[it_tile]
    loads = live & first[it_tile]
    it_slice = jnp.where(loads, k, ns - 1).astype(jnp.int32)
    computes = live & (~first[it_tile] | (k == ns - 1))
    it_flag = loads.astype(jnp.int32) + 2 * computes.astype(jnp.int32)
    return (it_tile, it_slice, it_flag, tile_e, n_valid.reshape(1), row_tok, row_dst,
            jnp.broadcast_to(row_w[:, None], (n_rows, LANES)))


def _moe(h2, rinfo, wg, wu, wd, layer, tm=MOE_TM, ns=MOE_SLICES):
    T, D = h2.shape
    FF = wg.shape[-1]
    assert D % ns == 0 and FF % ns == 0
    dk, fk = D // ns, FF // ns
    it_tile, it_slice, it_flag, tile_e, n_valid, row_tok, row_dst, row_w = _moe_plan(rinfo, T, tm, ns)
    w_idx = lambda i, itt, its, itf, te, nv, tok, dst: (layer, te[itt[i]], its[i], 0)
    grid_spec = pltpu.PrefetchScalarGridSpec(
        num_scalar_prefetch=7,
        grid=(it_tile.shape[0],),
        in_specs=[pl.BlockSpec(memory_space=pl.ANY),
                  pl.BlockSpec((tm, LANES), lambda i, itt, its, itf, te, nv, tok, dst: (itt[i], 0)),
                  pl.BlockSpec((None, None, dk, FF), w_idx),
                  pl.BlockSpec((None, None, dk, FF), w_idx),
                  pl.BlockSpec((None, None, fk, D), w_idx)],
        out_specs=pl.BlockSpec(memory_space=pl.ANY),
        scratch_shapes=[pltpu.VMEM((tm, D), F32), pltpu.VMEM((tm, D), BF16), pltpu.VMEM((tm, D), F32),
                        pltpu.VMEM((D, FF), BF16), pltpu.VMEM((D, FF), BF16), pltpu.VMEM((FF, D), BF16),
                        pltpu.SemaphoreType.DMA(()), pltpu.SemaphoreType.DMA(())],
    )
    return pl.pallas_call(
        functools.partial(_moe_kernel, tm=tm, n_real=TOP_K * T),
        grid_spec=grid_spec,
        out_shape=jax.ShapeDtypeStruct((TOP_K * T + 2 * tm, D), F32),
        compiler_params=_params("arbitrary"),
        name="moe_routed",
    )(it_tile, it_slice, it_flag, tile_e, n_valid, row_tok, row_dst, h2, row_w, wg, wu, wd)


def kernel(x_prompt, x_sample, state_hgrn, state_mlstm_C, state_mlstm_n, state_mlstm_m, state_gdn, state_conv, c_prompt, c_sample, w_in, b_in, hgrn_lb_logits, hgrn_norm_g, gdn_conv_w, gdn_A_log, gdn_dt_bias, gdn_norm_g, w_branch, w_merge, b_merge, w_out, ln_mix_g, ln_mix_b, w_ada, b_ada, w_router, b_router, w_exp_gate, w_exp_up, w_exp_down, ln_ffn_g, ln_ffn_b):
    depth = w_in.shape[0]
    Bp, Lp, D = x_prompt.shape
    Bs, Ls, _ = x_sample.shape
    Tp, Ts = Bp * Lp, Bs * Ls
    T = Tp + Ts
    alpha = (2 * depth) ** 0.25

    def tok_specs(tb):
        assert Lp % tb == 0 and tb % Ls == 0 and Ts % tb == 0
        npb, spb = Tp // tb, tb // Ls
        mp, ms = _mod_specs(npb, Bs, Lp // tb, Bp, spb, D)
        xp_s, xs_s = _group_specs(npb, tb, D)
        return dict(npb=npb, spb=spb, mp=mp, ms=ms, xp=xp_s, xs=xs_s, grid=(T // tb,),
                    tok=pl.BlockSpec((tb, D), lambda i: (i, 0)),
                    tok_hi=pl.BlockSpec((tb, D), lambda i: (T // tb + i, 0)),
                    lanes=pl.BlockSpec((tb, LANES), lambda i: (i, 0)))

    ts = tok_specs(TOK_BLOCK)
    tr = tok_specs(ROUTE_BLOCK)
    n_ab = 8 * WIDTH
    g0, c0 = n_ab, n_ab + 2 * N_HEADS
    c1 = c0 + 4 * WIDTH

    nc = Bp + Bs
    nc_pad = -(-nc // 16) * 16
    c_all = jnp.pad(jnp.concatenate([c_sample, c_prompt], axis=0), ((0, nc_pad - nc), (0, 0)))
    c_act = (c_all * jax.nn.sigmoid(c_all)).astype(BF16)

    wr = jnp.pad(w_router.astype(F32), ((0, 0), (0, LANES - N_EXPERTS)))
    br = jnp.pad(b_router.astype(F32), ((0, LANES - N_EXPERTS),)).reshape(1, LANES)
    vec_spec = pl.BlockSpec((None, 1, D), lambda i: (0, 0, 0))
    vec = lambda a, l: a[l].reshape(1, 1, D)
    zero_bias = jnp.zeros((1, D), F32)

    w_in_t = jnp.swapaxes(w_in, 1, 2)
    tn = min(MM_TN, D)
    tm = MM_TM if T % MM_TM == 0 else TOK_BLOCK
    xp, xs = x_prompt.reshape(Tp, D), x_sample.reshape(Ts, D)
    x_shapes = [jax.ShapeDtypeStruct((Tp, D), F32), jax.ShapeDtypeStruct((Ts, D), F32)]
    def state_bufs(B):
        z = lambda *s: jnp.zeros((depth, B) + s, F32)
        return (z(*_STATE_DIMS), [z(*_STATE_DIMS), z(*_VEC_DIMS), z(*_VEC_DIMS)],
                [z(*_STATE_DIMS), z(CONV_W - 1, 3 * WIDTH)])

    hg_p, ml_p, gd_p = state_bufs(Bp)
    hg_s, ml_s, gd_s = state_bufs(Bs)
    mlstm_n_in = state_mlstm_n.reshape(depth, Bs, N_HEADS, 1, HEAD_DIM)
    mlstm_m_in = jnp.broadcast_to(state_mlstm_m[..., None, None], (depth, Bs, N_HEADS, 1, HEAD_DIM))
    for l in range(depth):
        mod = _matmul(c_act, w_ada, (None, D, 1024), lambda n: (l, 0, n),
                      b_ada.reshape(depth, 1, 6 * D), (None, 1, 1024), lambda n: (l, 0, n),
                      6 * D, F32, tm=nc_pad, tn=1024, name="ada").reshape(nc_pad, 6, D)

        h = pl.pallas_call(
            functools.partial(_modulate_kernel, npb=ts["npb"], spb=ts["spb"], sh_idx=0, sc_idx=1),
            grid=ts["grid"],
            in_specs=[ts["xp"], ts["xs"], ts["mp"], ts["ms"]],
            out_specs=ts["tok"],
            out_shape=jax.ShapeDtypeStruct((T, D), BF16),
            compiler_params=_params("arbitrary"),
            name="modulate",
        )(xp, xs, mod, mod)

        b_in3 = b_in.reshape(depth, 1, -1)
        proj_ab = _matmul(h, w_in_t, (None, tn, D), lambda n: (l, n, 0),
                          b_in3, (None, 1, tn), lambda n: (l, 0, n),
                          n_ab, F32, tm=tm, tn=tn, name="proj_ab", w_transposed=True)
        proj_c = _matmul(h, w_in_t[l, c0:c1, :], (tn, D), lambda n: (n, 0),
                         b_in[l, c0:c1].reshape(1, -1), (1, tn), lambda n: (0, n),
                         4 * WIDTH, F32, tm=tm, tn=tn, name="proj_c", w_transposed=True)
        w_g = jnp.pad(jnp.concatenate([w_in_t[l, g0:c0, :], w_in_t[l, c1:, :]], axis=0),
                      ((0, LANES - 4 * N_HEADS), (0, 0)))
        b_g = jnp.pad(jnp.concatenate([b_in[l, g0:c0], b_in[l, c1:]]), ((0, LANES - 4 * N_HEADS),))
        gates = _matmul(h, w_g, (LANES, D), lambda n: (n, 0), b_g.reshape(1, LANES), (1, LANES),
                        lambda n: (0, n), LANES, F32, tm=tm, tn=LANES, name="proj_gates", w_transposed=True)
        gate = _matmul(h, w_merge, (None, None, D, tn), lambda n: (l, n // (D // tn), 0, n % (D // tn)),
                       b_merge.reshape(depth, 3, 1, D), (None, None, 1, tn),
                       lambda n: (l, n // (D // tn), 0, n % (D // tn)),
                       3 * D, BF16, tm=tm, tn=tn, name="merge_gates", act="sigmoid")

        conv_w_t = gdn_conv_w[l].T.astype(F32)
        y_init = jnp.zeros((T, WIDTH), F32)
        ya, hg_p = _hgrn(proj_ab, 0, 0, Bp, Lp, hgrn_lb_logits, hgrn_norm_g[l], l, None, (y_init, hg_p))
        ya, hg_s = _hgrn(proj_ab, 0, Tp, Bs, Ls, hgrn_lb_logits, hgrn_norm_g[l], l, state_hgrn, (ya, hg_s))
        yb, *ml_p = _mlstm(proj_ab, 4, gates, 0, Bp, Lp, l, None, (y_init, *ml_p))
        yb, *ml_s = _mlstm(proj_ab, 4, gates, Tp, Bs, Ls, l, (state_mlstm_C, mlstm_n_in, mlstm_m_in),
                           (yb, *ml_s))
        yc, *gd_p = _gdn(proj_c, 0, gates, 0, Bp, Lp, conv_w_t, gdn_A_log[l], gdn_dt_bias[l],
                         gdn_norm_g[l], l, None, (y_init, *gd_p))
        yc, *gd_s = _gdn(proj_c, 0, gates, Tp, Bs, Ls, conv_w_t, gdn_A_log[l], gdn_dt_bias[l],
                         gdn_norm_g[l], l, (state_gdn, state_conv), (yc, *gd_s))

        merged = _merge((ya, yb, yc), gate, w_branch, l)
        mix = _matmul(merged, w_out, (None, D, tn), lambda n: (l, 0, n),
                      zero_bias, (1, tn), lambda n: (0, n), D, F32, tm=tm, tn=tn, name="out_proj")

        xp, xs, h2, rinfo = pl.pallas_call(
            functools.partial(_post_route_kernel, npb=tr["npb"], spb=tr["spb"], alpha=alpha),
            grid=tr["grid"],
            in_specs=[tr["xp"], tr["xs"], tr["tok"], tr["mp"], tr["ms"], vec_spec, vec_spec,
                      pl.BlockSpec((D, LANES), lambda i: (0, 0)), pl.BlockSpec((1, LANES), lambda i: (0, 0))],
            out_specs=[tr["xp"], tr["xs"], tr["tok"], tr["lanes"]],
            out_shape=[x_shapes[0], x_shapes[1], jax.ShapeDtypeStruct((T, D), F32),
                       jax.ShapeDtypeStruct((T, LANES), F32)],
            compiler_params=_params("arbitrary"),
            name="post_route",
        )(xp, xs, mix, mod, mod, vec(ln_mix_g, l), vec(ln_mix_b, l), wr, br)

        ffn = _moe(h2, rinfo, w_exp_gate, w_exp_up, w_exp_down, l)
        xp, xs = pl.pallas_call(
            functools.partial(_post_final_kernel, npb=tr["npb"], spb=tr["spb"], alpha=alpha),
            grid=tr["grid"],
            in_specs=[tr["xp"], tr["xs"], tr["tok"], tr["tok_hi"], tr["mp"], tr["ms"], vec_spec, vec_spec],
            out_specs=[tr["xp"], tr["xs"]],
            out_shape=x_shapes,
            compiler_params=_params("arbitrary"),
            name="post_final",
        )(xp, xs, ffn, ffn, mod, mod, vec(ln_ffn_g, l), vec(ln_ffn_b, l))

    as_vec = lambda n, B: n.reshape(depth, B, N_HEADS, HEAD_DIM)
    as_scalar = lambda m: m[:, :, :, 0, 0]
    return (xp.reshape(Bp, Lp, D), xs.reshape(Bs, Ls, D), hg_p, hg_s, ml_p[0], ml_s[0],
            as_vec(ml_p[1], Bp), as_vec(ml_s[1], Bs), as_scalar(ml_p[2]), as_scalar(ml_s[2]),
            gd_p[0], gd_s[0], gd_p[1], gd_s[1])
```

```python
import functools
import math

import numpy as np
import jax
import jax.numpy as jnp
from jax import lax
from jax.experimental import pallas as pl
from jax.experimental.pallas import tpu as pltpu

F32 = jnp.float32
BF16 = jnp.bfloat16

HEAD_DIM = 128
N_HEADS = 8
WIDTH = N_HEADS * HEAD_DIM
CHUNK = 64
CONV_W = 4
N_EXPERTS = 16
N_GROUPS = 4
EXPERTS_PER_GROUP = N_EXPERTS // N_GROUPS
TOP_K = 2
LN_EPS = 1e-5
NORM_EPS = 1e-6
LB_TINY = 1e-20
NEG_BIG = -1e30
LANES = 128
VMEM_LIMIT = 56 * 1024 * 1024
TOK_BLOCK = 256
ROUTE_BLOCK = 128
MM_TM = 1024
MM_TN = 512
SEQ_ROWS_PER_STEP = 256
SHORT_SEQS_PER_STEP = 8
LONG_SEQS_PER_STEP = 2
MOE_TM = 256
MOE_SLICES = 8
HEADS = tuple(slice(h * HEAD_DIM, (h + 1) * HEAD_DIM) for h in range(N_HEADS))


def _params(*sem):
    return pltpu.CompilerParams(dimension_semantics=sem, vmem_limit_bytes=VMEM_LIMIT)


def _dot(a, b):
    return jnp.dot(a.astype(BF16), b.astype(BF16), preferred_element_type=F32)


def _dot_nt(a, b):
    return lax.dot_general(a.astype(BF16), b.astype(BF16), (((1,), (1,)), ((), ())),
                           preferred_element_type=F32)


def _dot_tn(a, b):
    return lax.dot_general(a.astype(BF16), b.astype(BF16), (((0,), (0,)), ((), ())),
                           preferred_element_type=F32)


def _split2(x):
    x1 = x.astype(BF16)
    return x1, (x - x1.astype(F32)).astype(BF16)


def _dot_exact_l(a_bf, x):
    x1, x2 = _split2(x)
    d = lambda p: jnp.dot(a_bf, p, preferred_element_type=F32)
    return d(x1) + d(x2)


def _dot_exact_r(x, b_bf):
    x1, x2 = _split2(x)
    d = lambda p: jnp.dot(p, b_bf, preferred_element_type=F32)
    return d(x1) + d(x2)


def _dot3(a, b):
    a1, a2 = _split2(a)
    b1, b2 = _split2(b)
    d = lambda p, q: jnp.dot(p, q, preferred_element_type=F32)
    return d(a1, b1) + (d(a1, b2) + d(a2, b1))


def _sigmoid(x):
    return jax.nn.sigmoid(x)


def _silu(x):
    return x * jax.nn.sigmoid(x)


def _log_sigmoid(x):
    return jnp.minimum(x, 0.0) - jnp.log1p(jnp.exp(-jnp.abs(x)))


def _softplus(x):
    return jnp.maximum(x, 0.0) + jnp.log1p(jnp.exp(-jnp.abs(x)))


def _logaddexp(a, b):
    return jnp.maximum(a, b) + jnp.log1p(jnp.exp(-jnp.abs(a - b)))


def _tri_masks(cn):
    r = lax.broadcasted_iota(jnp.int32, (cn, cn), 0)
    c = lax.broadcasted_iota(jnp.int32, (cn, cn), 1)
    return r >= c, r > c, r == c


def _rms_gate(o, ng, gate):
    return o * lax.rsqrt(jnp.mean(o * o, axis=-1, keepdims=True) + NORM_EPS) * ng * gate


def _mm_kernel(x_ref, w_ref, b_ref, o_ref, wbf_ref, *, act, w_transposed):
    @pl.when(pl.program_id(1) == 0)
    def _():
        if w_transposed:
            tn = w_ref.shape[0]
            step = min(tn, 256)
            for c in range(0, tn, step):
                wbf_ref[:, c:c + step] = w_ref[c:c + step, :].T.astype(BF16)
        else:
            wbf_ref[...] = w_ref[...].astype(BF16)

    acc = jnp.dot(x_ref[...], wbf_ref[...], preferred_element_type=F32) + b_ref[...]
    if act == "sigmoid":
        acc = _sigmoid(acc)
    o_ref[...] = acc.astype(o_ref.dtype)


def _matmul(x, w, w_block, w_index, b, b_block, b_index, n_out, out_dtype, *, tm, tn, name, act=None,
            w_transposed=False):
    M, K = x.shape
    tm = min(tm, M)
    assert M % tm == 0 and n_out % tn == 0, (M, tm, n_out, tn)
    return pl.pallas_call(
        functools.partial(_mm_kernel, act=act, w_transposed=w_transposed),
        grid=(n_out // tn, M // tm),
        in_specs=[pl.BlockSpec((tm, K), lambda n, m: (m, 0)),
                  pl.BlockSpec(w_block, lambda n, m: w_index(n)),
                  pl.BlockSpec(b_block, lambda n, m: b_index(n))],
        out_specs=pl.BlockSpec((tm, tn), lambda n, m: (m, n)),
        out_shape=jax.ShapeDtypeStruct((M, n_out), out_dtype),
        scratch_shapes=[pltpu.VMEM((K, tn), BF16)],
        compiler_params=_params("arbitrary", "arbitrary"),
        name=name,
    )(x, w, b)


def _with_mod(npb, mp_ref, ms_ref, body):
    i = pl.program_id(0)

    @pl.when(i < npb)
    def _():
        body(mp_ref[...], 0)

    @pl.when(i >= npb)
    def _():
        body(ms_ref[...], 1)


def _group_specs(npb, tb, D):
    xp = pl.BlockSpec((tb, D), lambda i: (jnp.minimum(i, npb - 1), 0))
    xs = pl.BlockSpec((tb, D), lambda i: (jnp.maximum(i - npb, 0), 0))
    return xp, xs


def _mod_specs(npb, n_sample_seq, blocks_per_prompt_seq, n_prompt_seq, spb, D):
    mp = pl.BlockSpec((1, 6, D), lambda i: (n_sample_seq + jnp.minimum(i // blocks_per_prompt_seq,
                                                                        n_prompt_seq - 1), 0, 0))
    ms = pl.BlockSpec((spb, 6, D), lambda i: (jnp.maximum(i - npb, 0), 0, 0))
    return mp, ms


def _modulate_kernel(xp_ref, xs_ref, mp_ref, ms_ref, o_ref, *, npb, spb, sh_idx, sc_idx):
    tb, D = xp_ref.shape

    def body(m, group):
        x = (xp_ref, xs_ref)[group][...].reshape(spb, tb // spb, D)
        h = x * (1.0 + m[:, sc_idx:sc_idx + 1, :]) + m[:, sh_idx:sh_idx + 1, :]
        o_ref[...] = h.reshape(tb, D).astype(o_ref.dtype)

    _with_mod(npb, mp_ref, ms_ref, body)


def _layer_norm(z, g, b):
    mu = jnp.mean(z, axis=-1, keepdims=True)
    zc = z - mu
    var = jnp.mean(zc * zc, axis=-1, keepdims=True)
    return zc * lax.rsqrt(var + LN_EPS) * g + b


def _route(logits):
    lane = lax.broadcasted_iota(jnp.int32, logits.shape, 1)
    valid = lane < N_EXPERTS
    lg = jnp.where(valid, logits, NEG_BIG)
    ex = jnp.where(valid, jnp.exp(lg - jnp.max(lg, axis=-1, keepdims=True)), 0.0)
    probs = ex / jnp.sum(ex, axis=-1, keepdims=True)
    big = LANES
    best = None
    for g in range(N_GROUPS):
        in_g = (lane >= g * EXPERTS_PER_GROUP) & (lane < (g + 1) * EXPERTS_PER_GROUP)
        m1 = jnp.max(jnp.where(in_g, probs, -1.0), axis=-1, keepdims=True)
        i1 = jnp.min(jnp.where(in_g & (probs == m1), lane, big), axis=-1, keepdims=True)
        rest = in_g & (lane != i1)
        m2 = jnp.max(jnp.where(rest, probs, -1.0), axis=-1, keepdims=True)
        i2 = jnp.min(jnp.where(rest & (probs == m2), lane, big), axis=-1, keepdims=True)
        cand = (m1 + m2, i1, i2, m1, m2)
        if best is None:
            best = cand
        else:
            better = cand[0] > best[0]
            best = tuple(jnp.where(better, c, o) for c, o in zip(cand, best))
    _, i1, i2, m1, m2 = best
    tot = m1 + m2
    return (jnp.where(lane == 0, i1.astype(F32), 0.0) + jnp.where(lane == 1, i2.astype(F32), 0.0)
            + jnp.where(lane == 2, m1 / tot, 0.0) + jnp.where(lane == 3, m2 / tot, 0.0))


def _post_route_kernel(xp_ref, xs_ref, y_ref, mp_ref, ms_ref, g_ref, b_ref, wr_ref, br_ref,
                       xop_ref, xos_ref, h_ref, r_ref, *, npb, spb, alpha):
    tb, D = xp_ref.shape

    def body(m, group):
        x = (xp_ref, xs_ref)[group][...].reshape(spb, tb // spb, D)
        y = y_ref[...].reshape(spb, tb // spb, D)
        xn = _layer_norm(alpha * x + m[:, 2:3, :] * y, g_ref[...], b_ref[...])
        (xop_ref, xos_ref)[group][...] = xn.reshape(tb, D)
        h2 = (xn * (1.0 + m[:, 4:5, :]) + m[:, 3:4, :]).reshape(tb, D)
        h_ref[...] = h2
        r_ref[...] = _route(_dot3(h2, wr_ref[...]) + br_ref[...])

    _with_mod(npb, mp_ref, ms_ref, body)


def _post_final_kernel(xp_ref, xs_ref, y0_ref, y1_ref, mp_ref, ms_ref, g_ref, b_ref,
                       xop_ref, xos_ref, *, npb, spb, alpha):
    tb, D = xp_ref.shape

    def body(m, group):
        x = (xp_ref, xs_ref)[group][...].reshape(spb, tb // spb, D)
        y = (y0_ref[...] + y1_ref[...]).reshape(spb, tb // spb, D)
        xn = _layer_norm(alpha * x + m[:, 5:6, :] * y, g_ref[...], b_ref[...])
        (xop_ref, xos_ref)[group][...] = xn.reshape(tb, D)

    _with_mod(npb, mp_ref, ms_ref, body)


def _seq_blocking(L):
    cn = math.gcd(L, CHUNK)
    lb = min(L, SEQ_ROWS_PER_STEP)
    assert L % lb == 0 and lb % cn == 0
    return cn, lb, L // lb


def _step_major_block(i, tb, B, L):
    _, lb, _ = _seq_blocking(L)
    per_seq, per_step = L // tb, lb // tb
    b, rem = i // per_seq, i % per_seq
    return ((rem // per_step) * B + b) * per_step + rem % per_step


def _run_chunks(chunk, nchunk):
    if nchunk == 1:
        chunk(0, 0)
    else:
        lax.fori_loop(0, nchunk, chunk, 0)


def _hgrn_tables(cn):
    nlev = int(math.log2(cn))
    assert 1 << nlev == cn
    r = np.arange(cn)
    mats = [np.tril(np.ones((cn, cn)))]
    masks = [np.eye(cn)]
    j = r[None, :]
    for lev in range(nlev):
        s = cn >> (lev + 1)
        blk = 2 * s
        pos = r % blk
        ref = r - pos + s - 1
        second = pos >= s
        m_second = (j > ref[:, None]) & (j <= r[:, None])
        m_first = (j > r[:, None]) & (j <= ref[:, None])
        mats.append(np.where(second[:, None], m_second, m_first))
        same = (r[:, None] // blk) == (r[None, :] // blk)
        masks.append(same & second[:, None] & (~second)[None, :])
    return (np.concatenate(mats, 0).astype(np.float32), np.stack(masks).astype(np.float32))


def _hgrn_kernel(*refs, layer, cn, nchunk, nseq, has_state, n_alias):
    refs = list(refs)
    lbl_ref, ng_ref, mall_ref, masks_ref, q_ref, f_ref, i_ref, g_ref = refs[:8]
    pos = 8
    s0_ref = None
    if has_state:
        s0_ref = refs[pos]
        pos += 1
    pos += n_alias
    y_ref, s_ref, st_ref = refs[pos:pos + 3]
    j = pl.program_id(1)
    nlev = masks_ref.shape[0] - 1

    seqs = range(nseq)
    units = [(sq, h) for sq in seqs for h in range(N_HEADS)]
    seq_rows = q_ref.shape[0] // nseq

    @pl.when(j == 0)
    def _():
        for n, (sq, h) in enumerate(units):
            if has_state:
                st_ref[n] = s0_ref[sq, h].T
            else:
                st_ref[n] = jnp.zeros((HEAD_DIM, HEAD_DIM), F32)

    logits = lbl_ref[...]
    e = jnp.exp(logits - jnp.max(logits, axis=0, keepdims=True))
    p = e / jnp.sum(e, axis=0, keepdims=True)
    lb = jnp.zeros((1, WIDTH), F32)
    for l in range(1, layer + 1):
        lb = lb + p[l:l + 1, :]
    log_lb = jnp.log(jnp.maximum(lb, LB_TINY))
    log1m_lb = jnp.log1p(-lb)
    one_m_lb = 1.0 - lb
    ng = ng_ref[...]
    mall = mall_ref[...]

    def chunk(c, carry):
        r0 = pl.multiple_of(c * cn, cn)
        rows = [pl.ds(r0 + sq * seq_rows, cn) for sq in seqs]
        q = [_silu(q_ref[rw, :]) for rw in rows]
        fr = [f_ref[rw, :] for rw in rows]
        v = [i_ref[rw, :].astype(BF16) for rw in rows]
        log_f = [_logaddexp(log_lb, log1m_lb + _log_sigmoid(x)) for x in fr]
        k = [one_m_lb * _sigmoid(-x) for x in fr]
        dall = [_dot_exact_l(mall, x) for x in log_f]
        b = [x[:cn] for x in dall]
        qb = [x.astype(BF16) for x in q]
        kb = [x.astype(BF16) for x in k]
        att = [masks_ref[0] * _dot_nt(qb[sq][:, HEADS[h]], kb[sq][:, HEADS[h]]) for sq, h in units]
        for lev in range(1, nlev + 1):
            ex = [jnp.exp(x[lev * cn:(lev + 1) * cn]) for x in dall]
            qe = [(x * e).astype(BF16) for x, e in zip(q, ex)]
            ke = [(x * e).astype(BF16) for x, e in zip(k, ex)]
            att = [a + masks_ref[lev] * _dot_nt(qe[sq][:, HEADS[h]], ke[sq][:, HEADS[h]])
                   for a, (sq, h) in zip(att, units)]
        st = [st_ref[n] for n in range(len(units))]
        qd = [(x * jnp.exp(bb)).astype(BF16) for x, bb in zip(q, b)]
        o = [_dot(a, v[sq][:, HEADS[h]]) + _dot_nt(qd[sq][:, HEADS[h]], s)
             for a, (sq, h), s in zip(att, units, st)]
        b_last = [bb[cn - 1:cn] for bb in b]
        k_dec = [(x * jnp.exp(bl - bb)).astype(BF16) for x, bl, bb in zip(k, b_last, b)]
        dec = [jnp.exp(bl) for bl in b_last]
        for n, (sq, h) in enumerate(units):
            cs = HEADS[h]
            st_ref[n] = st[n] * dec[sq][:, cs] + _dot_tn(v[sq][:, cs], k_dec[sq][:, cs])
        gate = [_sigmoid(g_ref[rw, :]) for rw in rows]
        for n, (sq, h) in enumerate(units):
            cs = HEADS[h]
            y_ref[rows[sq], cs] = _rms_gate(o[n], ng[:, cs], gate[sq][:, cs])
        return carry

    _run_chunks(chunk, nchunk)

    @pl.when(j == pl.num_programs(1) - 1)
    def _():
        for n, (sq, h) in enumerate(units):
            s_ref[sq, h] = st_ref[n].T


def _mixer_specs(T_total, row0, B, L):
    cn, LB, ncb = _seq_blocking(L)
    nseq = math.gcd(B, SHORT_SEQS_PER_STEP if LB <= CHUNK else LONG_SEQS_PER_STEP)
    rows = nseq * LB
    assert row0 % rows == 0
    rb0 = row0 // rows
    blk = lambda g, j: j * (B // nseq) + g
    col = lambda cidx: pl.BlockSpec((rows, WIDTH), lambda g, j: (rb0 + blk(g, j), cidx))
    yspec = pl.BlockSpec((rows, WIDTH), lambda g, j: (rb0 + blk(g, j), 0))
    gspec = pl.BlockSpec((rows, LANES), lambda g, j: (rb0 + blk(g, j), 0))
    gtspec = pl.BlockSpec((rows // cn, 4 * N_HEADS, cn), lambda g, j: (blk(g, j), 0, 0))
    return cn, LB, ncb, nseq, col, yspec, gspec, gtspec


def _state_spec(layer, trailing, nseq):
    return pl.BlockSpec((None, nseq) + trailing, lambda b, j: (layer, b) + (0,) * len(trailing))


_STATE_DIMS = (N_HEADS, HEAD_DIM, HEAD_DIM)
_VEC_DIMS = (N_HEADS, 1, HEAD_DIM)


def _full(a):
    return pl.BlockSpec(a.shape, lambda b, j: (0,) * a.ndim)


def _alias_outputs(args, in_specs, prevs):
    aliases = {}
    for k, prev in enumerate(prevs):
        args.append(prev)
        in_specs.append(pl.BlockSpec(memory_space=pl.ANY))
        aliases[len(args) - 1] = k
    return aliases


def _shapes(prevs):
    return [jax.ShapeDtypeStruct(p.shape, p.dtype) for p in prevs]


def _hgrn(proj, a_col0, row0, B, L, lb_logits, norm_g, layer, state, prevs):
    cn, LB, ncb, nseq, col, yspec, _, _ = _mixer_specs(proj.shape[0], row0, B, L)
    mall, masks = _hgrn_tables(cn)
    mall = jnp.asarray(mall, BF16)
    masks = jnp.asarray(masks)
    ng = jnp.tile(norm_g.reshape(1, HEAD_DIM).astype(F32), (1, N_HEADS))
    sspec = _state_spec(layer, _STATE_DIMS, nseq)
    args = [lb_logits, ng, mall, masks, proj, proj, proj, proj]
    in_specs = [_full(lb_logits), _full(ng), _full(mall), _full(masks)] + [col(a_col0 + t) for t in range(4)]
    if state is not None:
        args.append(state)
        in_specs.append(sspec)
    aliases = _alias_outputs(args, in_specs, prevs)
    return pl.pallas_call(
        functools.partial(_hgrn_kernel, layer=layer, cn=cn, nchunk=LB // cn, nseq=nseq,
                          has_state=state is not None, n_alias=len(prevs)),
        grid=(B // nseq, ncb),
        in_specs=in_specs,
        out_specs=[yspec, sspec],
        out_shape=_shapes(prevs),
        scratch_shapes=[pltpu.VMEM((nseq * N_HEADS, HEAD_DIM, HEAD_DIM), F32)],
        input_output_aliases=aliases,
        compiler_params=_params("arbitrary", "arbitrary"),
        name="hgrn2",
    )(*args)


def _head_cols(mat, base):
    return [mat[:, base + h:base + h + 1] for h in range(N_HEADS)]


def _head_rows(mat, base):
    return [mat[base + h:base + h + 1, :] for h in range(N_HEADS)]


def _mlstm_kernel(*refs, cn, nchunk, nseq, has_state, n_alias):
    refs = list(refs)
    gt_ref, gtt_ref, q_ref, k_ref, v_ref, o_ref = refs[:6]
    pos = 6
    if has_state:
        c0_ref, n0_ref, m0_ref = refs[pos:pos + 3]
        pos += 3
    pos += n_alias
    y_ref, c_ref, n_ref, m_ref = refs[pos:pos + 4]
    j = pl.program_id(1)

    @pl.when(j == 0)
    def _():
        if has_state:
            c_ref[...] = c0_ref[...]
            n_ref[...] = n0_ref[...]
            m_ref[...] = m0_ref[...]
        else:
            c_ref[...] = jnp.zeros(c_ref.shape, F32)
            n_ref[...] = jnp.zeros(n_ref.shape, F32)
            m_ref[...] = jnp.zeros(m_ref.shape, F32)

    incl, _, _ = _tri_masks(cn)
    tril = incl.astype(BF16)
    triu = jnp.logical_not(_tri_masks(cn)[1]).astype(BF16)

    seqs = range(nseq)
    units = [(sq, h) for sq in seqs for h in range(N_HEADS)]
    lb = q_ref.shape[0] // nseq

    def chunk(c, carry):
        r0 = pl.multiple_of(c * cn, cn)
        rows = [pl.ds(r0 + sq * lb, cn) for sq in seqs]
        gates = [gt_ref[rw, :] for rw in rows]
        gates_r = [gtt_ref[sq * nchunk + c] for sq in seqs]
        f_c = [_dot_exact_l(tril, _log_sigmoid(g)) for g in gates]
        f_r = [_dot_exact_r(_log_sigmoid(g), triu) for g in gates_r]
        Fc = [x for sq in seqs for x in _head_cols(f_c[sq], N_HEADS)]
        ic = [x for sq in seqs for x in _head_cols(gates[sq], 0)]
        Fr = [x for sq in seqs for x in _head_rows(f_r[sq], N_HEADS)]
        ir = [x for sq in seqs for x in _head_rows(gates_r[sq], 0)]
        q = [q_ref[rows[sq], HEADS[h]].astype(BF16) for sq, h in units]
        k = [k_ref[rows[sq], HEADS[h]] * (HEAD_DIM ** -0.5) for sq, h in units]
        kb = [x.astype(BF16) for x in k]
        v = [v_ref[rows[sq], HEADS[h]].astype(BF16) for sq, h in units]
        C = [c_ref[sq, h] for sq, h in units]
        n = [n_ref[sq, h] for sq, h in units]
        m = [m_ref[sq, h][:, 0:1] for sq, h in units]
        qk = [_dot_nt(a, b) for a, b in zip(q, kb)]
        qc = [_dot(a, b) for a, b in zip(q, C)]
        raw = [a - b + d for a, b, d in zip(Fc, Fr, ir)]
        inter = [a + b for a, b in zip(Fc, m)]
        m_t = [jnp.maximum(a, jnp.max(jnp.where(incl, r, NEG_BIG), axis=-1, keepdims=True))
               for a, r in zip(inter, raw)]
        w_inter = [jnp.exp(a - b) for a, b in zip(inter, m_t)]
        s = [jnp.where(incl, a * jnp.exp(jnp.where(incl, r - mt, 0.0)), 0.0)
             for a, r, mt in zip(qk, raw, m_t)]
        num = [_dot(a, b) + w * d for a, b, w, d in zip(s, v, w_inter, qc)]
        den = [jnp.sum(a, axis=-1, keepdims=True)
               + w * jnp.sum(q_ref[rows[sq], HEADS[h]] * nn, axis=-1, keepdims=True)
               for a, w, (sq, h), nn in zip(s, w_inter, units, n)]
        hh = [a / jnp.maximum(jnp.abs(d), jnp.exp(-mt)) for a, d, mt in zip(num, den, m_t)]
        for u, (sq, h) in enumerate(units):
            m_new = m_t[u][cn - 1:cn]
            w_c = jnp.exp(inter[u][cn - 1:cn] - m_new)
            w_s = jnp.exp(Fc[u][cn - 1:cn] - Fc[u] + ic[u] - m_new)
            ks = k[u] * w_s
            c_ref[sq, h] = w_c * C[u] + _dot_tn(ks, v[u])
            n_ref[sq, h] = w_c * n[u] + jnp.sum(ks, axis=0, keepdims=True)
            m_ref[sq, h] = jnp.broadcast_to(m_new, (1, HEAD_DIM))
            y_ref[rows[sq], HEADS[h]] = _sigmoid(o_ref[rows[sq], HEADS[h]]) * hh[u]
        return carry

    _run_chunks(chunk, nchunk)


def _gates_t(gates, row0, B, L, cn):
    g = gates[row0:row0 + B * L, :4 * N_HEADS]
    return g.reshape(B * L // cn, cn, 4 * N_HEADS).transpose(0, 2, 1)


def _mlstm(proj, b_col0, gates, row0, B, L, layer, state, prevs):
    cn, LB, ncb, nseq, col, yspec, gspec, gtspec = _mixer_specs(proj.shape[0], row0, B, L)
    cspec = _state_spec(layer, _STATE_DIMS, nseq)
    nspec = _state_spec(layer, _VEC_DIMS, nseq)
    gt = _gates_t(gates, row0, B, L, cn)
    args = [gates, gt, proj, proj, proj, proj]
    in_specs = [gspec, gtspec] + [col(b_col0 + t) for t in range(4)]
    if state is not None:
        args += list(state)
        in_specs += [cspec, nspec, nspec]
    aliases = _alias_outputs(args, in_specs, prevs)
    return pl.pallas_call(
        functools.partial(_mlstm_kernel, cn=cn, nchunk=LB // cn, nseq=nseq, has_state=state is not None,
                          n_alias=len(prevs)),
        grid=(B // nseq, ncb),
        in_specs=in_specs,
        out_specs=[yspec, cspec, nspec, nspec],
        out_shape=_shapes(prevs),
        input_output_aliases=aliases,
        compiler_params=_params("arbitrary", "arbitrary"),
        name="mlstm",
    )(*args)


_CONV_PAD = 8


def _gdn_kernel(*refs, cn, nchunk, nseq, has_state, n_alias):
    refs = list(refs)
    (gt_ref, gtt_ref, cw_ref, al_ref, dtb_ref, alc_ref, dtbc_ref, ng_ref,
     xq_ref, xk_ref, xv_ref, g_ref) = refs[:12]
    pos = 12
    if has_state:
        s0_ref, b0_ref = refs[pos:pos + 2]
        pos += 2
    pos += n_alias
    y_ref, s_ref, buf_ref, xp_ref, cv_ref = refs[pos:pos + 5]
    j = pl.program_id(1)
    lb = xq_ref.shape[0] // nseq
    hist = CONV_W - 1
    h0 = _CONV_PAD - hist
    seqs = range(nseq)
    units = [(sq, h) for sq in seqs for h in range(N_HEADS)]

    @pl.when(j == 0)
    def _():
        if has_state:
            s_ref[...] = s0_ref[...]
        else:
            s_ref[...] = jnp.zeros(s_ref.shape, F32)

    for sq in seqs:
        blk = slice(sq * lb, (sq + 1) * lb)

        @pl.when(j == 0)
        def _():
            xp_ref[h0:_CONV_PAD, :] = b0_ref[sq] if has_state else jnp.zeros((hist, 3 * WIDTH), F32)

        @pl.when(j > 0)
        def _():
            xp_ref[h0:_CONV_PAD, :] = buf_ref[sq]
        xp_ref[_CONV_PAD:_CONV_PAD + lb, 0:WIDTH] = xq_ref[blk, :]
        xp_ref[_CONV_PAD:_CONV_PAD + lb, WIDTH:2 * WIDTH] = xk_ref[blk, :]
        xp_ref[_CONV_PAD:_CONV_PAD + lb, 2 * WIDTH:3 * WIDTH] = xv_ref[blk, :]
        acc = xp_ref[h0:h0 + lb, :] * cw_ref[0:1, :]
        for t in range(1, CONV_W):
            acc = acc + xp_ref[h0 + t:h0 + t + lb, :] * cw_ref[t:t + 1, :]
        cv_ref[blk, :] = _silu(acc)
        buf_ref[sq] = xp_ref[_CONV_PAD + lb - hist:_CONV_PAD + lb, :]

    incl, strict, eye = _tri_masks(cn)
    tril = incl.astype(BF16)
    triu = jnp.logical_not(strict).astype(BF16)
    eye_f = eye.astype(F32)
    lane = lax.broadcasted_iota(jnp.int32, (1, LANES), 1)
    is_a = (lane >= 2 * N_HEADS) & (lane < 3 * N_HEADS)
    sub = lax.broadcasted_iota(jnp.int32, (4 * N_HEADS, 1), 0)
    is_a_r = (sub >= 2 * N_HEADS) & (sub < 3 * N_HEADS)
    neg_rate = -jnp.exp(al_ref[...])
    dtb = dtb_ref[...]
    neg_rate_r = -jnp.exp(alc_ref[...])
    dtb_r = dtbc_ref[...]
    ng = ng_ref[...]
    nsq = int(math.log2(cn))

    def chunk(c, carry):
        r0 = pl.multiple_of(c * cn, cn)
        rows = [pl.ds(r0 + sq * lb, cn) for sq in seqs]
        gam_c, gam_r, beta = [], [], []
        for sq in seqs:
            gates = gt_ref[rows[sq], :]
            gates_r = gtt_ref[sq * nchunk + c]
            g_c = jnp.where(is_a, neg_rate * _softplus(gates + dtb), 0.0)
            g_r = jnp.where(is_a_r, neg_rate_r * _softplus(gates_r + dtb_r), 0.0)
            gam_c += _head_cols(_dot_exact_l(tril, g_c), 2 * N_HEADS)
            gam_r += _head_rows(_dot_exact_r(g_r, triu), 2 * N_HEADS)
            beta += _head_cols(_sigmoid(gates), 3 * N_HEADS)
        q, k, v = [], [], []
        for sq, h in units:
            cs = HEADS[h]
            qr = cv_ref[rows[sq], cs]
            kr = cv_ref[rows[sq], slice(WIDTH + cs.start, WIDTH + cs.stop)]
            v.append(cv_ref[rows[sq], slice(2 * WIDTH + cs.start, 2 * WIDTH + cs.stop)])
            q.append(qr * (lax.rsqrt(jnp.sum(qr * qr, axis=-1, keepdims=True) + NORM_EPS) * (HEAD_DIM ** -0.5)))
            k.append(kr * lax.rsqrt(jnp.sum(kr * kr, axis=-1, keepdims=True) + NORM_EPS))
        kb = [x.astype(BF16) for x in k]
        S = [s_ref[sq, h] for sq, h in units]
        kk = [_dot_nt(a, a) for a in kb]
        qk = [_dot_nt(a, b) for a, b in zip(q, kb)]
        dec = [jnp.exp(jnp.where(incl, a - b, 0.0)) for a, b in zip(gam_c, gam_r)]
        pw = [jnp.where(strict, -(bt * a * d), 0.0) for bt, a, d in zip(beta, kk, dec)]
        t_inv = [eye_f + a for a in pw]
        for _ in range(nsq - 1):
            pw = [_dot3(a, a) for a in pw]
            t_inv = [t + _dot3(t, a) for t, a in zip(t_inv, pw)]
        eg = [jnp.exp(a) for a in gam_c]
        sol_v = [_dot3(t, bt * x) for t, bt, x in zip(t_inv, beta, v)]
        sol_k = [_dot3(t, (bt * e) * x) for t, bt, e, x in zip(t_inv, beta, eg, k)]
        u = [a - _dot(b, s) for a, b, s in zip(sol_v, sol_k, S)]
        o = [_dot(jnp.where(incl, a * d, 0.0), uu) + _dot(qq * e, s)
             for a, d, uu, qq, e, s in zip(qk, dec, u, q, eg, S)]
        for n, (sq, h) in enumerate(units):
            g_last = gam_c[n][cn - 1:cn]
            s_ref[sq, h] = jnp.exp(g_last) * S[n] + _dot_tn(k[n] * jnp.exp(g_last - gam_c[n]), u[n])
            y_ref[rows[sq], HEADS[h]] = _rms_gate(o[n], ng, _silu(g_ref[rows[sq], HEADS[h]]))
        return carry

    _run_chunks(chunk, nchunk)


def _gdn(proj, c_col0, gates, row0, B, L, conv_w_t, a_log, dt_bias, norm_g, layer, state, prevs):
    cn, LB, ncb, nseq, col, yspec, gspec, gtspec = _mixer_specs(proj.shape[0], row0, B, L)
    hist = CONV_W - 1
    sspec = _state_spec(layer, _STATE_DIMS, nseq)
    bspec = _state_spec(layer, (hist, 3 * WIDTH), nseq)
    a0, a1 = 2 * N_HEADS, 3 * N_HEADS
    al = jnp.zeros((1, LANES), F32).at[0, a0:a1].set(a_log.astype(F32))
    db = jnp.zeros((1, LANES), F32).at[0, a0:a1].set(dt_bias.astype(F32))
    alc = al[0, :4 * N_HEADS].reshape(4 * N_HEADS, 1)
    dbc = db[0, :4 * N_HEADS].reshape(4 * N_HEADS, 1)
    ng = norm_g.reshape(1, HEAD_DIM).astype(F32)
    gt = _gates_t(gates, row0, B, L, cn)
    args = [gates, gt, conv_w_t, al, db, alc, dbc, ng, proj, proj, proj, proj]
    in_specs = [gspec, gtspec, _full(conv_w_t), _full(al), _full(db), _full(alc), _full(dbc), _full(ng)]
    in_specs += [col(c_col0 + t) for t in range(4)]
    if state is not None:
        args += [state[0], state[1]]
        in_specs += [sspec, bspec]
    aliases = _alias_outputs(args, in_specs, prevs)
    return pl.pallas_call(
        functools.partial(_gdn_kernel, cn=cn, nchunk=LB // cn, nseq=nseq, has_state=state is not None,
                          n_alias=len(prevs)),
        grid=(B // nseq, ncb),
        in_specs=in_specs,
        out_specs=[yspec, sspec, bspec],
        out_shape=_shapes(prevs),
        scratch_shapes=[pltpu.VMEM((_CONV_PAD + LB, 3 * WIDTH), F32),
                        pltpu.VMEM((nseq * LB, 3 * WIDTH), F32)],
        input_output_aliases=aliases,
        compiler_params=_params("arbitrary", "arbitrary"),
        name="gdn",
    )(*args)


def _merge_kernel(ya_ref, yb_ref, yc_ref, ga_ref, gb_ref, gc_ref, wb_ref, o_ref, wbf_ref):
    @pl.when(pl.program_id(1) == 0)
    def _():
        wbf_ref[...] = wb_ref[...].astype(BF16)

    acc = None
    for jdx, (y_ref, g_ref) in enumerate(((ya_ref, ga_ref), (yb_ref, gb_ref), (yc_ref, gc_ref))):
        br = jnp.dot(y_ref[...].astype(BF16), wbf_ref[jdx], preferred_element_type=F32)
        term = g_ref[...].astype(F32) * br
        acc = term if acc is None else acc + term
    o_ref[...] = acc.astype(o_ref.dtype)


def _merge(ys, gate, w_branch, layer, tm=256, tn=1024):
    T, W = ys[0].shape
    D = w_branch.shape[-1]
    tn = min(tn, D)
    nb = D // tn
    yspec = pl.BlockSpec((tm, W), lambda n, m: (m, 0))
    gspec = lambda jdx: pl.BlockSpec((tm, tn), lambda n, m: (m, jdx * nb + n))
    return pl.pallas_call(
        _merge_kernel,
        grid=(nb, T // tm),
        in_specs=[yspec, yspec, yspec, gspec(0), gspec(1), gspec(2),
                  pl.BlockSpec((None, 3, W, tn), lambda n, m: (layer, 0, 0, n))],
        out_specs=pl.BlockSpec((tm, tn), lambda n, m: (m, n)),
        out_shape=jax.ShapeDtypeStruct((T, D), BF16),
        scratch_shapes=[pltpu.VMEM((3, W, tn), BF16)],
        compiler_params=_params("arbitrary", "arbitrary"),
        name="merge",
    )(ys[0], ys[1], ys[2], gate, gate, gate, w_branch)


def _moe_kernel(it_tile_ref, it_slice_ref, it_flag_ref, te_ref, nv_ref, tok_ref, dst_ref,
                h_hbm, roww_ref, wg_ref, wu_ref, wd_ref, out_hbm,
                xbuf, xb_ref, acc_ref, wgb_ref, wub_ref, wdb_ref, gsem, ssem, *, tm, n_real):
    i = pl.program_id(0)
    r = it_tile_ref[i]
    s = it_slice_ref[i]
    flag = it_flag_ref[i]
    nv = nv_ref[0]
    dk = wg_ref.shape[0]
    fk = wd_ref.shape[0]
    scatter_rows = min(tm, 128)

    def gather_start(tile):
        base = tile * tm

        def body(i, carry):
            tok = tok_ref[base + i]
            pltpu.make_async_copy(h_hbm.at[pl.ds(tok, 1)], xbuf.at[pl.ds(i, 1)], gsem).start()
            return carry

        lax.fori_loop(0, tm, body, 0, unroll=8)

    def gather_wait():
        pltpu.make_async_copy(h_hbm.at[pl.ds(0, tm)], xbuf, gsem).wait()

    def scatter_wait():
        pltpu.make_async_copy(acc_ref, out_hbm.at[pl.ds(0, tm)], ssem).wait()

    @pl.when(i == 0)
    def _():
        acc_ref[...] = jnp.zeros(acc_ref.shape, F32)
        for half in range(2):
            fill = pltpu.make_async_copy(acc_ref, out_hbm.at[pl.ds(n_real + half * tm, tm)], ssem)
            fill.start()
            fill.wait()
        gather_start(0)

    @pl.when((flag & 1) != 0)
    def _():
        k0 = pl.multiple_of(s * dk, dk)
        wgb_ref[pl.ds(k0, dk), :] = wg_ref[...].astype(BF16)
        wub_ref[pl.ds(k0, dk), :] = wu_ref[...].astype(BF16)
        f0 = pl.multiple_of(s * fk, fk)
        wdb_ref[pl.ds(f0, fk), :] = wd_ref[...].astype(BF16)

    @pl.when((flag & 2) != 0)
    def _():
        gather_wait()
        xb_ref[...] = xbuf[...].astype(BF16)
        nxt = jnp.minimum(r + 1, nv - 1)
        for j in range(tm):
            tok = tok_ref[nxt * tm + j]
            pltpu.make_async_copy(h_hbm.at[pl.ds(tok, 1)], xbuf.at[pl.ds(j, 1)], gsem).start()

        x = xb_ref[...]
        g = jnp.dot(x, wgb_ref[...], preferred_element_type=F32)
        u = jnp.dot(x, wub_ref[...], preferred_element_type=F32)
        he = (_silu(g) * u).astype(BF16)

        @pl.when(r > 0)
        def _():
            scatter_wait()

        for c in range(tm // scatter_rows):
            rows = slice(c * scatter_rows, (c + 1) * scatter_rows)
            y = jnp.dot(he[rows], wdb_ref[...], preferred_element_type=F32)
            acc_ref[rows, :] = y * roww_ref[rows, 0:1]
            for j in range(scatter_rows):
                row = c * scatter_rows + j
                dst = dst_ref[r * tm + row]
                pltpu.make_async_copy(acc_ref.at[pl.ds(row, 1)], out_hbm.at[pl.ds(dst, 1)], ssem).start()

        @pl.when(r == nv - 1)
        def _():
            scatter_wait()
            gather_wait()


def _moe_plan(rinfo, T, tm, ns):
    ids = rinfo[:, 0:TOP_K].astype(jnp.int32)
    wts = rinfo[:, TOP_K:2 * TOP_K]
    n_assign = TOP_K * T
    n_rows = n_assign + N_EXPERTS * tm
    n_tiles = n_rows // tm
    e_flat = ids.reshape(-1)
    onehot = (e_flat[:, None] == jnp.arange(N_EXPERTS, dtype=jnp.int32)[None, :]).astype(jnp.int32)
    rank = jnp.take_along_axis(jnp.cumsum(onehot, axis=0), e_flat[:, None], axis=1)[:, 0] - 1
    counts = jnp.sum(onehot, axis=0)
    padded = ((counts + tm - 1) // tm) * tm
    ends = jnp.cumsum(padded)
    starts = ends - padded
    dest = starts[e_flat] + rank
    upd = jnp.stack([jnp.arange(1, n_assign + 1, dtype=jnp.int32).astype(F32), wts.reshape(-1)], axis=1)
    rows = jnp.zeros((n_rows, 2), F32).at[dest].set(upd)
    row_a = rows[:, 0].astype(jnp.int32) - 1
    row_w = rows[:, 1]
    is_real = row_a >= 0
    tok = jnp.maximum(row_a, 0) // TOP_K
    slot = jnp.maximum(row_a, 0) % TOP_K
    p = jnp.arange(n_rows, dtype=jnp.int32)
    dump = n_assign + ((p // tm) % 2) * tm + (p % tm)
    row_tok = jnp.where(is_real, tok, 0)
    row_dst = jnp.where(is_real, slot * T + tok, dump)
    n_valid = (ends[-1] // tm).astype(jnp.int32)
    tile_start = jnp.arange(n_tiles, dtype=jnp.int32) * tm
    tile_e = jnp.minimum(jnp.searchsorted(ends, tile_start, side="right"), N_EXPERTS - 1).astype(jnp.int32)
    valid_t = tile_start < ends[-1]
    last_e = tile_e[jnp.maximum(n_valid - 1, 0)]
    tile_e = jnp.where(valid_t, tile_e, last_e)

    prev_e = jnp.concatenate([jnp.full((1,), -1, jnp.int32), tile_e[:-1]])
    first = valid_t & (tile_e != prev_e)
    cnt = jnp.where(valid_t, jnp.where(first, ns, 1), 0).astype(jnp.int32)
    item_end = jnp.cumsum(cnt)
    item_start = item_end - cnt
    n_items = n_tiles + N_EXPERTS * (ns - 1)
    it = jnp.arange(n_items, dtype=jnp.int32)
    live = it < item_end[-1]
    it_tile = jnp.minimum(jnp.searchsorted(item_end, it, side="right"), n_tiles - 1).astype(jnp.int32)
    it_tile = jnp.where(live, it_tile, jnp.maximum(n_valid - 1, 0))
    k = it - item_start[it_tile]
    loads = live & first[it_tile]
    it_slice = jnp.where(loads, k, ns - 1).astype(jnp.int32)
    computes = live & (~first[it_tile] | (k == ns - 1))
    it_flag = loads.astype(jnp.int32) + 2 * computes.astype(jnp.int32)
    return (it_tile, it_slice, it_flag, tile_e, n_valid.reshape(1), row_tok, row_dst,
            jnp.broadcast_to(row_w[:, None], (n_rows, LANES)))


def _moe(h2, rinfo, wg, wu, wd, layer, tm=MOE_TM, ns=MOE_SLICES):
    T, D = h2.shape
    FF = wg.shape[-1]
    assert D % ns == 0 and FF % ns == 0
    dk, fk = D // ns, FF // ns
    it_tile, it_slice, it_flag, tile_e, n_valid, row_tok, row_dst, row_w = _moe_plan(rinfo, T, tm, ns)
    w_idx = lambda i, itt, its, itf, te, nv, tok, dst: (layer, te[itt[i]], its[i], 0)
    grid_spec = pltpu.PrefetchScalarGridSpec(
        num_scalar_prefetch=7,
        grid=(it_tile.shape[0],),
        in_specs=[pl.BlockSpec(memory_space=pl.ANY),
                  pl.BlockSpec((tm, LANES), lambda i, itt, its, itf, te, nv, tok, dst: (itt[i], 0)),
                  pl.BlockSpec((None, None, dk, FF), w_idx),
                  pl.BlockSpec((None, None, dk, FF), w_idx),
                  pl.BlockSpec((None, None, fk, D), w_idx)],
        out_specs=pl.BlockSpec(memory_space=pl.ANY),
        scratch_shapes=[pltpu.VMEM((tm, D), F32), pltpu.VMEM((tm, D), BF16), pltpu.VMEM((tm, D), F32),
                        pltpu.VMEM((D, FF), BF16), pltpu.VMEM((D, FF), BF16), pltpu.VMEM((FF, D), BF16),
                        pltpu.SemaphoreType.DMA(()), pltpu.SemaphoreType.DMA(())],
    )
    return pl.pallas_call(
        functools.partial(_moe_kernel, tm=tm, n_real=TOP_K * T),
        grid_spec=grid_spec,
        out_shape=jax.ShapeDtypeStruct((TOP_K * T + 2 * tm, D), F32),
        compiler_params=_params("arbitrary"),
        name="moe_routed",
    )(it_tile, it_slice, it_flag, tile_e, n_valid, row_tok, row_dst, h2, row_w, wg, wu, wd)


def kernel(x_prompt, x_sample, state_hgrn, state_mlstm_C, state_mlstm_n, state_mlstm_m, state_gdn, state_conv, c_prompt, c_sample, w_in, b_in, hgrn_lb_logits, hgrn_norm_g, gdn_conv_w, gdn_A_log, gdn_dt_bias, gdn_norm_g, w_branch, w_merge, b_merge, w_out, ln_mix_g, ln_mix_b, w_ada, b_ada, w_router, b_router, w_exp_gate, w_exp_up, w_exp_down, ln_ffn_g, ln_ffn_b):
    depth = w_in.shape[0]
    Bp, Lp, D = x_prompt.shape
    Bs, Ls, _ = x_sample.shape
    Tp, Ts = Bp * Lp, Bs * Ls
    T = Tp + Ts
    alpha = (2 * depth) ** 0.25

    def tok_specs(tb):
        assert Lp % tb == 0 and tb % Ls == 0 and Ts % tb == 0
        npb, spb = Tp // tb, tb // Ls
        mp, ms = _mod_specs(npb, Bs, Lp // tb, Bp, spb, D)
        xp_s, xs_s = _group_specs(npb, tb, D)
        return dict(npb=npb, spb=spb, mp=mp, ms=ms, xp=xp_s, xs=xs_s, grid=(T // tb,),
                    tok=pl.BlockSpec((tb, D), lambda i: (i, 0)),
                    tok_sm=pl.BlockSpec((tb, D), lambda i: (
                        jnp.where(i < npb, _step_major_block(jnp.minimum(i, npb - 1), tb, Bp, Lp), i), 0)),
                    tok_hi=pl.BlockSpec((tb, D), lambda i: (T // tb + i, 0)),
                    lanes=pl.BlockSpec((tb, LANES), lambda i: (i, 0)))

    ts = tok_specs(TOK_BLOCK)
    tr = tok_specs(ROUTE_BLOCK)
    n_ab = 8 * WIDTH
    g0, c0 = n_ab, n_ab + 2 * N_HEADS
    c1 = c0 + 4 * WIDTH

    nc = Bp + Bs
    nc_pad = -(-nc // 16) * 16
    c_all = jnp.pad(jnp.concatenate([c_sample, c_prompt], axis=0), ((0, nc_pad - nc), (0, 0)))
    c_act = (c_all * jax.nn.sigmoid(c_all)).astype(BF16)

    wr = jnp.pad(w_router.astype(F32), ((0, 0), (0, LANES - N_EXPERTS)))
    br = jnp.pad(b_router.astype(F32), ((0, LANES - N_EXPERTS),)).reshape(1, LANES)
    vec_spec = pl.BlockSpec((None, 1, D), lambda i: (0, 0, 0))
    vec = lambda a, l: a[l].reshape(1, 1, D)
    zero_bias = jnp.zeros((1, D), F32)

    w_in_t = jnp.swapaxes(w_in, 1, 2)
    tn = min(MM_TN, D)
    tm = MM_TM if T % MM_TM == 0 else TOK_BLOCK
    xp, xs = x_prompt.reshape(Tp, D), x_sample.reshape(Ts, D)
    x_shapes = [jax.ShapeDtypeStruct((Tp, D), F32), jax.ShapeDtypeStruct((Ts, D), F32)]
    def state_bufs(B):
        z = lambda *s: jnp.zeros((depth, B) + s, F32)
        return (z(*_STATE_DIMS), [z(*_STATE_DIMS), z(*_VEC_DIMS), z(*_VEC_DIMS)],
                [z(*_STATE_DIMS), z(CONV_W - 1, 3 * WIDTH)])

    hg_p, ml_p, gd_p = state_bufs(Bp)
    hg_s, ml_s, gd_s = state_bufs(Bs)
    mlstm_n_in = state_mlstm_n.reshape(depth, Bs, N_HEADS, 1, HEAD_DIM)
    mlstm_m_in = jnp.broadcast_to(state_mlstm_m[..., None, None], (depth, Bs, N_HEADS, 1, HEAD_DIM))
    for l in range(depth):
        mod = _matmul(c_act, w_ada, (None, D, 1024), lambda n: (l, 0, n),
                      b_ada.reshape(depth, 1, 6 * D), (None, 1, 1024), lambda n: (l, 0, n),
                      6 * D, F32, tm=nc_pad, tn=1024, name="ada").reshape(nc_pad, 6, D)

        h = pl.pallas_call(
            functools.partial(_modulate_kernel, npb=ts["npb"], spb=ts["spb"], sh_idx=0, sc_idx=1),
            grid=ts["grid"],
            in_specs=[ts["xp"], ts["xs"], ts["mp"], ts["ms"]],
            out_specs=ts["tok_sm"],
            out_shape=jax.ShapeDtypeStruct((T, D), BF16),
            compiler_params=_params("arbitrary"),
            name="modulate",
        )(xp, xs, mod, mod)

        b_in3 = b_in.reshape(depth, 1, -1)
        proj_ab = _matmul(h, w_in_t, (None, tn, D), lambda n: (l, n, 0),
                          b_in3, (None, 1, tn), lambda n: (l, 0, n),
                          n_ab, F32, tm=tm, tn=tn, name="proj_ab", w_transposed=True)
        proj_c = _matmul(h, w_in_t[l, c0:c1, :], (tn, D), lambda n: (n, 0),
                         b_in[l, c0:c1].reshape(1, -1), (1, tn), lambda n: (0, n),
                         4 * WIDTH, F32, tm=tm, tn=tn, name="proj_c", w_transposed=True)
        w_g = jnp.pad(jnp.concatenate([w_in_t[l, g0:c0, :], w_in_t[l, c1:, :]], axis=0),
                      ((0, LANES - 4 * N_HEADS), (0, 0)))
        b_g = jnp.pad(jnp.concatenate([b_in[l, g0:c0], b_in[l, c1:]]), ((0, LANES - 4 * N_HEADS),))
        gates = _matmul(h, w_g, (LANES, D), lambda n: (n, 0), b_g.reshape(1, LANES), (1, LANES),
                        lambda n: (0, n), LANES, F32, tm=tm, tn=LANES, name="proj_gates", w_transposed=True)
        gate = _matmul(h, w_merge, (None, None, D, tn), lambda n: (l, n // (D // tn), 0, n % (D // tn)),
                       b_merge.reshape(depth, 3, 1, D), (None, None, 1, tn),
                       lambda n: (l, n // (D // tn), 0, n % (D // tn)),
                       3 * D, BF16, tm=tm, tn=tn, name="merge_gates", act="sigmoid")

        conv_w_t = gdn_conv_w[l].T.astype(F32)
        y_init = jnp.zeros((T, WIDTH), F32)
        ya, hg_p = _hgrn(proj_ab, 0, 0, Bp, Lp, hgrn_lb_logits, hgrn_norm_g[l], l, None, (y_init, hg_p))
        ya, hg_s = _hgrn(proj_ab, 0, Tp, Bs, Ls, hgrn_lb_logits, hgrn_norm_g[l], l, state_hgrn, (ya, hg_s))
        yb, *ml_p = _mlstm(proj_ab, 4, gates, 0, Bp, Lp, l, None, (y_init, *ml_p))
        yb, *ml_s = _mlstm(proj_ab, 4, gates, Tp, Bs, Ls, l, (state_mlstm_C, mlstm_n_in, mlstm_m_in),
                           (yb, *ml_s))
        yc, *gd_p = _gdn(proj_c, 0, gates, 0, Bp, Lp, conv_w_t, gdn_A_log[l], gdn_dt_bias[l],
                         gdn_norm_g[l], l, None, (y_init, *gd_p))
        yc, *gd_s = _gdn(proj_c, 0, gates, Tp, Bs, Ls, conv_w_t, gdn_A_log[l], gdn_dt_bias[l],
                         gdn_norm_g[l], l, (state_gdn, state_conv), (yc, *gd_s))

        merged = _merge((ya, yb, yc), gate, w_branch, l)
        mix = _matmul(merged, w_out, (None, D, tn), lambda n: (l, 0, n),
                      zero_bias, (1, tn), lambda n: (0, n), D, F32, tm=tm, tn=tn, name="out_proj")

        xp, xs, h2, rinfo = pl.pallas_call(
            functools.partial(_post_route_kernel, npb=tr["npb"], spb=tr["spb"], alpha=alpha),
            grid=tr["grid"],
            in_specs=[tr["xp"], tr["xs"], tr["tok_sm"], tr["mp"], tr["ms"], vec_spec, vec_spec,
                      pl.BlockSpec((D, LANES), lambda i: (0, 0)), pl.BlockSpec((1, LANES), lambda i: (0, 0))],
            out_specs=[tr["xp"], tr["xs"], tr["tok"], tr["lanes"]],
            out_shape=[x_shapes[0], x_shapes[1], jax.ShapeDtypeStruct((T, D), F32),
                       jax.ShapeDtypeStruct((T, LANES), F32)],
            compiler_params=_params("arbitrary"),
            name="post_route",
        )(xp, xs, mix, mod, mod, vec(ln_mix_g, l), vec(ln_mix_b, l), wr, br)

        ffn = _moe(h2, rinfo, w_exp_gate, w_exp_up, w_exp_down, l)
        xp, xs = pl.pallas_call(
            functools.partial(_post_final_kernel, npb=tr["npb"], spb=tr["spb"], alpha=alpha),
            grid=tr["grid"],
            in_specs=[tr["xp"], tr["xs"], tr["tok"], tr["tok_hi"], tr["mp"], tr["ms"], vec_spec, vec_spec],
            out_specs=[tr["xp"], tr["xs"]],
            out_shape=x_shapes,
            compiler_params=_params("arbitrary"),
            name="post_final",
        )(xp, xs, ffn, ffn, mod, mod, vec(ln_ffn_g, l), vec(ln_ffn_b, l))

    as_vec = lambda n, B: n.reshape(depth, B, N_HEADS, HEAD_DIM)
    as_scalar = lambda m: m[:, :, :, 0, 0]
    return (xp.reshape(Bp, Lp, D), xs.reshape(Bs, Ls, D), hg_p, hg_s, ml_p[0], ml_s[0],
            as_vec(ml_p[1], Bp), as_vec(ml_s[1], Bs), as_scalar(ml_p[2]), as_scalar(ml_s[2]),
            gd_p[0], gd_s[0], gd_p[1], gd_s[1])
```

```python
import functools
import math

import numpy as np
import jax
import jax.numpy as jnp
from jax import lax
from jax.experimental import pallas as pl
from jax.experimental.pallas import tpu as pltpu

F32 = jnp.float32
BF16 = jnp.bfloat16

HEAD_DIM = 128
N_HEADS = 8
WIDTH = N_HEADS * HEAD_DIM
CHUNK = 64
CONV_W = 4
N_EXPERTS = 16
N_GROUPS = 4
EXPERTS_PER_GROUP = N_EXPERTS // N_GROUPS
TOP_K = 2
LN_EPS = 1e-5
NORM_EPS = 1e-6
LB_TINY = 1e-20
NEG_BIG = -1e30
LANES = 128
VMEM_LIMIT = 56 * 1024 * 1024
TOK_BLOCK = 256
ROUTE_BLOCK = 128
MM_TM = 1024
MM_TN = 512
SEQ_ROWS_PER_STEP = 256
SHORT_SEQS_PER_STEP = 8
LONG_SEQS_PER_STEP = 2
MOE_TM = 256
MOE_SLICES = 8
HEADS = tuple(slice(h * HEAD_DIM, (h + 1) * HEAD_DIM) for h in range(N_HEADS))


def _params(*sem):
    return pltpu.CompilerParams(dimension_semantics=sem, vmem_limit_bytes=VMEM_LIMIT)


def _dot(a, b):
    return jnp.dot(a.astype(BF16), b.astype(BF16), preferred_element_type=F32)


def _dot_nt(a, b):
    return lax.dot_general(a.astype(BF16), b.astype(BF16), (((1,), (1,)), ((), ())),
                           preferred_element_type=F32)


def _dot_tn(a, b):
    return lax.dot_general(a.astype(BF16), b.astype(BF16), (((0,), (0,)), ((), ())),
                           preferred_element_type=F32)


def _split2(x):
    x1 = x.astype(BF16)
    return x1, (x - x1.astype(F32)).astype(BF16)


def _dot_exact_l(a_bf, x):
    x1, x2 = _split2(x)
    d = lambda p: jnp.dot(a_bf, p, preferred_element_type=F32)
    return d(x1) + d(x2)


def _dot_exact_r(x, b_bf):
    x1, x2 = _split2(x)
    d = lambda p: jnp.dot(p, b_bf, preferred_element_type=F32)
    return d(x1) + d(x2)


def _dot3(a, b):
    a1, a2 = _split2(a)
    b1, b2 = _split2(b)
    d = lambda p, q: jnp.dot(p, q, preferred_element_type=F32)
    return d(a1, b1) + (d(a1, b2) + d(a2, b1))


def _sigmoid(x):
    return jax.nn.sigmoid(x)


def _silu(x):
    return x * jax.nn.sigmoid(x)


def _log_sigmoid(x):
    return jnp.minimum(x, 0.0) - jnp.log1p(jnp.exp(-jnp.abs(x)))


def _softplus(x):
    return jnp.maximum(x, 0.0) + jnp.log1p(jnp.exp(-jnp.abs(x)))


def _logaddexp(a, b):
    return jnp.maximum(a, b) + jnp.log1p(jnp.exp(-jnp.abs(a - b)))


def _tri_masks(cn):
    r = lax.broadcasted_iota(jnp.int32, (cn, cn), 0)
    c = lax.broadcasted_iota(jnp.int32, (cn, cn), 1)
    return r >= c, r > c, r == c


def _rms_gate(o, ng, gate):
    return o * lax.rsqrt(jnp.mean(o * o, axis=-1, keepdims=True) + NORM_EPS) * ng * gate


def _mm_kernel(x_ref, w_ref, b_ref, o_ref, wbf_ref, *, act, w_transposed):
    @pl.when(pl.program_id(1) == 0)
    def _():
        if w_transposed:
            tn = w_ref.shape[0]
            step = min(tn, 256)
            for c in range(0, tn, step):
                wbf_ref[:, c:c + step] = w_ref[c:c + step, :].T.astype(BF16)
        else:
            wbf_ref[...] = w_ref[...].astype(BF16)

    acc = jnp.dot(x_ref[...], wbf_ref[...], preferred_element_type=F32) + b_ref[...]
    if act == "sigmoid":
        acc = _sigmoid(acc)
    o_ref[...] = acc.astype(o_ref.dtype)


def _matmul(x, w, w_block, w_index, b, b_block, b_index, n_out, out_dtype, *, tm, tn, name, act=None,
            w_transposed=False):
    M, K = x.shape
    tm = min(tm, M)
    assert M % tm == 0 and n_out % tn == 0, (M, tm, n_out, tn)
    return pl.pallas_call(
        functools.partial(_mm_kernel, act=act, w_transposed=w_transposed),
        grid=(n_out // tn, M // tm),
        in_specs=[pl.BlockSpec((tm, K), lambda n, m: (m, 0)),
                  pl.BlockSpec(w_block, lambda n, m: w_index(n)),
                  pl.BlockSpec(b_block, lambda n, m: b_index(n))],
        out_specs=pl.BlockSpec((tm, tn), lambda n, m: (m, n)),
        out_shape=jax.ShapeDtypeStruct((M, n_out), out_dtype),
        scratch_shapes=[pltpu.VMEM((K, tn), BF16)],
        compiler_params=_params("arbitrary", "arbitrary"),
        name=name,
    )(x, w, b)


def _with_mod(npb, mp_ref, ms_ref, body):
    i = pl.program_id(0)

    @pl.when(i < npb)
    def _():
        body(mp_ref[...], 0)

    @pl.when(i >= npb)
    def _():
        body(ms_ref[...], 1)


def _group_specs(npb, tb, D):
    xp = pl.BlockSpec((tb, D), lambda i: (jnp.minimum(i, npb - 1), 0))
    xs = pl.BlockSpec((tb, D), lambda i: (jnp.maximum(i - npb, 0), 0))
    return xp, xs


def _mod_specs(npb, n_sample_seq, blocks_per_prompt_seq, n_prompt_seq, spb, D):
    mp = pl.BlockSpec((1, 6, D), lambda i: (n_sample_seq + jnp.minimum(i // blocks_per_prompt_seq,
                                                                        n_prompt_seq - 1), 0, 0))
    ms = pl.BlockSpec((spb, 6, D), lambda i: (jnp.maximum(i - npb, 0), 0, 0))
    return mp, ms


def _modulate_kernel(xp_ref, xs_ref, mp_ref, ms_ref, o_ref, *, npb, spb, sh_idx, sc_idx):
    tb, D = xp_ref.shape

    def body(m, group):
        x = (xp_ref, xs_ref)[group][...].reshape(spb, tb // spb, D)
        h = x * (1.0 + m[:, sc_idx:sc_idx + 1, :]) + m[:, sh_idx:sh_idx + 1, :]
        o_ref[...] = h.reshape(tb, D).astype(o_ref.dtype)

    _with_mod(npb, mp_ref, ms_ref, body)


def _layer_norm(z, g, b):
    mu = jnp.mean(z, axis=-1, keepdims=True)
    zc = z - mu
    var = jnp.mean(zc * zc, axis=-1, keepdims=True)
    return zc * lax.rsqrt(var + LN_EPS) * g + b


def _route(logits):
    lane = lax.broadcasted_iota(jnp.int32, logits.shape, 1)
    valid = lane < N_EXPERTS
    lg = jnp.where(valid, logits, NEG_BIG)
    ex = jnp.where(valid, jnp.exp(lg - jnp.max(lg, axis=-1, keepdims=True)), 0.0)
    probs = ex / jnp.sum(ex, axis=-1, keepdims=True)
    big = LANES
    best = None
    for g in range(N_GROUPS):
        in_g = (lane >= g * EXPERTS_PER_GROUP) & (lane < (g + 1) * EXPERTS_PER_GROUP)
        m1 = jnp.max(jnp.where(in_g, probs, -1.0), axis=-1, keepdims=True)
        i1 = jnp.min(jnp.where(in_g & (probs == m1), lane, big), axis=-1, keepdims=True)
        rest = in_g & (lane != i1)
        m2 = jnp.max(jnp.where(rest, probs, -1.0), axis=-1, keepdims=True)
        i2 = jnp.min(jnp.where(rest & (probs == m2), lane, big), axis=-1, keepdims=True)
        cand = (m1 + m2, i1, i2, m1, m2)
        if best is None:
            best = cand
        else:
            better = cand[0] > best[0]
            best = tuple(jnp.where(better, c, o) for c, o in zip(cand, best))
    _, i1, i2, m1, m2 = best
    tot = m1 + m2
    return (jnp.where(lane == 0, i1.astype(F32), 0.0) + jnp.where(lane == 1, i2.astype(F32), 0.0)
            + jnp.where(lane == 2, m1 / tot, 0.0) + jnp.where(lane == 3, m2 / tot, 0.0))


def _post_route_kernel(xp_ref, xs_ref, y_ref, mp_ref, ms_ref, g_ref, b_ref, wr_ref, br_ref,
                       xop_ref, xos_ref, h_ref, r_ref, *, npb, spb, alpha):
    tb, D = xp_ref.shape

    def body(m, group):
        x = (xp_ref, xs_ref)[group][...].reshape(spb, tb // spb, D)
        y = y_ref[...].reshape(spb, tb // spb, D)
        xn = _layer_norm(alpha * x + m[:, 2:3, :] * y, g_ref[...], b_ref[...])
        (xop_ref, xos_ref)[group][...] = xn.reshape(tb, D)
        h2 = (xn * (1.0 + m[:, 4:5, :]) + m[:, 3:4, :]).reshape(tb, D)
        h_ref[...] = h2
        r_ref[...] = _route(_dot3(h2, wr_ref[...]) + br_ref[...])

    _with_mod(npb, mp_ref, ms_ref, body)


def _post_final_kernel(xp_ref, xs_ref, y0_ref, y1_ref, mp_ref, ms_ref, g_ref, b_ref,
                       xop_ref, xos_ref, *, npb, spb, alpha):
    tb, D = xp_ref.shape

    def body(m, group):
        x = (xp_ref, xs_ref)[group][...].reshape(spb, tb // spb, D)
        y = (y0_ref[...] + y1_ref[...]).reshape(spb, tb // spb, D)
        xn = _layer_norm(alpha * x + m[:, 5:6, :] * y, g_ref[...], b_ref[...])
        (xop_ref, xos_ref)[group][...] = xn.reshape(tb, D)

    _with_mod(npb, mp_ref, ms_ref, body)


def _seq_blocking(L):
    cn = math.gcd(L, CHUNK)
    lb = min(L, SEQ_ROWS_PER_STEP)
    assert L % lb == 0 and lb % cn == 0
    return cn, lb, L // lb


def _step_major_block(i, tb, B, L):
    _, lb, _ = _seq_blocking(L)
    per_seq, per_step = L // tb, lb // tb
    b, rem = i // per_seq, i % per_seq
    return ((rem // per_step) * B + b) * per_step + rem % per_step


def _run_chunks(chunk, nchunk):
    if nchunk == 1:
        chunk(0, 0)
    else:
        lax.fori_loop(0, nchunk, chunk, 0)


def _hgrn_tables(cn):
    nlev = int(math.log2(cn))
    assert 1 << nlev == cn
    r = np.arange(cn)
    mats = [np.tril(np.ones((cn, cn)))]
    masks = [np.eye(cn)]
    j = r[None, :]
    for lev in range(nlev):
        s = cn >> (lev + 1)
        blk = 2 * s
        pos = r % blk
        ref = r - pos + s - 1
        second = pos >= s
        m_second = (j > ref[:, None]) & (j <= r[:, None])
        m_first = (j > r[:, None]) & (j <= ref[:, None])
        mats.append(np.where(second[:, None], m_second, m_first))
        same = (r[:, None] // blk) == (r[None, :] // blk)
        masks.append(same & second[:, None] & (~second)[None, :])
    return (np.concatenate(mats, 0).astype(np.float32), np.stack(masks).astype(np.float32))


def _hgrn_kernel(*refs, layer, cn, nchunk, nseq, has_state, n_alias):
    refs = list(refs)
    lbl_ref, ng_ref, mall_ref, masks_ref, q_ref, f_ref, i_ref, g_ref = refs[:8]
    pos = 8
    s0_ref = None
    if has_state:
        s0_ref = refs[pos]
        pos += 1
    pos += n_alias
    y_ref, s_ref, st_ref = refs[pos:pos + 3]
    j = pl.program_id(1)
    nlev = masks_ref.shape[0] - 1

    seqs = range(nseq)
    units = [(sq, h) for sq in seqs for h in range(N_HEADS)]
    seq_rows = q_ref.shape[0] // nseq

    @pl.when(j == 0)
    def _():
        for n, (sq, h) in enumerate(units):
            if has_state:
                st_ref[n] = s0_ref[sq, h].T
            else:
                st_ref[n] = jnp.zeros((HEAD_DIM, HEAD_DIM), F32)

    logits = lbl_ref[...]
    e = jnp.exp(logits - jnp.max(logits, axis=0, keepdims=True))
    p = e / jnp.sum(e, axis=0, keepdims=True)
    lb = jnp.zeros((1, WIDTH), F32)
    for l in range(1, layer + 1):
        lb = lb + p[l:l + 1, :]
    log_lb = jnp.log(jnp.maximum(lb, LB_TINY))
    log1m_lb = jnp.log1p(-lb)
    one_m_lb = 1.0 - lb
    ng = ng_ref[...]
    mall = mall_ref[...]

    def chunk(c, carry):
        r0 = pl.multiple_of(c * cn, cn)
        rows = [pl.ds(r0 + sq * seq_rows, cn) for sq in seqs]
        q = [_silu(q_ref[rw, :]) for rw in rows]
        fr = [f_ref[rw, :] for rw in rows]
        v = [i_ref[rw, :].astype(BF16) for rw in rows]
        log_f = [_logaddexp(log_lb, log1m_lb + _log_sigmoid(x)) for x in fr]
        k = [one_m_lb * _sigmoid(-x) for x in fr]
        dall = [_dot_exact_l(mall, x) for x in log_f]
        b = [x[:cn] for x in dall]
        qb = [x.astype(BF16) for x in q]
        kb = [x.astype(BF16) for x in k]
        att = [masks_ref[0] * _dot_nt(qb[sq][:, HEADS[h]], kb[sq][:, HEADS[h]]) for sq, h in units]
        for lev in range(1, nlev + 1):
            ex = [jnp.exp(x[lev * cn:(lev + 1) * cn]) for x in dall]
            qe = [(x * e).astype(BF16) for x, e in zip(q, ex)]
            ke = [(x * e).astype(BF16) for x, e in zip(k, ex)]
            att = [a + masks_ref[lev] * _dot_nt(qe[sq][:, HEADS[h]], ke[sq][:, HEADS[h]])
                   for a, (sq, h) in zip(att, units)]
        st = [st_ref[n] for n in range(len(units))]
        qd = [(x * jnp.exp(bb)).astype(BF16) for x, bb in zip(q, b)]
        o = [_dot(a, v[sq][:, HEADS[h]]) + _dot_nt(qd[sq][:, HEADS[h]], s)
             for a, (sq, h), s in zip(att, units, st)]
        b_last = [bb[cn - 1:cn] for bb in b]
        k_dec = [(x * jnp.exp(bl - bb)).astype(BF16) for x, bl, bb in zip(k, b_last, b)]
        dec = [jnp.exp(bl) for bl in b_last]
        for n, (sq, h) in enumerate(units):
            cs = HEADS[h]
            st_ref[n] = st[n] * dec[sq][:, cs] + _dot_tn(v[sq][:, cs], k_dec[sq][:, cs])
        gate = [_sigmoid(g_ref[rw, :]) for rw in rows]
        for n, (sq, h) in enumerate(units):
            cs = HEADS[h]
            y_ref[rows[sq], cs] = _rms_gate(o[n], ng[:, cs], gate[sq][:, cs])
        return carry

    _run_chunks(chunk, nchunk)

    @pl.when(j == pl.num_programs(1) - 1)
    def _():
        for n, (sq, h) in enumerate(units):
            s_ref[sq, h] = st_ref[n].T


def _mixer_specs(T_total, row0, B, L):
    cn, LB, ncb = _seq_blocking(L)
    nseq = math.gcd(B, SHORT_SEQS_PER_STEP if LB <= CHUNK else LONG_SEQS_PER_STEP)
    rows = nseq * LB
    assert row0 % rows == 0
    rb0 = row0 // rows
    blk = lambda g, j: j * (B // nseq) + g
    col = lambda cidx: pl.BlockSpec((rows, WIDTH), lambda g, j: (rb0 + blk(g, j), cidx))
    yspec = pl.BlockSpec((rows, WIDTH), lambda g, j: (rb0 + blk(g, j), 0))
    gspec = pl.BlockSpec((rows, LANES), lambda g, j: (rb0 + blk(g, j), 0))
    gtspec = pl.BlockSpec((rows // cn, 4 * N_HEADS, cn), lambda g, j: (blk(g, j), 0, 0))
    return cn, LB, ncb, nseq, col, yspec, gspec, gtspec


def _state_spec(layer, trailing, nseq):
    return pl.BlockSpec((None, nseq) + trailing, lambda b, j: (layer, b) + (0,) * len(trailing))


_STATE_DIMS = (N_HEADS, HEAD_DIM, HEAD_DIM)
_VEC_DIMS = (N_HEADS, 1, HEAD_DIM)


def _full(a):
    return pl.BlockSpec(a.shape, lambda b, j: (0,) * a.ndim)


def _alias_outputs(args, in_specs, prevs):
    aliases = {}
    for k, prev in enumerate(prevs):
        args.append(prev)
        in_specs.append(pl.BlockSpec(memory_space=pl.ANY))
        aliases[len(args) - 1] = k
    return aliases


def _shapes(prevs):
    return [jax.ShapeDtypeStruct(p.shape, p.dtype) for p in prevs]


def _hgrn(proj, a_col0, row0, B, L, lb_logits, norm_g, layer, state, prevs):
    cn, LB, ncb, nseq, col, yspec, _, _ = _mixer_specs(proj.shape[0], row0, B, L)
    mall, masks = _hgrn_tables(cn)
    mall = jnp.asarray(mall, BF16)
    masks = jnp.asarray(masks)
    ng = jnp.tile(norm_g.reshape(1, HEAD_DIM).astype(F32), (1, N_HEADS))
    sspec = _state_spec(layer, _STATE_DIMS, nseq)
    args = [lb_logits, ng, mall, masks, proj, proj, proj, proj]
    in_specs = [_full(lb_logits), _full(ng), _full(mall), _full(masks)] + [col(a_col0 + t) for t in range(4)]
    if state is not None:
        args.append(state)
        in_specs.append(sspec)
    aliases = _alias_outputs(args, in_specs, prevs)
    return pl.pallas_call(
        functools.partial(_hgrn_kernel, layer=layer, cn=cn, nchunk=LB // cn, nseq=nseq,
                          has_state=state is not None, n_alias=len(prevs)),
        grid=(B // nseq, ncb),
        in_specs=in_specs,
        out_specs=[yspec, sspec],
        out_shape=_shapes(prevs),
        scratch_shapes=[pltpu.VMEM((nseq * N_HEADS, HEAD_DIM, HEAD_DIM), F32)],
        input_output_aliases=aliases,
        compiler_params=_params("arbitrary", "arbitrary"),
        name="hgrn2",
    )(*args)


def _head_cols(mat, base):
    return [mat[:, base + h:base + h + 1] for h in range(N_HEADS)]


def _head_rows(mat, base):
    return [mat[base + h:base + h + 1, :] for h in range(N_HEADS)]


def _mlstm_kernel(*refs, cn, nchunk, nseq, has_state, n_alias):
    refs = list(refs)
    gt_ref, gtt_ref, q_ref, k_ref, v_ref, o_ref = refs[:6]
    pos = 6
    if has_state:
        c0_ref, n0_ref, m0_ref = refs[pos:pos + 3]
        pos += 3
    pos += n_alias
    y_ref, c_ref, n_ref, m_ref = refs[pos:pos + 4]
    j = pl.program_id(1)

    @pl.when(j == 0)
    def _():
        if has_state:
            c_ref[...] = c0_ref[...]
            n_ref[...] = n0_ref[...]
            m_ref[...] = m0_ref[...]
        else:
            c_ref[...] = jnp.zeros(c_ref.shape, F32)
            n_ref[...] = jnp.zeros(n_ref.shape, F32)
            m_ref[...] = jnp.zeros(m_ref.shape, F32)

    incl, _, _ = _tri_masks(cn)
    tril = incl.astype(BF16)
    triu = jnp.logical_not(_tri_masks(cn)[1]).astype(BF16)

    seqs = range(nseq)
    units = [(sq, h) for sq in seqs for h in range(N_HEADS)]
    lb = q_ref.shape[0] // nseq

    def chunk(c, carry):
        r0 = pl.multiple_of(c * cn, cn)
        rows = [pl.ds(r0 + sq * lb, cn) for sq in seqs]
        gates = [gt_ref[rw, :] for rw in rows]
        gates_r = [gtt_ref[sq * nchunk + c] for sq in seqs]
        f_c = [_dot_exact_l(tril, _log_sigmoid(g)) for g in gates]
        f_r = [_dot_exact_r(_log_sigmoid(g), triu) for g in gates_r]
        Fc = [x for sq in seqs for x in _head_cols(f_c[sq], N_HEADS)]
        ic = [x for sq in seqs for x in _head_cols(gates[sq], 0)]
        Fr = [x for sq in seqs for x in _head_rows(f_r[sq], N_HEADS)]
        ir = [x for sq in seqs for x in _head_rows(gates_r[sq], 0)]
        q = [q_ref[rows[sq], HEADS[h]].astype(BF16) for sq, h in units]
        k = [k_ref[rows[sq], HEADS[h]] * (HEAD_DIM ** -0.5) for sq, h in units]
        kb = [x.astype(BF16) for x in k]
        v = [v_ref[rows[sq], HEADS[h]].astype(BF16) for sq, h in units]
        C = [c_ref[sq, h] for sq, h in units]
        n = [n_ref[sq, h] for sq, h in units]
        m = [m_ref[sq, h][:, 0:1] for sq, h in units]
        qk = [_dot_nt(a, b) for a, b in zip(q, kb)]
        qc = [_dot(a, b) for a, b in zip(q, C)]
        raw = [a - b + d for a, b, d in zip(Fc, Fr, ir)]
        inter = [a + b for a, b in zip(Fc, m)]
        m_t = [jnp.maximum(a, jnp.max(jnp.where(incl, r, NEG_BIG), axis=-1, keepdims=True))
               for a, r in zip(inter, raw)]
        w_inter = [jnp.exp(a - b) for a, b in zip(inter, m_t)]
        s = [jnp.where(incl, a * jnp.exp(jnp.where(incl, r - mt, 0.0)), 0.0)
             for a, r, mt in zip(qk, raw, m_t)]
        num = [_dot(a, b) + w * d for a, b, w, d in zip(s, v, w_inter, qc)]
        den = [jnp.sum(a, axis=-1, keepdims=True)
               + w * jnp.sum(q_ref[rows[sq], HEADS[h]] * nn, axis=-1, keepdims=True)
               for a, w, (sq, h), nn in zip(s, w_inter, units, n)]
        hh = [a / jnp.maximum(jnp.abs(d), jnp.exp(-mt)) for a, d, mt in zip(num, den, m_t)]
        for u, (sq, h) in enumerate(units):
            m_new = m_t[u][cn - 1:cn]
            w_c = jnp.exp(inter[u][cn - 1:cn] - m_new)
            w_s = jnp.exp(Fc[u][cn - 1:cn] - Fc[u] + ic[u] - m_new)
            ks = k[u] * w_s
            c_ref[sq, h] = w_c * C[u] + _dot_tn(ks, v[u])
            n_ref[sq, h] = w_c * n[u] + jnp.sum(ks, axis=0, keepdims=True)
            m_ref[sq, h] = jnp.broadcast_to(m_new, (1, HEAD_DIM))
            y_ref[rows[sq], HEADS[h]] = _sigmoid(o_ref[rows[sq], HEADS[h]]) * hh[u]
        return carry

    _run_chunks(chunk, nchunk)


def _gates_t(gates, row0, B, L, cn):
    g = gates[row0:row0 + B * L, :4 * N_HEADS]
    return g.reshape(B * L // cn, cn, 4 * N_HEADS).transpose(0, 2, 1)


def _mlstm(proj, b_col0, gates, row0, B, L, layer, state, prevs):
    cn, LB, ncb, nseq, col, yspec, gspec, gtspec = _mixer_specs(proj.shape[0], row0, B, L)
    cspec = _state_spec(layer, _STATE_DIMS, nseq)
    nspec = _state_spec(layer, _VEC_DIMS, nseq)
    gt = _gates_t(gates, row0, B, L, cn)
    args = [gates, gt, proj, proj, proj, proj]
    in_specs = [gspec, gtspec] + [col(b_col0 + t) for t in range(4)]
    if state is not None:
        args += list(state)
        in_specs += [cspec, nspec, nspec]
    aliases = _alias_outputs(args, in_specs, prevs)
    return pl.pallas_call(
        functools.partial(_mlstm_kernel, cn=cn, nchunk=LB // cn, nseq=nseq, has_state=state is not None,
                          n_alias=len(prevs)),
        grid=(B // nseq, ncb),
        in_specs=in_specs,
        out_specs=[yspec, cspec, nspec, nspec],
        out_shape=_shapes(prevs),
        input_output_aliases=aliases,
        compiler_params=_params("arbitrary", "arbitrary"),
        name="mlstm",
    )(*args)


_CONV_PAD = 8


def _gdn_kernel(*refs, cn, nchunk, nseq, has_state, n_alias):
    refs = list(refs)
    (gt_ref, gtt_ref, cw_ref, al_ref, dtb_ref, alc_ref, dtbc_ref, ng_ref,
     xq_ref, xk_ref, xv_ref, g_ref) = refs[:12]
    pos = 12
    if has_state:
        s0_ref, b0_ref = refs[pos:pos + 2]
        pos += 2
    pos += n_alias
    y_ref, s_ref, buf_ref, xp_ref, cv_ref = refs[pos:pos + 5]
    j = pl.program_id(1)
    lb = xq_ref.shape[0] // nseq
    hist = CONV_W - 1
    h0 = _CONV_PAD - hist
    seqs = range(nseq)
    units = [(sq, h) for sq in seqs for h in range(N_HEADS)]

    @pl.when(j == 0)
    def _():
        if has_state:
            s_ref[...] = s0_ref[...]
        else:
            s_ref[...] = jnp.zeros(s_ref.shape, F32)

    for sq in seqs:
        blk = slice(sq * lb, (sq + 1) * lb)

        @pl.when(j == 0)
        def _():
            xp_ref[h0:_CONV_PAD, :] = b0_ref[sq] if has_state else jnp.zeros((hist, 3 * WIDTH), F32)

        @pl.when(j > 0)
        def _():
            xp_ref[h0:_CONV_PAD, :] = buf_ref[sq]
        xp_ref[_CONV_PAD:_CONV_PAD + lb, 0:WIDTH] = xq_ref[blk, :]
        xp_ref[_CONV_PAD:_CONV_PAD + lb, WIDTH:2 * WIDTH] = xk_ref[blk, :]
        xp_ref[_CONV_PAD:_CONV_PAD + lb, 2 * WIDTH:3 * WIDTH] = xv_ref[blk, :]
        acc = xp_ref[h0:h0 + lb, :] * cw_ref[0:1, :]
        for t in range(1, CONV_W):
            acc = acc + xp_ref[h0 + t:h0 + t + lb, :] * cw_ref[t:t + 1, :]
        cv_ref[blk, :] = _silu(acc)
        buf_ref[sq] = xp_ref[_CONV_PAD + lb - hist:_CONV_PAD + lb, :]

    incl, strict, eye = _tri_masks(cn)
    tril = incl.astype(BF16)
    triu = jnp.logical_not(strict).astype(BF16)
    eye_f = eye.astype(F32)
    lane = lax.broadcasted_iota(jnp.int32, (1, LANES), 1)
    is_a = (lane >= 2 * N_HEADS) & (lane < 3 * N_HEADS)
    sub = lax.broadcasted_iota(jnp.int32, (4 * N_HEADS, 1), 0)
    is_a_r = (sub >= 2 * N_HEADS) & (sub < 3 * N_HEADS)
    neg_rate = -jnp.exp(al_ref[...])
    dtb = dtb_ref[...]
    neg_rate_r = -jnp.exp(alc_ref[...])
    dtb_r = dtbc_ref[...]
    ng = ng_ref[...]
    nsq = int(math.log2(cn))

    def chunk(c, carry):
        r0 = pl.multiple_of(c * cn, cn)
        rows = [pl.ds(r0 + sq * lb, cn) for sq in seqs]
        gam_c, gam_r, beta = [], [], []
        for sq in seqs:
            gates = gt_ref[rows[sq], :]
            gates_r = gtt_ref[sq * nchunk + c]
            g_c = jnp.where(is_a, neg_rate * _softplus(gates + dtb), 0.0)
            g_r = jnp.where(is_a_r, neg_rate_r * _softplus(gates_r + dtb_r), 0.0)
            gam_c += _head_cols(_dot_exact_l(tril, g_c), 2 * N_HEADS)
            gam_r += _head_rows(_dot_exact_r(g_r, triu), 2 * N_HEADS)
            beta += _head_cols(_sigmoid(gates), 3 * N_HEADS)
        q, k, v = [], [], []
        for sq, h in units:
            cs = HEADS[h]
            qr = cv_ref[rows[sq], cs]
            kr = cv_ref[rows[sq], slice(WIDTH + cs.start, WIDTH + cs.stop)]
            v.append(cv_ref[rows[sq], slice(2 * WIDTH + cs.start, 2 * WIDTH + cs.stop)])
            q.append(qr * (lax.rsqrt(jnp.sum(qr * qr, axis=-1, keepdims=True) + NORM_EPS) * (HEAD_DIM ** -0.5)))
            k.append(kr * lax.rsqrt(jnp.sum(kr * kr, axis=-1, keepdims=True) + NORM_EPS))
        kb = [x.astype(BF16) for x in k]
        S = [s_ref[sq, h] for sq, h in units]
        kk = [_dot_nt(a, a) for a in kb]
        qk = [_dot_nt(a, b) for a, b in zip(q, kb)]
        dec = [jnp.exp(jnp.where(incl, a - b, 0.0)) for a, b in zip(gam_c, gam_r)]
        pw = [jnp.where(strict, -(bt * a * d), 0.0) for bt, a, d in zip(beta, kk, dec)]
        t_inv = [eye_f + a for a in pw]
        for _ in range(nsq - 1):
            pw = [_dot3(a, a) for a in pw]
            t_inv = [t + _dot3(t, a) for t, a in zip(t_inv, pw)]
        eg = [jnp.exp(a) for a in gam_c]
        sol_v = [_dot3(t, bt * x) for t, bt, x in zip(t_inv, beta, v)]
        sol_k = [_dot3(t, (bt * e) * x) for t, bt, e, x in zip(t_inv, beta, eg, k)]
        u = [a - _dot(b, s) for a, b, s in zip(sol_v, sol_k, S)]
        o = [_dot(jnp.where(incl, a * d, 0.0), uu) + _dot(qq * e, s)
             for a, d, uu, qq, e, s in zip(qk, dec, u, q, eg, S)]
        for n, (sq, h) in enumerate(units):
            g_last = gam_c[n][cn - 1:cn]
            s_ref[sq, h] = jnp.exp(g_last) * S[n] + _dot_tn(k[n] * jnp.exp(g_last - gam_c[n]), u[n])
            y_ref[rows[sq], HEADS[h]] = _rms_gate(o[n], ng, _silu(g_ref[rows[sq], HEADS[h]]))
        return carry

    _run_chunks(chunk, nchunk)


def _gdn(proj, c_col0, gates, row0, B, L, conv_w_t, a_log, dt_bias, norm_g, layer, state, prevs):
    cn, LB, ncb, nseq, col, yspec, gspec, gtspec = _mixer_specs(proj.shape[0], row0, B, L)
    hist = CONV_W - 1
    sspec = _state_spec(layer, _STATE_DIMS, nseq)
    bspec = _state_spec(layer, (hist, 3 * WIDTH), nseq)
    a0, a1 = 2 * N_HEADS, 3 * N_HEADS
    al = jnp.zeros((1, LANES), F32).at[0, a0:a1].set(a_log.astype(F32))
    db = jnp.zeros((1, LANES), F32).at[0, a0:a1].set(dt_bias.astype(F32))
    alc = al[0, :4 * N_HEADS].reshape(4 * N_HEADS, 1)
    dbc = db[0, :4 * N_HEADS].reshape(4 * N_HEADS, 1)
    ng = norm_g.reshape(1, HEAD_DIM).astype(F32)
    gt = _gates_t(gates, row0, B, L, cn)
    args = [gates, gt, conv_w_t, al, db, alc, dbc, ng, proj, proj, proj, proj]
    in_specs = [gspec, gtspec, _full(conv_w_t), _full(al), _full(db), _full(alc), _full(dbc), _full(ng)]
    in_specs += [col(c_col0 + t) for t in range(4)]
    if state is not None:
        args += [state[0], state[1]]
        in_specs += [sspec, bspec]
    aliases = _alias_outputs(args, in_specs, prevs)
    return pl.pallas_call(
        functools.partial(_gdn_kernel, cn=cn, nchunk=LB // cn, nseq=nseq, has_state=state is not None,
                          n_alias=len(prevs)),
        grid=(B // nseq, ncb),
        in_specs=in_specs,
        out_specs=[yspec, sspec, bspec],
        out_shape=_shapes(prevs),
        scratch_shapes=[pltpu.VMEM((_CONV_PAD + LB, 3 * WIDTH), F32),
                        pltpu.VMEM((nseq * LB, 3 * WIDTH), F32)],
        input_output_aliases=aliases,
        compiler_params=_params("arbitrary", "arbitrary"),
        name="gdn",
    )(*args)


def _merge_kernel(ya_ref, yb_ref, yc_ref, ga_ref, gb_ref, gc_ref, wb_ref, o_ref, wbf_ref):
    @pl.when(pl.program_id(1) == 0)
    def _():
        wbf_ref[...] = wb_ref[...].astype(BF16)

    acc = None
    for jdx, (y_ref, g_ref) in enumerate(((ya_ref, ga_ref), (yb_ref, gb_ref), (yc_ref, gc_ref))):
        br = jnp.dot(y_ref[...].astype(BF16), wbf_ref[jdx], preferred_element_type=F32)
        term = g_ref[...].astype(F32) * br
        acc = term if acc is None else acc + term
    o_ref[...] = acc.astype(o_ref.dtype)


def _merge(ys, gate, w_branch, layer, tm=256, tn=1024):
    T, W = ys[0].shape
    D = w_branch.shape[-1]
    tn = min(tn, D)
    nb = D // tn
    yspec = pl.BlockSpec((tm, W), lambda n, m: (m, 0))
    gspec = lambda jdx: pl.BlockSpec((tm, tn), lambda n, m: (m, jdx * nb + n))
    return pl.pallas_call(
        _merge_kernel,
        grid=(nb, T // tm),
        in_specs=[yspec, yspec, yspec, gspec(0), gspec(1), gspec(2),
                  pl.BlockSpec((None, 3, W, tn), lambda n, m: (layer, 0, 0, n))],
        out_specs=pl.BlockSpec((tm, tn), lambda n, m: (m, n)),
        out_shape=jax.ShapeDtypeStruct((T, D), BF16),
        scratch_shapes=[pltpu.VMEM((3, W, tn), BF16)],
        compiler_params=_params("arbitrary", "arbitrary"),
        name="merge",
    )(ys[0], ys[1], ys[2], gate, gate, gate, w_branch)


def _moe_kernel(it_tile_ref, it_slice_ref, it_flag_ref, te_ref, nv_ref, tok_ref, dst_ref,
                h_hbm, roww_ref, wg_ref, wu_ref, wd_ref, out_hbm,
                xbuf, xb_ref, acc_ref, wgb_ref, wub_ref, wdb_ref, gsem, ssem, *, tm, n_real):
    i = pl.program_id(0)
    r = it_tile_ref[i]
    s = it_slice_ref[i]
    flag = it_flag_ref[i]
    nv = nv_ref[0]
    dk = wg_ref.shape[0]
    fk = wd_ref.shape[0]
    scatter_rows = min(tm, 128)

    def gather_start(tile):
        base = tile * tm

        def body(i, carry):
            tok = tok_ref[base + i]
            pltpu.make_async_copy(h_hbm.at[pl.ds(tok, 1)], xbuf.at[pl.ds(i, 1)], gsem).start()
            return carry

        lax.fori_loop(0, tm, body, 0, unroll=8)

    def gather_wait():
        pltpu.make_async_copy(h_hbm.at[pl.ds(0, tm)], xbuf, gsem).wait()

    def scatter_wait():
        pltpu.make_async_copy(acc_ref, out_hbm.at[pl.ds(0, tm)], ssem).wait()

    @pl.when(i == 0)
    def _():
        acc_ref[...] = jnp.zeros(acc_ref.shape, F32)
        for half in range(2):
            fill = pltpu.make_async_copy(acc_ref, out_hbm.at[pl.ds(n_real + half * tm, tm)], ssem)
            fill.start()
            fill.wait()
        gather_start(0)

    @pl.when((flag & 1) != 0)
    def _():
        k0 = pl.multiple_of(s * dk, dk)
        wgb_ref[pl.ds(k0, dk), :] = wg_ref[...].astype(BF16)
        wub_ref[pl.ds(k0, dk), :] = wu_ref[...].astype(BF16)
        f0 = pl.multiple_of(s * fk, fk)
        wdb_ref[pl.ds(f0, fk), :] = wd_ref[...].astype(BF16)

    @pl.when((flag & 2) != 0)
    def _():
        gather_wait()
        xb_ref[...] = xbuf[...].astype(BF16)
        nxt = jnp.minimum(r + 1, nv - 1)
        for j in range(tm):
            tok = tok_ref[nxt * tm + j]
            pltpu.make_async_copy(h_hbm.at[pl.ds(tok, 1)], xbuf.at[pl.ds(j, 1)], gsem).start()

        x = xb_ref[...]
        g = jnp.dot(x, wgb_ref[...], preferred_element_type=F32)
        u = jnp.dot(x, wub_ref[...], preferred_element_type=F32)
        he = (_silu(g) * u).astype(BF16)

        @pl.when(r > 0)
        def _():
            scatter_wait()

        for c in range(tm // scatter_rows):
            rows = slice(c * scatter_rows, (c + 1) * scatter_rows)
            y = jnp.dot(he[rows], wdb_ref[...], preferred_element_type=F32)
            acc_ref[rows, :] = y * roww_ref[rows, 0:1]
            for j in range(scatter_rows):
                row = c * scatter_rows + j
                dst = dst_ref[r * tm + row]
                pltpu.make_async_copy(acc_ref.at[pl.ds(row, 1)], out_hbm.at[pl.ds(dst, 1)], ssem).start()

        @pl.when(r == nv - 1)
        def _():
            scatter_wait()
            gather_wait()


def _moe_plan(rinfo, T, tm, ns):
    ids = rinfo[:, 0:TOP_K].astype(jnp.int32)
    wts = rinfo[:, TOP_K:2 * TOP_K]
    n_assign = TOP_K * T
    n_rows = n_assign + N_EXPERTS * tm
    n_tiles = n_rows // tm
    e_flat = ids.reshape(-1)
    onehot = (e_flat[:, None] == jnp.arange(N_EXPERTS, dtype=jnp.int32)[None, :]).astype(jnp.int32)
    rank = jnp.take_along_axis(jnp.cumsum(onehot, axis=0), e_flat[:, None], axis=1)[:, 0] - 1
    counts = jnp.sum(onehot, axis=0)
    padded = ((counts + tm - 1) // tm) * tm
    ends = jnp.cumsum(padded)
    starts = ends - padded
    dest = starts[e_flat] + rank
    upd = jnp.stack([jnp.arange(1, n_assign + 1, dtype=jnp.int32).astype(F32), wts.reshape(-1)], axis=1)
    rows = jnp.zeros((n_rows, 2), F32).at[dest].set(upd)
    row_a = rows[:, 0].astype(jnp.int32) - 1
    row_w = rows[:, 1]
    is_real = row_a >= 0
    tok = jnp.maximum(row_a, 0) // TOP_K
    slot = jnp.maximum(row_a, 0) % TOP_K
    p = jnp.arange(n_rows, dtype=jnp.int32)
    dump = n_assign + ((p // tm) % 2) * tm + (p % tm)
    row_tok = jnp.where(is_real, tok, 0)
    row_dst = jnp.where(is_real, slot * T + tok, dump)
    n_valid = (ends[-1] // tm).astype(jnp.int32)
    tile_start = jnp.arange(n_tiles, dtype=jnp.int32) * tm
    count_le = lambda sorted_ends, x: jnp.sum(sorted_ends[None, :] <= x[:, None], axis=1).astype(jnp.int32)
    tile_e = jnp.minimum(count_le(ends, tile_start), N_EXPERTS - 1)
    valid_t = tile_start < ends[-1]
    last_e = tile_e[jnp.maximum(n_valid - 1, 0)]
    tile_e = jnp.where(valid_t, tile_e, last_e)

    prev_e = jnp.concatenate([jnp.full((1,), -1, jnp.int32), tile_e[:-1]])
    first = valid_t & (tile_e != prev_e)
    cnt = jnp.where(valid_t, jnp.where(first, ns, 1), 0).astype(jnp.int32)
    item_end = jnp.cumsum(cnt)
    item_start = item_end - cnt
    n_items = n_tiles + N_EXPERTS * (ns - 1)
    it = jnp.arange(n_items, dtype=jnp.int32)
    live = it < item_end[-1]
    it_tile = jnp.minimum(count_le(item_end, it), n_tiles - 1)
    it_tile = jnp.where(live, it_tile, jnp.maximum(n_valid - 1, 0))
    k = it - item_start[it_tile]
    loads = live & first[it_tile]
    it_slice = jnp.where(loads, k, ns - 1).astype(jnp.int32)
    computes = live & (~first[it_tile] | (k == ns - 1))
    it_flag = loads.astype(jnp.int32) + 2 * computes.astype(jnp.int32)
    return (it_tile, it_slice, it_flag, tile_e, n_valid.reshape(1), row_tok, row_dst,
            jnp.broadcast_to(row_w[:, None], (n_rows, LANES)))


def _moe(h2, rinfo, wg, wu, wd, layer, tm=MOE_TM, ns=MOE_SLICES):
    T, D = h2.shape
    FF = wg.shape[-1]
    assert D % ns == 0 and FF % ns == 0
    dk, fk = D // ns, FF // ns
    it_tile, it_slice, it_flag, tile_e, n_valid, row_tok, row_dst, row_w = _moe_plan(rinfo, T, tm, ns)
    w_idx = lambda i, itt, its, itf, te, nv, tok, dst: (layer, te[itt[i]], its[i], 0)
    grid_spec = pltpu.PrefetchScalarGridSpec(
        num_scalar_prefetch=7,
        grid=(it_tile.shape[0],),
        in_specs=[pl.BlockSpec(memory_space=pl.ANY),
                  pl.BlockSpec((tm, LANES), lambda i, itt, its, itf, te, nv, tok, dst: (itt[i], 0)),
                  pl.BlockSpec((None, None, dk, FF), w_idx),
                  pl.BlockSpec((None, None, dk, FF), w_idx),
                  pl.BlockSpec((None, None, fk, D), w_idx)],
        out_specs=pl.BlockSpec(memory_space=pl.ANY),
        scratch_shapes=[pltpu.VMEM((tm, D), F32), pltpu.VMEM((tm, D), BF16), pltpu.VMEM((tm, D), F32),
                        pltpu.VMEM((D, FF), BF16), pltpu.VMEM((D, FF), BF16), pltpu.VMEM((FF, D), BF16),
                        pltpu.SemaphoreType.DMA(()), pltpu.SemaphoreType.DMA(())],
    )
    return pl.pallas_call(
        functools.partial(_moe_kernel, tm=tm, n_real=TOP_K * T),
        grid_spec=grid_spec,
        out_shape=jax.ShapeDtypeStruct((TOP_K * T + 2 * tm, D), F32),
        compiler_params=_params("arbitrary"),
        name="moe_routed",
    )(it_tile, it_slice, it_flag, tile_e, n_valid, row_tok, row_dst, h2, row_w, wg, wu, wd)


def kernel(x_prompt, x_sample, state_hgrn, state_mlstm_C, state_mlstm_n, state_mlstm_m, state_gdn, state_conv, c_prompt, c_sample, w_in, b_in, hgrn_lb_logits, hgrn_norm_g, gdn_conv_w, gdn_A_log, gdn_dt_bias, gdn_norm_g, w_branch, w_merge, b_merge, w_out, ln_mix_g, ln_mix_b, w_ada, b_ada, w_router, b_router, w_exp_gate, w_exp_up, w_exp_down, ln_ffn_g, ln_ffn_b):
    depth = w_in.shape[0]
    Bp, Lp, D = x_prompt.shape
    Bs, Ls, _ = x_sample.shape
    Tp, Ts = Bp * Lp, Bs * Ls
    T = Tp + Ts
    alpha = (2 * depth) ** 0.25

    def tok_specs(tb):
        assert Lp % tb == 0 and tb % Ls == 0 and Ts % tb == 0
        npb, spb = Tp // tb, tb // Ls
        mp, ms = _mod_specs(npb, Bs, Lp // tb, Bp, spb, D)
        xp_s, xs_s = _group_specs(npb, tb, D)
        return dict(npb=npb, spb=spb, mp=mp, ms=ms, xp=xp_s, xs=xs_s, grid=(T // tb,),
                    tok=pl.BlockSpec((tb, D), lambda i: (i, 0)),
                    tok_sm=pl.BlockSpec((tb, D), lambda i: (
                        jnp.where(i < npb, _step_major_block(jnp.minimum(i, npb - 1), tb, Bp, Lp), i), 0)),
                    tok_hi=pl.BlockSpec((tb, D), lambda i: (T // tb + i, 0)),
                    lanes=pl.BlockSpec((tb, LANES), lambda i: (i, 0)))

    ts = tok_specs(TOK_BLOCK)
    tr = tok_specs(ROUTE_BLOCK)
    n_ab = 8 * WIDTH
    g0, c0 = n_ab, n_ab + 2 * N_HEADS
    c1 = c0 + 4 * WIDTH

    nc = Bp + Bs
    nc_pad = -(-nc // 16) * 16
    c_all = jnp.pad(jnp.concatenate([c_sample, c_prompt], axis=0), ((0, nc_pad - nc), (0, 0)))
    c_act = (c_all * jax.nn.sigmoid(c_all)).astype(BF16)

    wr = jnp.pad(w_router.astype(F32), ((0, 0), (0, LANES - N_EXPERTS)))
    br = jnp.pad(b_router.astype(F32), ((0, LANES - N_EXPERTS),)).reshape(1, LANES)
    vec_spec = pl.BlockSpec((None, 1, D), lambda i: (0, 0, 0))
    vec = lambda a, l: a[l].reshape(1, 1, D)
    zero_bias = jnp.zeros((1, D), F32)

    w_in_t = jnp.swapaxes(w_in, 1, 2)
    tn = min(MM_TN, D)
    tm = MM_TM if T % MM_TM == 0 else TOK_BLOCK
    xp, xs = x_prompt.reshape(Tp, D), x_sample.reshape(Ts, D)
    x_shapes = [jax.ShapeDtypeStruct((Tp, D), F32), jax.ShapeDtypeStruct((Ts, D), F32)]
    def state_bufs(B):
        z = lambda *s: jnp.zeros((depth, B) + s, F32)
        return (z(*_STATE_DIMS), [z(*_STATE_DIMS), z(*_VEC_DIMS), z(*_VEC_DIMS)],
                [z(*_STATE_DIMS), z(CONV_W - 1, 3 * WIDTH)])

    hg_p, ml_p, gd_p = state_bufs(Bp)
    hg_s, ml_s, gd_s = state_bufs(Bs)
    mlstm_n_in = state_mlstm_n.reshape(depth, Bs, N_HEADS, 1, HEAD_DIM)
    mlstm_m_in = jnp.broadcast_to(state_mlstm_m[..., None, None], (depth, Bs, N_HEADS, 1, HEAD_DIM))
    for l in range(depth):
        mod = _matmul(c_act, w_ada, (None, D, 1024), lambda n: (l, 0, n),
                      b_ada.reshape(depth, 1, 6 * D), (None, 1, 1024), lambda n: (l, 0, n),
                      6 * D, F32, tm=nc_pad, tn=1024, name="ada").reshape(nc_pad, 6, D)

        h = pl.pallas_call(
            functools.partial(_modulate_kernel, npb=ts["npb"], spb=ts["spb"], sh_idx=0, sc_idx=1),
            grid=ts["grid"],
            in_specs=[ts["xp"], ts["xs"], ts["mp"], ts["ms"]],
            out_specs=ts["tok_sm"],
            out_shape=jax.ShapeDtypeStruct((T, D), BF16),
            compiler_params=_params("arbitrary"),
            name="modulate",
        )(xp, xs, mod, mod)

        b_in3 = b_in.reshape(depth, 1, -1)
        proj_ab = _matmul(h, w_in_t, (None, tn, D), lambda n: (l, n, 0),
                          b_in3, (None, 1, tn), lambda n: (l, 0, n),
                          n_ab, F32, tm=tm, tn=tn, name="proj_ab", w_transposed=True)
        proj_c = _matmul(h, w_in_t[l, c0:c1, :], (tn, D), lambda n: (n, 0),
                         b_in[l, c0:c1].reshape(1, -1), (1, tn), lambda n: (0, n),
                         4 * WIDTH, F32, tm=tm, tn=tn, name="proj_c", w_transposed=True)
        w_g = jnp.pad(jnp.concatenate([w_in_t[l, g0:c0, :], w_in_t[l, c1:, :]], axis=0),
                      ((0, LANES - 4 * N_HEADS), (0, 0)))
        b_g = jnp.pad(jnp.concatenate([b_in[l, g0:c0], b_in[l, c1:]]), ((0, LANES - 4 * N_HEADS),))
        gates = _matmul(h, w_g, (LANES, D), lambda n: (n, 0), b_g.reshape(1, LANES), (1, LANES),
                        lambda n: (0, n), LANES, F32, tm=tm, tn=LANES, name="proj_gates", w_transposed=True)
        gate = _matmul(h, w_merge, (None, None, D, tn), lambda n: (l, n // (D // tn), 0, n % (D // tn)),
                       b_merge.reshape(depth, 3, 1, D), (None, None, 1, tn),
                       lambda n: (l, n // (D // tn), 0, n % (D // tn)),
                       3 * D, BF16, tm=tm, tn=tn, name="merge_gates", act="sigmoid")

        conv_w_t = gdn_conv_w[l].T.astype(F32)
        y_init = jnp.zeros((T, WIDTH), F32)
        ya, hg_p = _hgrn(proj_ab, 0, 0, Bp, Lp, hgrn_lb_logits, hgrn_norm_g[l], l, None, (y_init, hg_p))
        ya, hg_s = _hgrn(proj_ab, 0, Tp, Bs, Ls, hgrn_lb_logits, hgrn_norm_g[l], l, state_hgrn, (ya, hg_s))
        yb, *ml_p = _mlstm(proj_ab, 4, gates, 0, Bp, Lp, l, None, (y_init, *ml_p))
        yb, *ml_s = _mlstm(proj_ab, 4, gates, Tp, Bs, Ls, l, (state_mlstm_C, mlstm_n_in, mlstm_m_in),
                           (yb, *ml_s))
        yc, *gd_p = _gdn(proj_c, 0, gates, 0, Bp, Lp, conv_w_t, gdn_A_log[l], gdn_dt_bias[l],
                         gdn_norm_g[l], l, None, (y_init, *gd_p))
        yc, *gd_s = _gdn(proj_c, 0, gates, Tp, Bs, Ls, conv_w_t, gdn_A_log[l], gdn_dt_bias[l],
                         gdn_norm_g[l], l, (state_gdn, state_conv), (yc, *gd_s))

        merged = _merge((ya, yb, yc), gate, w_branch, l)
        mix = _matmul(merged, w_out, (None, D, tn), lambda n: (l, 0, n),
                      zero_bias, (1, tn), lambda n: (0, n), D, F32, tm=tm, tn=tn, name="out_proj")

        xp, xs, h2, rinfo = pl.pallas_call(
            functools.partial(_post_route_kernel, npb=tr["npb"], spb=tr["spb"], alpha=alpha),
            grid=tr["grid"],
            in_specs=[tr["xp"], tr["xs"], tr["tok_sm"], tr["mp"], tr["ms"], vec_spec, vec_spec,
                      pl.BlockSpec((D, LANES), lambda i: (0, 0)), pl.BlockSpec((1, LANES), lambda i: (0, 0))],
            out_specs=[tr["xp"], tr["xs"], tr["tok"], tr["lanes"]],
            out_shape=[x_shapes[0], x_shapes[1], jax.ShapeDtypeStruct((T, D), F32),
                       jax.ShapeDtypeStruct((T, LANES), F32)],
            compiler_params=_params("arbitrary"),
            name="post_route",
        )(xp, xs, mix, mod, mod, vec(ln_mix_g, l), vec(ln_mix_b, l), wr, br)

        ffn = _moe(h2, rinfo, w_exp_gate, w_exp_up, w_exp_down, l)
        xp, xs = pl.pallas_call(
            functools.partial(_post_final_kernel, npb=tr["npb"], spb=tr["spb"], alpha=alpha),
            grid=tr["grid"],
            in_specs=[tr["xp"], tr["xs"], tr["tok"], tr["tok_hi"], tr["mp"], tr["ms"], vec_spec, vec_spec],
            out_specs=[tr["xp"], tr["xs"]],
            out_shape=x_shapes,
            compiler_params=_params("arbitrary"),
            name="post_final",
        )(xp, xs, ffn, ffn, mod, mod, vec(ln_ffn_g, l), vec(ln_ffn_b, l))

    as_vec = lambda n, B: n.reshape(depth, B, N_HEADS, HEAD_DIM)
    as_scalar = lambda m: m[:, :, :, 0, 0]
    return (xp.reshape(Bp, Lp, D), xs.reshape(Bs, Ls, D), hg_p, hg_s, ml_p[0], ml_s[0],
            as_vec(ml_p[1], Bp), as_vec(ml_s[1], Bs), as_scalar(ml_p[2]), as_scalar(ml_s[2]),
            gd_p[0], gd_s[0], gd_p[1], gd_s[1])
```

```python
import functools
import math

import numpy as np
import jax
import jax.numpy as jnp
from jax import lax
from jax.experimental import pallas as pl
from jax.experimental.pallas import tpu as pltpu

F32 = jnp.float32
BF16 = jnp.bfloat16

HEAD_DIM = 128
N_HEADS = 8
WIDTH = N_HEADS * HEAD_DIM
CHUNK = 64
CONV_W = 4
N_EXPERTS = 16
N_GROUPS = 4
EXPERTS_PER_GROUP = N_EXPERTS // N_GROUPS
TOP_K = 2
LN_EPS = 1e-5
NORM_EPS = 1e-6
LB_TINY = 1e-20
NEG_BIG = -1e30
LANES = 128
VMEM_LIMIT = 56 * 1024 * 1024
TOK_BLOCK = 256
ROUTE_BLOCK = 128
MM_TM = 1024
MM_TN = 512
SEQ_ROWS_PER_STEP = 256
SHORT_SEQS_PER_STEP = 8
LONG_SEQS_PER_STEP = 2
MOE_TM = 256
MOE_SLICES = 8
HEADS = tuple(slice(h * HEAD_DIM, (h + 1) * HEAD_DIM) for h in range(N_HEADS))


def _params(*sem):
    return pltpu.CompilerParams(dimension_semantics=sem, vmem_limit_bytes=VMEM_LIMIT)


def _dot(a, b):
    return jnp.dot(a.astype(BF16), b.astype(BF16), preferred_element_type=F32)


def _dot_nt(a, b):
    return lax.dot_general(a.astype(BF16), b.astype(BF16), (((1,), (1,)), ((), ())),
                           preferred_element_type=F32)


def _dot_tn(a, b):
    return lax.dot_general(a.astype(BF16), b.astype(BF16), (((0,), (0,)), ((), ())),
                           preferred_element_type=F32)


def _split2(x):
    x1 = x.astype(BF16)
    return x1, (x - x1.astype(F32)).astype(BF16)


def _dot_exact_l(a_bf, x):
    x1, x2 = _split2(x)
    d = lambda p: jnp.dot(a_bf, p, preferred_element_type=F32)
    return d(x1) + d(x2)


def _dot_exact_r(x, b_bf):
    x1, x2 = _split2(x)
    d = lambda p: jnp.dot(p, b_bf, preferred_element_type=F32)
    return d(x1) + d(x2)


def _dot3(a, b):
    a1, a2 = _split2(a)
    b1, b2 = _split2(b)
    d = lambda p, q: jnp.dot(p, q, preferred_element_type=F32)
    return d(a1, b1) + (d(a1, b2) + d(a2, b1))


def _sigmoid(x):
    return jax.nn.sigmoid(x)


def _silu(x):
    return x * jax.nn.sigmoid(x)


def _log_sigmoid(x):
    return jnp.minimum(x, 0.0) - jnp.log1p(jnp.exp(-jnp.abs(x)))


def _softplus(x):
    return jnp.maximum(x, 0.0) + jnp.log1p(jnp.exp(-jnp.abs(x)))


def _logaddexp(a, b):
    return jnp.maximum(a, b) + jnp.log1p(jnp.exp(-jnp.abs(a - b)))


def _tri_masks(cn):
    r = lax.broadcasted_iota(jnp.int32, (cn, cn), 0)
    c = lax.broadcasted_iota(jnp.int32, (cn, cn), 1)
    return r >= c, r > c, r == c


def _rms_gate(o, ng, gate):
    return o * lax.rsqrt(jnp.mean(o * o, axis=-1, keepdims=True) + NORM_EPS) * ng * gate


def _mm_kernel(x_ref, w_ref, b_ref, o_ref, wbf_ref, *, act, w_transposed):
    @pl.when(pl.program_id(1) == 0)
    def _():
        if w_transposed:
            tn = w_ref.shape[0]
            step = min(tn, 256)
            for c in range(0, tn, step):
                wbf_ref[:, c:c + step] = w_ref[c:c + step, :].T.astype(BF16)
        else:
            wbf_ref[...] = w_ref[...].astype(BF16)

    acc = jnp.dot(x_ref[...], wbf_ref[...], preferred_element_type=F32) + b_ref[...]
    if act == "sigmoid":
        acc = _sigmoid(acc)
    o_ref[...] = acc.astype(o_ref.dtype)


def _matmul(x, w, w_block, w_index, b, b_block, b_index, n_out, out_dtype, *, tm, tn, name, act=None,
            w_transposed=False):
    M, K = x.shape
    tm = min(tm, M)
    assert M % tm == 0 and n_out % tn == 0, (M, tm, n_out, tn)
    return pl.pallas_call(
        functools.partial(_mm_kernel, act=act, w_transposed=w_transposed),
        grid=(n_out // tn, M // tm),
        in_specs=[pl.BlockSpec((tm, K), lambda n, m: (m, 0)),
                  pl.BlockSpec(w_block, lambda n, m: w_index(n)),
                  pl.BlockSpec(b_block, lambda n, m: b_index(n))],
        out_specs=pl.BlockSpec((tm, tn), lambda n, m: (m, n)),
        out_shape=jax.ShapeDtypeStruct((M, n_out), out_dtype),
        scratch_shapes=[pltpu.VMEM((K, tn), BF16)],
        compiler_params=_params("arbitrary", "arbitrary"),
        name=name,
    )(x, w, b)


def _with_mod(npb, mp_ref, ms_ref, body):
    i = pl.program_id(0)

    @pl.when(i < npb)
    def _():
        body(mp_ref[...], 0)

    @pl.when(i >= npb)
    def _():
        body(ms_ref[...], 1)


def _group_specs(npb, tb, D):
    xp = pl.BlockSpec((tb, D), lambda i: (jnp.minimum(i, npb - 1), 0))
    xs = pl.BlockSpec((tb, D), lambda i: (jnp.maximum(i - npb, 0), 0))
    return xp, xs


def _mod_specs(npb, n_sample_seq, blocks_per_prompt_seq, n_prompt_seq, spb, D):
    mp = pl.BlockSpec((1, 6, D), lambda i: (n_sample_seq + jnp.minimum(i // blocks_per_prompt_seq,
                                                                        n_prompt_seq - 1), 0, 0))
    ms = pl.BlockSpec((spb, 6, D), lambda i: (jnp.maximum(i - npb, 0), 0, 0))
    return mp, ms


def _modulate_kernel(xp_ref, xs_ref, mp_ref, ms_ref, o_ref, *, npb, spb, sh_idx, sc_idx):
    tb, D = xp_ref.shape

    def body(m, group):
        x = (xp_ref, xs_ref)[group][...].reshape(spb, tb // spb, D)
        h = x * (1.0 + m[:, sc_idx:sc_idx + 1, :]) + m[:, sh_idx:sh_idx + 1, :]
        o_ref[...] = h.reshape(tb, D).astype(o_ref.dtype)

    _with_mod(npb, mp_ref, ms_ref, body)


def _layer_norm(z, g, b):
    mu = jnp.mean(z, axis=-1, keepdims=True)
    zc = z - mu
    var = jnp.mean(zc * zc, axis=-1, keepdims=True)
    return zc * lax.rsqrt(var + LN_EPS) * g + b


def _route(logits):
    lane = lax.broadcasted_iota(jnp.int32, logits.shape, 1)
    valid = lane < N_EXPERTS
    lg = jnp.where(valid, logits, NEG_BIG)
    ex = jnp.where(valid, jnp.exp(lg - jnp.max(lg, axis=-1, keepdims=True)), 0.0)
    probs = ex / jnp.sum(ex, axis=-1, keepdims=True)
    big = LANES
    best = None
    for g in range(N_GROUPS):
        in_g = (lane >= g * EXPERTS_PER_GROUP) & (lane < (g + 1) * EXPERTS_PER_GROUP)
        m1 = jnp.max(jnp.where(in_g, probs, -1.0), axis=-1, keepdims=True)
        i1 = jnp.min(jnp.where(in_g & (probs == m1), lane, big), axis=-1, keepdims=True)
        rest = in_g & (lane != i1)
        m2 = jnp.max(jnp.where(rest, probs, -1.0), axis=-1, keepdims=True)
        i2 = jnp.min(jnp.where(rest & (probs == m2), lane, big), axis=-1, keepdims=True)
        cand = (m1 + m2, i1, i2, m1, m2)
        if best is None:
            best = cand
        else:
            better = cand[0] > best[0]
            best = tuple(jnp.where(better, c, o) for c, o in zip(cand, best))
    _, i1, i2, m1, m2 = best
    tot = m1 + m2
    return (jnp.where(lane == 0, i1.astype(F32), 0.0) + jnp.where(lane == 1, i2.astype(F32), 0.0)
            + jnp.where(lane == 2, m1 / tot, 0.0) + jnp.where(lane == 3, m2 / tot, 0.0))


def _post_route_kernel(xp_ref, xs_ref, y_ref, mp_ref, ms_ref, g_ref, b_ref, wr_ref, br_ref,
                       xop_ref, xos_ref, h_ref, r_ref, *, npb, spb, alpha):
    tb, D = xp_ref.shape

    def body(m, group):
        x = (xp_ref, xs_ref)[group][...].reshape(spb, tb // spb, D)
        y = y_ref[...].reshape(spb, tb // spb, D)
        xn = _layer_norm(alpha * x + m[:, 2:3, :] * y, g_ref[...], b_ref[...])
        (xop_ref, xos_ref)[group][...] = xn.reshape(tb, D)
        h2 = (xn * (1.0 + m[:, 4:5, :]) + m[:, 3:4, :]).reshape(tb, D)
        h_ref[...] = h2
        r_ref[...] = _route(_dot3(h2, wr_ref[...]) + br_ref[...])

    _with_mod(npb, mp_ref, ms_ref, body)


def _post_final_kernel(xp_ref, xs_ref, y0_ref, y1_ref, mp_ref, ms_ref, g_ref, b_ref,
                       xop_ref, xos_ref, *, npb, spb, alpha):
    tb, D = xp_ref.shape

    def body(m, group):
        x = (xp_ref, xs_ref)[group][...].reshape(spb, tb // spb, D)
        y = (y0_ref[...] + y1_ref[...]).reshape(spb, tb // spb, D)
        xn = _layer_norm(alpha * x + m[:, 5:6, :] * y, g_ref[...], b_ref[...])
        (xop_ref, xos_ref)[group][...] = xn.reshape(tb, D)

    _with_mod(npb, mp_ref, ms_ref, body)


def _seq_blocking(L):
    cn = math.gcd(L, CHUNK)
    lb = min(L, SEQ_ROWS_PER_STEP)
    assert L % lb == 0 and lb % cn == 0
    return cn, lb, L // lb


def _step_major_block(i, tb, B, L):
    _, lb, _ = _seq_blocking(L)
    per_seq, per_step = L // tb, lb // tb
    b, rem = i // per_seq, i % per_seq
    return ((rem // per_step) * B + b) * per_step + rem % per_step


def _run_chunks(chunk, nchunk):
    if nchunk == 1:
        chunk(0, 0)
    else:
        lax.fori_loop(0, nchunk, chunk, 0)


def _hgrn_tables(cn):
    nlev = int(math.log2(cn))
    assert 1 << nlev == cn
    r = np.arange(cn)
    mats = [np.tril(np.ones((cn, cn)))]
    masks = [np.eye(cn)]
    j = r[None, :]
    for lev in range(nlev):
        s = cn >> (lev + 1)
        blk = 2 * s
        pos = r % blk
        ref = r - pos + s - 1
        second = pos >= s
        m_second = (j > ref[:, None]) & (j <= r[:, None])
        m_first = (j > r[:, None]) & (j <= ref[:, None])
        mats.append(np.where(second[:, None], m_second, m_first))
        same = (r[:, None] // blk) == (r[None, :] // blk)
        masks.append(same & second[:, None] & (~second)[None, :])
    return (np.concatenate(mats, 0).astype(np.float32), np.stack(masks).astype(np.float32))


def _hgrn_kernel(*refs, layer, cn, nchunk, nseq, has_state, n_alias):
    refs = list(refs)
    lbl_ref, ng_ref, mall_ref, masks_ref, q_ref, f_ref, i_ref, g_ref = refs[:8]
    pos = 8
    s0_ref = None
    if has_state:
        s0_ref = refs[pos]
        pos += 1
    pos += n_alias
    y_ref, s_ref, st_ref = refs[pos:pos + 3]
    j = pl.program_id(1)
    nlev = masks_ref.shape[0] - 1

    seqs = range(nseq)
    units = [(sq, h) for sq in seqs for h in range(N_HEADS)]
    seq_rows = q_ref.shape[0] // nseq

    @pl.when(j == 0)
    def _():
        for n, (sq, h) in enumerate(units):
            if has_state:
                st_ref[n] = s0_ref[sq, h].T
            else:
                st_ref[n] = jnp.zeros((HEAD_DIM, HEAD_DIM), F32)

    logits = lbl_ref[...]
    e = jnp.exp(logits - jnp.max(logits, axis=0, keepdims=True))
    p = e / jnp.sum(e, axis=0, keepdims=True)
    lb = jnp.zeros((1, WIDTH), F32)
    for l in range(1, layer + 1):
        lb = lb + p[l:l + 1, :]
    log_lb = jnp.log(jnp.maximum(lb, LB_TINY))
    log1m_lb = jnp.log1p(-lb)
    one_m_lb = 1.0 - lb
    ng = ng_ref[...]
    mall = mall_ref[...]

    def chunk(c, carry):
        r0 = pl.multiple_of(c * cn, cn)
        rows = [pl.ds(r0 + sq * seq_rows, cn) for sq in seqs]
        q = [_silu(q_ref[rw, :]) for rw in rows]
        fr = [f_ref[rw, :] for rw in rows]
        v = [i_ref[rw, :].astype(BF16) for rw in rows]
        log_f = [_logaddexp(log_lb, log1m_lb + _log_sigmoid(x)) for x in fr]
        k = [one_m_lb * _sigmoid(-x) for x in fr]
        dall = [_dot_exact_l(mall, x) for x in log_f]
        b = [x[:cn] for x in dall]
        qb = [x.astype(BF16) for x in q]
        kb = [x.astype(BF16) for x in k]
        att = [masks_ref[0] * _dot_nt(qb[sq][:, HEADS[h]], kb[sq][:, HEADS[h]]) for sq, h in units]
        for lev in range(1, nlev + 1):
            ex = [jnp.exp(x[lev * cn:(lev + 1) * cn]) for x in dall]
            qe = [(x * e).astype(BF16) for x, e in zip(q, ex)]
            ke = [(x * e).astype(BF16) for x, e in zip(k, ex)]
            att = [a + masks_ref[lev] * _dot_nt(qe[sq][:, HEADS[h]], ke[sq][:, HEADS[h]])
                   for a, (sq, h) in zip(att, units)]
        st = [st_ref[n] for n in range(len(units))]
        qd = [(x * jnp.exp(bb)).astype(BF16) for x, bb in zip(q, b)]
        o = [_dot(a, v[sq][:, HEADS[h]]) + _dot_nt(qd[sq][:, HEADS[h]], s)
             for a, (sq, h), s in zip(att, units, st)]
        b_last = [bb[cn - 1:cn] for bb in b]
        k_dec = [(x * jnp.exp(bl - bb)).astype(BF16) for x, bl, bb in zip(k, b_last, b)]
        dec = [jnp.exp(bl) for bl in b_last]
        for n, (sq, h) in enumerate(units):
            cs = HEADS[h]
            st_ref[n] = st[n] * dec[sq][:, cs] + _dot_tn(v[sq][:, cs], k_dec[sq][:, cs])
        gate = [_sigmoid(g_ref[rw, :]) for rw in rows]
        for n, (sq, h) in enumerate(units):
            cs = HEADS[h]
            y_ref[rows[sq], cs] = _rms_gate(o[n], ng[:, cs], gate[sq][:, cs])
        return carry

    _run_chunks(chunk, nchunk)

    @pl.when(j == pl.num_programs(1) - 1)
    def _():
        for n, (sq, h) in enumerate(units):
            s_ref[sq, h] = st_ref[n].T


def _mixer_specs(T_total, row0, B, L):
    cn, LB, ncb = _seq_blocking(L)
    nseq = math.gcd(B, SHORT_SEQS_PER_STEP if LB <= CHUNK else LONG_SEQS_PER_STEP)
    rows = nseq * LB
    assert row0 % rows == 0
    rb0 = row0 // rows
    blk = lambda g, j: j * (B // nseq) + g
    col = lambda cidx: pl.BlockSpec((rows, WIDTH), lambda g, j: (rb0 + blk(g, j), cidx))
    yspec = pl.BlockSpec((rows, WIDTH), lambda g, j: (rb0 + blk(g, j), 0))
    gspec = pl.BlockSpec((rows, LANES), lambda g, j: (rb0 + blk(g, j), 0))
    gtspec = pl.BlockSpec((rows // cn, 4 * N_HEADS, cn), lambda g, j: (blk(g, j), 0, 0))
    return cn, LB, ncb, nseq, col, yspec, gspec, gtspec


def _state_spec(layer, trailing, nseq):
    return pl.BlockSpec((None, nseq) + trailing, lambda b, j: (layer, b) + (0,) * len(trailing))


_STATE_DIMS = (N_HEADS, HEAD_DIM, HEAD_DIM)
_VEC_DIMS = (N_HEADS, 1, HEAD_DIM)


def _full(a):
    return pl.BlockSpec(a.shape, lambda b, j: (0,) * a.ndim)


def _alias_outputs(args, in_specs, prevs):
    aliases = {}
    for k, prev in enumerate(prevs):
        args.append(prev)
        in_specs.append(pl.BlockSpec(memory_space=pl.ANY))
        aliases[len(args) - 1] = k
    return aliases


def _shapes(prevs):
    return [jax.ShapeDtypeStruct(p.shape, p.dtype) for p in prevs]


def _hgrn(proj, a_col0, row0, B, L, lb_logits, norm_g, layer, state, prevs):
    cn, LB, ncb, nseq, col, yspec, _, _ = _mixer_specs(proj.shape[0], row0, B, L)
    mall, masks = _hgrn_tables(cn)
    mall = jnp.asarray(mall, BF16)
    masks = jnp.asarray(masks)
    ng = jnp.tile(norm_g.reshape(1, HEAD_DIM).astype(F32), (1, N_HEADS))
    sspec = _state_spec(layer, _STATE_DIMS, nseq)
    args = [lb_logits, ng, mall, masks, proj, proj, proj, proj]
    in_specs = [_full(lb_logits), _full(ng), _full(mall), _full(masks)] + [col(a_col0 + t) for t in range(4)]
    if state is not None:
        args.append(state)
        in_specs.append(sspec)
    aliases = _alias_outputs(args, in_specs, prevs)
    return pl.pallas_call(
        functools.partial(_hgrn_kernel, layer=layer, cn=cn, nchunk=LB // cn, nseq=nseq,
                          has_state=state is not None, n_alias=len(prevs)),
        grid=(B // nseq, ncb),
        in_specs=in_specs,
        out_specs=[yspec, sspec],
        out_shape=_shapes(prevs),
        scratch_shapes=[pltpu.VMEM((nseq * N_HEADS, HEAD_DIM, HEAD_DIM), F32)],
        input_output_aliases=aliases,
        compiler_params=_params("arbitrary", "arbitrary"),
        name="hgrn2",
    )(*args)


def _head_cols(mat, base):
    return [mat[:, base + h:base + h + 1] for h in range(N_HEADS)]


def _head_rows(mat, base):
    return [mat[base + h:base + h + 1, :] for h in range(N_HEADS)]


def _mlstm_kernel(*refs, cn, nchunk, nseq, has_state, n_alias):
    refs = list(refs)
    gt_ref, gtt_ref, q_ref, k_ref, v_ref, o_ref = refs[:6]
    pos = 6
    if has_state:
        c0_ref, n0_ref, m0_ref = refs[pos:pos + 3]
        pos += 3
    pos += n_alias
    y_ref, c_ref, n_ref, m_ref = refs[pos:pos + 4]
    j = pl.program_id(1)

    @pl.when(j == 0)
    def _():
        if has_state:
            c_ref[...] = c0_ref[...]
            n_ref[...] = n0_ref[...]
            m_ref[...] = m0_ref[...]
        else:
            c_ref[...] = jnp.zeros(c_ref.shape, F32)
            n_ref[...] = jnp.zeros(n_ref.shape, F32)
            m_ref[...] = jnp.zeros(m_ref.shape, F32)

    incl, _, _ = _tri_masks(cn)
    tril = incl.astype(BF16)
    triu = jnp.logical_not(_tri_masks(cn)[1]).astype(BF16)

    seqs = range(nseq)
    units = [(sq, h) for sq in seqs for h in range(N_HEADS)]
    lb = q_ref.shape[0] // nseq

    def chunk(c, carry):
        r0 = pl.multiple_of(c * cn, cn)
        rows = [pl.ds(r0 + sq * lb, cn) for sq in seqs]
        gates = [gt_ref[rw, :] for rw in rows]
        gates_r = [gtt_ref[sq * nchunk + c] for sq in seqs]
        f_c = [_dot_exact_l(tril, _log_sigmoid(g)) for g in gates]
        f_r = [_dot_exact_r(_log_sigmoid(g), triu) for g in gates_r]
        Fc = [x for sq in seqs for x in _head_cols(f_c[sq], N_HEADS)]
        ic = [x for sq in seqs for x in _head_cols(gates[sq], 0)]
        Fr = [x for sq in seqs for x in _head_rows(f_r[sq], N_HEADS)]
        ir = [x for sq in seqs for x in _head_rows(gates_r[sq], 0)]
        q = [q_ref[rows[sq], HEADS[h]].astype(BF16) for sq, h in units]
        k = [k_ref[rows[sq], HEADS[h]] * (HEAD_DIM ** -0.5) for sq, h in units]
        kb = [x.astype(BF16) for x in k]
        v = [v_ref[rows[sq], HEADS[h]].astype(BF16) for sq, h in units]
        C = [c_ref[sq, h] for sq, h in units]
        n = [n_ref[sq, h] for sq, h in units]
        m = [m_ref[sq, h][:, 0:1] for sq, h in units]
        qk = [_dot_nt(a, b) for a, b in zip(q, kb)]
        qc = [_dot(a, b) for a, b in zip(q, C)]
        raw = [a - b + d for a, b, d in zip(Fc, Fr, ir)]
        inter = [a + b for a, b in zip(Fc, m)]
        m_t = [jnp.maximum(a, jnp.max(jnp.where(incl, r, NEG_BIG), axis=-1, keepdims=True))
               for a, r in zip(inter, raw)]
        w_inter = [jnp.exp(a - b) for a, b in zip(inter, m_t)]
        s = [jnp.where(incl, a * jnp.exp(jnp.where(incl, r - mt, 0.0)), 0.0)
             for a, r, mt in zip(qk, raw, m_t)]
        num = [_dot(a, b) + w * d for a, b, w, d in zip(s, v, w_inter, qc)]
        den = [jnp.sum(a, axis=-1, keepdims=True)
               + w * jnp.sum(q_ref[rows[sq], HEADS[h]] * nn, axis=-1, keepdims=True)
               for a, w, (sq, h), nn in zip(s, w_inter, units, n)]
        hh = [a / jnp.maximum(jnp.abs(d), jnp.exp(-mt)) for a, d, mt in zip(num, den, m_t)]
        for u, (sq, h) in enumerate(units):
            m_new = m_t[u][cn - 1:cn]
            w_c = jnp.exp(inter[u][cn - 1:cn] - m_new)
            w_s = jnp.exp(Fc[u][cn - 1:cn] - Fc[u] + ic[u] - m_new)
            ks = k[u] * w_s
            c_ref[sq, h] = w_c * C[u] + _dot_tn(ks, v[u])
            n_ref[sq, h] = w_c * n[u] + jnp.sum(ks, axis=0, keepdims=True)
            m_ref[sq, h] = jnp.broadcast_to(m_new, (1, HEAD_DIM))
            y_ref[rows[sq], HEADS[h]] = _sigmoid(o_ref[rows[sq], HEADS[h]]) * hh[u]
        return carry

    _run_chunks(chunk, nchunk)


def _gates_t(gates, row0, B, L, cn):
    g = gates[row0:row0 + B * L, :4 * N_HEADS]
    return g.reshape(B * L // cn, cn, 4 * N_HEADS).transpose(0, 2, 1)


def _mlstm(proj, b_col0, gates, row0, B, L, layer, state, prevs):
    cn, LB, ncb, nseq, col, yspec, gspec, gtspec = _mixer_specs(proj.shape[0], row0, B, L)
    cspec = _state_spec(layer, _STATE_DIMS, nseq)
    nspec = _state_spec(layer, _VEC_DIMS, nseq)
    gt = _gates_t(gates, row0, B, L, cn)
    args = [gates, gt, proj, proj, proj, proj]
    in_specs = [gspec, gtspec] + [col(b_col0 + t) for t in range(4)]
    if state is not None:
        args += list(state)
        in_specs += [cspec, nspec, nspec]
    aliases = _alias_outputs(args, in_specs, prevs)
    return pl.pallas_call(
        functools.partial(_mlstm_kernel, cn=cn, nchunk=LB // cn, nseq=nseq, has_state=state is not None,
                          n_alias=len(prevs)),
        grid=(B // nseq, ncb),
        in_specs=in_specs,
        out_specs=[yspec, cspec, nspec, nspec],
        out_shape=_shapes(prevs),
        input_output_aliases=aliases,
        compiler_params=_params("arbitrary", "arbitrary"),
        name="mlstm",
    )(*args)


_CONV_PAD = 8


def _gdn_kernel(*refs, cn, nchunk, nseq, has_state, n_alias):
    refs = list(refs)
    (gt_ref, gtt_ref, cw_ref, al_ref, dtb_ref, alc_ref, dtbc_ref, ng_ref,
     xq_ref, xk_ref, xv_ref, g_ref) = refs[:12]
    pos = 12
    if has_state:
        s0_ref, b0_ref = refs[pos:pos + 2]
        pos += 2
    pos += n_alias
    y_ref, s_ref, buf_ref, xp_ref, cv_ref = refs[pos:pos + 5]
    j = pl.program_id(1)
    lb = xq_ref.shape[0] // nseq
    hist = CONV_W - 1
    h0 = _CONV_PAD - hist
    seqs = range(nseq)
    units = [(sq, h) for sq in seqs for h in range(N_HEADS)]

    @pl.when(j == 0)
    def _():
        if has_state:
            s_ref[...] = s0_ref[...]
        else:
            s_ref[...] = jnp.zeros(s_ref.shape, F32)

    for sq in seqs:
        blk = slice(sq * lb, (sq + 1) * lb)

        @pl.when(j == 0)
        def _():
            xp_ref[h0:_CONV_PAD, :] = b0_ref[sq] if has_state else jnp.zeros((hist, 3 * WIDTH), F32)

        @pl.when(j > 0)
        def _():
            xp_ref[h0:_CONV_PAD, :] = buf_ref[sq]
        xp_ref[_CONV_PAD:_CONV_PAD + lb, 0:WIDTH] = xq_ref[blk, :]
        xp_ref[_CONV_PAD:_CONV_PAD + lb, WIDTH:2 * WIDTH] = xk_ref[blk, :]
        xp_ref[_CONV_PAD:_CONV_PAD + lb, 2 * WIDTH:3 * WIDTH] = xv_ref[blk, :]
        acc = xp_ref[h0:h0 + lb, :] * cw_ref[0:1, :]
        for t in range(1, CONV_W):
            acc = acc + xp_ref[h0 + t:h0 + t + lb, :] * cw_ref[t:t + 1, :]
        cv_ref[blk, :] = _silu(acc)
        buf_ref[sq] = xp_ref[_CONV_PAD + lb - hist:_CONV_PAD + lb, :]

    incl, strict, eye = _tri_masks(cn)
    tril = incl.astype(BF16)
    triu = jnp.logical_not(strict).astype(BF16)
    eye_f = eye.astype(F32)
    lane = lax.broadcasted_iota(jnp.int32, (1, LANES), 1)
    is_a = (lane >= 2 * N_HEADS) & (lane < 3 * N_HEADS)
    sub = lax.broadcasted_iota(jnp.int32, (4 * N_HEADS, 1), 0)
    is_a_r = (sub >= 2 * N_HEADS) & (sub < 3 * N_HEADS)
    neg_rate = -jnp.exp(al_ref[...])
    dtb = dtb_ref[...]
    neg_rate_r = -jnp.exp(alc_ref[...])
    dtb_r = dtbc_ref[...]
    ng = ng_ref[...]
    nsq = int(math.log2(cn))

    def chunk(c, carry):
        r0 = pl.multiple_of(c * cn, cn)
        rows = [pl.ds(r0 + sq * lb, cn) for sq in seqs]
        gam_c, gam_r, beta = [], [], []
        for sq in seqs:
            gates = gt_ref[rows[sq], :]
            gates_r = gtt_ref[sq * nchunk + c]
            g_c = jnp.where(is_a, neg_rate * _softplus(gates + dtb), 0.0)
            g_r = jnp.where(is_a_r, neg_rate_r * _softplus(gates_r + dtb_r), 0.0)
            gam_c += _head_cols(_dot_exact_l(tril, g_c), 2 * N_HEADS)
            gam_r += _head_rows(_dot_exact_r(g_r, triu), 2 * N_HEADS)
            beta += _head_cols(_sigmoid(gates), 3 * N_HEADS)
        q, k, v = [], [], []
        for sq, h in units:
            cs = HEADS[h]
            qr = cv_ref[rows[sq], cs]
            kr = cv_ref[rows[sq], slice(WIDTH + cs.start, WIDTH + cs.stop)]
            v.append(cv_ref[rows[sq], slice(2 * WIDTH + cs.start, 2 * WIDTH + cs.stop)])
            q.append(qr * (lax.rsqrt(jnp.sum(qr * qr, axis=-1, keepdims=True) + NORM_EPS) * (HEAD_DIM ** -0.5)))
            k.append(kr * lax.rsqrt(jnp.sum(kr * kr, axis=-1, keepdims=True) + NORM_EPS))
        kb = [x.astype(BF16) for x in k]
        S = [s_ref[sq, h] for sq, h in units]
        kk = [_dot_nt(a, a) for a in kb]
        qk = [_dot_nt(a, b) for a, b in zip(q, kb)]
        dec = [jnp.exp(jnp.where(incl, a - b, 0.0)) for a, b in zip(gam_c, gam_r)]
        pw = [jnp.where(strict, -(bt * a * d), 0.0) for bt, a, d in zip(beta, kk, dec)]
        t_inv = [eye_f + a for a in pw]
        for _ in range(nsq - 1):
            pw = [_dot3(a, a) for a in pw]
            t_inv = [t + _dot3(t, a) for t, a in zip(t_inv, pw)]
        eg = [jnp.exp(a) for a in gam_c]
        sol_v = [_dot3(t, bt * x) for t, bt, x in zip(t_inv, beta, v)]
        sol_k = [_dot3(t, (bt * e) * x) for t, bt, e, x in zip(t_inv, beta, eg, k)]
        u = [a - _dot(b, s) for a, b, s in zip(sol_v, sol_k, S)]
        o = [_dot(jnp.where(incl, a * d, 0.0), uu) + _dot(qq * e, s)
             for a, d, uu, qq, e, s in zip(qk, dec, u, q, eg, S)]
        for n, (sq, h) in enumerate(units):
            g_last = gam_c[n][cn - 1:cn]
            s_ref[sq, h] = jnp.exp(g_last) * S[n] + _dot_tn(k[n] * jnp.exp(g_last - gam_c[n]), u[n])
            y_ref[rows[sq], HEADS[h]] = _rms_gate(o[n], ng, _silu(g_ref[rows[sq], HEADS[h]]))
        return carry

    _run_chunks(chunk, nchunk)


def _gdn(proj, c_col0, gates, row0, B, L, conv_w_t, a_log, dt_bias, norm_g, layer, state, prevs):
    cn, LB, ncb, nseq, col, yspec, gspec, gtspec = _mixer_specs(proj.shape[0], row0, B, L)
    hist = CONV_W - 1
    sspec = _state_spec(layer, _STATE_DIMS, nseq)
    bspec = _state_spec(layer, (hist, 3 * WIDTH), nseq)
    a0, a1 = 2 * N_HEADS, 3 * N_HEADS
    al = jnp.zeros((1, LANES), F32).at[0, a0:a1].set(a_log.astype(F32))
    db = jnp.zeros((1, LANES), F32).at[0, a0:a1].set(dt_bias.astype(F32))
    alc = al[0, :4 * N_HEADS].reshape(4 * N_HEADS, 1)
    dbc = db[0, :4 * N_HEADS].reshape(4 * N_HEADS, 1)
    ng = norm_g.reshape(1, HEAD_DIM).astype(F32)
    gt = _gates_t(gates, row0, B, L, cn)
    args = [gates, gt, conv_w_t, al, db, alc, dbc, ng, proj, proj, proj, proj]
    in_specs = [gspec, gtspec, _full(conv_w_t), _full(al), _full(db), _full(alc), _full(dbc), _full(ng)]
    in_specs += [col(c_col0 + t) for t in range(4)]
    if state is not None:
        args += [state[0], state[1]]
        in_specs += [sspec, bspec]
    aliases = _alias_outputs(args, in_specs, prevs)
    return pl.pallas_call(
        functools.partial(_gdn_kernel, cn=cn, nchunk=LB // cn, nseq=nseq, has_state=state is not None,
                          n_alias=len(prevs)),
        grid=(B // nseq, ncb),
        in_specs=in_specs,
        out_specs=[yspec, sspec, bspec],
        out_shape=_shapes(prevs),
        scratch_shapes=[pltpu.VMEM((_CONV_PAD + LB, 3 * WIDTH), F32),
                        pltpu.VMEM((nseq * LB, 3 * WIDTH), F32)],
        input_output_aliases=aliases,
        compiler_params=_params("arbitrary", "arbitrary"),
        name="gdn",
    )(*args)


def _merge_kernel(ya_ref, yb_ref, yc_ref, ga_ref, gb_ref, gc_ref, wb_ref, o_ref, wbf_ref):
    @pl.when(pl.program_id(1) == 0)
    def _():
        wbf_ref[...] = wb_ref[...].astype(BF16)

    acc = None
    for jdx, (y_ref, g_ref) in enumerate(((ya_ref, ga_ref), (yb_ref, gb_ref), (yc_ref, gc_ref))):
        br = jnp.dot(y_ref[...].astype(BF16), wbf_ref[jdx], preferred_element_type=F32)
        term = g_ref[...].astype(F32) * br
        acc = term if acc is None else acc + term
    o_ref[...] = acc.astype(o_ref.dtype)


def _merge(ys, gate, w_branch, layer, tm=256, tn=1024):
    T, W = ys[0].shape
    D = w_branch.shape[-1]
    tn = min(tn, D)
    nb = D // tn
    yspec = pl.BlockSpec((tm, W), lambda n, m: (m, 0))
    gspec = lambda jdx: pl.BlockSpec((tm, tn), lambda n, m: (m, jdx * nb + n))
    return pl.pallas_call(
        _merge_kernel,
        grid=(nb, T // tm),
        in_specs=[yspec, yspec, yspec, gspec(0), gspec(1), gspec(2),
                  pl.BlockSpec((None, 3, W, tn), lambda n, m: (layer, 0, 0, n))],
        out_specs=pl.BlockSpec((tm, tn), lambda n, m: (m, n)),
        out_shape=jax.ShapeDtypeStruct((T, D), BF16),
        scratch_shapes=[pltpu.VMEM((3, W, tn), BF16)],
        compiler_params=_params("arbitrary", "arbitrary"),
        name="merge",
    )(ys[0], ys[1], ys[2], gate, gate, gate, w_branch)


def _moe_kernel(it_tile_ref, it_slice_ref, it_flag_ref, te_ref, nv_ref, tok_ref, dst_ref,
                h_hbm, roww_ref, wg_ref, wu_ref, wd_ref, out_hbm,
                xbuf, xb_ref, acc_ref, wgb_ref, wub_ref, wdb_ref, gsem, ssem, *, tm, n_real):
    i = pl.program_id(0)
    r = it_tile_ref[i]
    s = it_slice_ref[i]
    flag = it_flag_ref[i]
    nv = nv_ref[0]
    dk = wg_ref.shape[0]
    fk = wd_ref.shape[0]
    scatter_rows = min(tm, 128)

    def gather_start(tile):
        base = tile * tm

        def body(i, carry):
            tok = tok_ref[base + i]
            pltpu.make_async_copy(h_hbm.at[pl.ds(tok, 1)], xbuf.at[pl.ds(i, 1)], gsem).start()
            return carry

        lax.fori_loop(0, tm, body, 0, unroll=8)

    def gather_wait():
        pltpu.make_async_copy(h_hbm.at[pl.ds(0, tm)], xbuf, gsem).wait()

    def scatter_wait():
        pltpu.make_async_copy(acc_ref, out_hbm.at[pl.ds(0, tm)], ssem).wait()

    @pl.when(i == 0)
    def _():
        acc_ref[...] = jnp.zeros(acc_ref.shape, F32)
        for half in range(2):
            fill = pltpu.make_async_copy(acc_ref, out_hbm.at[pl.ds(n_real + half * tm, tm)], ssem)
            fill.start()
            fill.wait()
        gather_start(0)

    @pl.when((flag & 1) != 0)
    def _():
        k0 = pl.multiple_of(s * dk, dk)
        wgb_ref[pl.ds(k0, dk), :] = wg_ref[...].astype(BF16)
        wub_ref[pl.ds(k0, dk), :] = wu_ref[...].astype(BF16)
        f0 = pl.multiple_of(s * fk, fk)
        wdb_ref[pl.ds(f0, fk), :] = wd_ref[...].astype(BF16)

    @pl.when((flag & 2) != 0)
    def _():
        gather_wait()
        xb_ref[...] = xbuf[...].astype(BF16)
        nxt = jnp.minimum(r + 1, nv - 1)
        for j in range(tm):
            tok = tok_ref[nxt * tm + j]
            pltpu.make_async_copy(h_hbm.at[pl.ds(tok, 1)], xbuf.at[pl.ds(j, 1)], gsem).start()

        x = xb_ref[...]
        g = jnp.dot(x, wgb_ref[...], preferred_element_type=F32)
        u = jnp.dot(x, wub_ref[...], preferred_element_type=F32)
        he = (_silu(g) * u).astype(BF16)

        @pl.when(r > 0)
        def _():
            scatter_wait()

        for c in range(tm // scatter_rows):
            rows = slice(c * scatter_rows, (c + 1) * scatter_rows)
            y = jnp.dot(he[rows], wdb_ref[...], preferred_element_type=F32)
            acc_ref[rows, :] = y * roww_ref[rows, 0:1]
            for j in range(scatter_rows):
                row = c * scatter_rows + j
                dst = dst_ref[r * tm + row]
                pltpu.make_async_copy(acc_ref.at[pl.ds(row, 1)], out_hbm.at[pl.ds(dst, 1)],
                                      ssem).start(priority=j % 2)

        @pl.when(r == nv - 1)
        def _():
            scatter_wait()
            gather_wait()


def _moe_plan(rinfo, T, tm, ns):
    ids = rinfo[:, 0:TOP_K].astype(jnp.int32)
    wts = rinfo[:, TOP_K:2 * TOP_K]
    n_assign = TOP_K * T
    n_rows = n_assign + N_EXPERTS * tm
    n_tiles = n_rows // tm
    e_flat = ids.reshape(-1)
    onehot = (e_flat[:, None] == jnp.arange(N_EXPERTS, dtype=jnp.int32)[None, :]).astype(jnp.int32)
    rank = jnp.take_along_axis(jnp.cumsum(onehot, axis=0), e_flat[:, None], axis=1)[:, 0] - 1
    counts = jnp.sum(onehot, axis=0)
    padded = ((counts + tm - 1) // tm) * tm
    ends = jnp.cumsum(padded)
    starts = ends - padded
    dest = starts[e_flat] + rank
    upd = jnp.stack([jnp.arange(1, n_assign + 1, dtype=jnp.int32).astype(F32), wts.reshape(-1)], axis=1)
    rows = jnp.zeros((n_rows, 2), F32).at[dest].set(upd)
    row_a = rows[:, 0].astype(jnp.int32) - 1
    row_w = rows[:, 1]
    is_real = row_a >= 0
    tok = jnp.maximum(row_a, 0) // TOP_K
    slot = jnp.maximum(row_a, 0) % TOP_K
    p = jnp.arange(n_rows, dtype=jnp.int32)
    dump = n_assign + ((p // tm) % 2) * tm + (p % tm)
    row_tok = jnp.where(is_real, tok, 0)
    row_dst = jnp.where(is_real, slot * T + tok, dump)
    n_valid = (ends[-1] // tm).astype(jnp.int32)
    tile_start = jnp.arange(n_tiles, dtype=jnp.int32) * tm
    count_le = lambda sorted_ends, x: jnp.sum(sorted_ends[None, :] <= x[:, None], axis=1).astype(jnp.int32)
    tile_e = jnp.minimum(count_le(ends, tile_start), N_EXPERTS - 1)
    valid_t = tile_start < ends[-1]
    last_e = tile_e[jnp.maximum(n_valid - 1, 0)]
    tile_e = jnp.where(valid_t, tile_e, last_e)

    prev_e = jnp.concatenate([jnp.full((1,), -1, jnp.int32), tile_e[:-1]])
    first = valid_t & (tile_e != prev_e)
    cnt = jnp.where(valid_t, jnp.where(first, ns, 1), 0).astype(jnp.int32)
    item_end = jnp.cumsum(cnt)
    item_start = item_end - cnt
    n_items = n_tiles + N_EXPERTS * (ns - 1)
    it = jnp.arange(n_items, dtype=jnp.int32)
    live = it < item_end[-1]
    it_tile = jnp.minimum(count_le(item_end, it), n_tiles - 1)
    it_tile = jnp.where(live, it_tile, jnp.maximum(n_valid - 1, 0))
    k = it - item_start[it_tile]
    loads = live & first[it_tile]
    it_slice = jnp.where(loads, k, ns - 1).astype(jnp.int32)
    computes = live & (~first[it_tile] | (k == ns - 1))
    it_flag = loads.astype(jnp.int32) + 2 * computes.astype(jnp.int32)
    return (it_tile, it_slice, it_flag, tile_e, n_valid.reshape(1), row_tok, row_dst,
            jnp.broadcast_to(row_w[:, None], (n_rows, LANES)))


def _moe(h2, rinfo, wg, wu, wd, layer, tm=MOE_TM, ns=MOE_SLICES):
    T, D = h2.shape
    FF = wg.shape[-1]
    assert D % ns == 0 and FF % ns == 0
    dk, fk = D // ns, FF // ns
    it_tile, it_slice, it_flag, tile_e, n_valid, row_tok, row_dst, row_w = _moe_plan(rinfo, T, tm, ns)
    w_idx = lambda i, itt, its, itf, te, nv, tok, dst: (layer, te[itt[i]], its[i], 0)
    grid_spec = pltpu.PrefetchScalarGridSpec(
        num_scalar_prefetch=7,
        grid=(it_tile.shape[0],),
        in_specs=[pl.BlockSpec(memory_space=pl.ANY),
                  pl.BlockSpec((tm, LANES), lambda i, itt, its, itf, te, nv, tok, dst: (itt[i], 0)),
                  pl.BlockSpec((None, None, dk, FF), w_idx),
                  pl.BlockSpec((None, None, dk, FF), w_idx),
                  pl.BlockSpec((None, None, fk, D), w_idx)],
        out_specs=pl.BlockSpec(memory_space=pl.ANY),
        scratch_shapes=[pltpu.VMEM((tm, D), F32), pltpu.VMEM((tm, D), BF16), pltpu.VMEM((tm, D), F32),
                        pltpu.VMEM((D, FF), BF16), pltpu.VMEM((D, FF), BF16), pltpu.VMEM((FF, D), BF16),
                        pltpu.SemaphoreType.DMA(()), pltpu.SemaphoreType.DMA(())],
    )
    return pl.pallas_call(
        functools.partial(_moe_kernel, tm=tm, n_real=TOP_K * T),
        grid_spec=grid_spec,
        out_shape=jax.ShapeDtypeStruct((TOP_K * T + 2 * tm, D), F32),
        compiler_params=_params("arbitrary"),
        name="moe_routed",
    )(it_tile, it_slice, it_flag, tile_e, n_valid, row_tok, row_dst, h2, row_w, wg, wu, wd)


def kernel(x_prompt, x_sample, state_hgrn, state_mlstm_C, state_mlstm_n, state_mlstm_m, state_gdn, state_conv, c_prompt, c_sample, w_in, b_in, hgrn_lb_logits, hgrn_norm_g, gdn_conv_w, gdn_A_log, gdn_dt_bias, gdn_norm_g, w_branch, w_merge, b_merge, w_out, ln_mix_g, ln_mix_b, w_ada, b_ada, w_router, b_router, w_exp_gate, w_exp_up, w_exp_down, ln_ffn_g, ln_ffn_b):
    depth = w_in.shape[0]
    Bp, Lp, D = x_prompt.shape
    Bs, Ls, _ = x_sample.shape
    Tp, Ts = Bp * Lp, Bs * Ls
    T = Tp + Ts
    alpha = (2 * depth) ** 0.25

    def tok_specs(tb):
        assert Lp % tb == 0 and tb % Ls == 0 and Ts % tb == 0
        npb, spb = Tp // tb, tb // Ls
        mp, ms = _mod_specs(npb, Bs, Lp // tb, Bp, spb, D)
        xp_s, xs_s = _group_specs(npb, tb, D)
        return dict(npb=npb, spb=spb, mp=mp, ms=ms, xp=xp_s, xs=xs_s, grid=(T // tb,),
                    tok=pl.BlockSpec((tb, D), lambda i: (i, 0)),
                    tok_sm=pl.BlockSpec((tb, D), lambda i: (
                        jnp.where(i < npb, _step_major_block(jnp.minimum(i, npb - 1), tb, Bp, Lp), i), 0)),
                    tok_hi=pl.BlockSpec((tb, D), lambda i: (T // tb + i, 0)),
                    lanes=pl.BlockSpec((tb, LANES), lambda i: (i, 0)))

    ts = tok_specs(TOK_BLOCK)
    tr = tok_specs(ROUTE_BLOCK)
    n_ab = 8 * WIDTH
    g0, c0 = n_ab, n_ab + 2 * N_HEADS
    c1 = c0 + 4 * WIDTH

    nc = Bp + Bs
    nc_pad = -(-nc // 16) * 16
    c_all = jnp.pad(jnp.concatenate([c_sample, c_prompt], axis=0), ((0, nc_pad - nc), (0, 0)))
    c_act = (c_all * jax.nn.sigmoid(c_all)).astype(BF16)

    wr = jnp.pad(w_router.astype(F32), ((0, 0), (0, LANES - N_EXPERTS)))
    br = jnp.pad(b_router.astype(F32), ((0, LANES - N_EXPERTS),)).reshape(1, LANES)
    vec_spec = pl.BlockSpec((None, 1, D), lambda i: (0, 0, 0))
    vec = lambda a, l: a[l].reshape(1, 1, D)
    zero_bias = jnp.zeros((1, D), F32)

    w_in_t = jnp.swapaxes(w_in, 1, 2)
    tn = min(MM_TN, D)
    tm = MM_TM if T % MM_TM == 0 else TOK_BLOCK
    xp, xs = x_prompt.reshape(Tp, D), x_sample.reshape(Ts, D)
    x_shapes = [jax.ShapeDtypeStruct((Tp, D), F32), jax.ShapeDtypeStruct((Ts, D), F32)]
    def state_bufs(B):
        z = lambda *s: jnp.zeros((depth, B) + s, F32)
        return (z(*_STATE_DIMS), [z(*_STATE_DIMS), z(*_VEC_DIMS), z(*_VEC_DIMS)],
                [z(*_STATE_DIMS), z(CONV_W - 1, 3 * WIDTH)])

    hg_p, ml_p, gd_p = state_bufs(Bp)
    hg_s, ml_s, gd_s = state_bufs(Bs)
    mlstm_n_in = state_mlstm_n.reshape(depth, Bs, N_HEADS, 1, HEAD_DIM)
    mlstm_m_in = jnp.broadcast_to(state_mlstm_m[..., None, None], (depth, Bs, N_HEADS, 1, HEAD_DIM))
    for l in range(depth):
        mod = _matmul(c_act, w_ada, (None, D, 1024), lambda n: (l, 0, n),
                      b_ada.reshape(depth, 1, 6 * D), (None, 1, 1024), lambda n: (l, 0, n),
                      6 * D, F32, tm=nc_pad, tn=1024, name="ada").reshape(nc_pad, 6, D)

        h = pl.pallas_call(
            functools.partial(_modulate_kernel, npb=ts["npb"], spb=ts["spb"], sh_idx=0, sc_idx=1),
            grid=ts["grid"],
            in_specs=[ts["xp"], ts["xs"], ts["mp"], ts["ms"]],
            out_specs=ts["tok_sm"],
            out_shape=jax.ShapeDtypeStruct((T, D), BF16),
            compiler_params=_params("arbitrary"),
            name="modulate",
        )(xp, xs, mod, mod)

        b_in3 = b_in.reshape(depth, 1, -1)
        proj_ab = _matmul(h, w_in_t, (None, tn, D), lambda n: (l, n, 0),
                          b_in3, (None, 1, tn), lambda n: (l, 0, n),
                          n_ab, F32, tm=tm, tn=tn, name="proj_ab", w_transposed=True)
        proj_c = _matmul(h, w_in_t[l, c0:c1, :], (tn, D), lambda n: (n, 0),
                         b_in[l, c0:c1].reshape(1, -1), (1, tn), lambda n: (0, n),
                         4 * WIDTH, F32, tm=tm, tn=tn, name="proj_c", w_transposed=True)
        w_g = jnp.pad(jnp.concatenate([w_in_t[l, g0:c0, :], w_in_t[l, c1:, :]], axis=0),
                      ((0, LANES - 4 * N_HEADS), (0, 0)))
        b_g = jnp.pad(jnp.concatenate([b_in[l, g0:c0], b_in[l, c1:]]), ((0, LANES - 4 * N_HEADS),))
        gates = _matmul(h, w_g, (LANES, D), lambda n: (n, 0), b_g.reshape(1, LANES), (1, LANES),
                        lambda n: (0, n), LANES, F32, tm=tm, tn=LANES, name="proj_gates", w_transposed=True)
        gate = _matmul(h, w_merge, (None, None, D, tn), lambda n: (l, n // (D // tn), 0, n % (D // tn)),
                       b_merge.reshape(depth, 3, 1, D), (None, None, 1, tn),
                       lambda n: (l, n // (D // tn), 0, n % (D // tn)),
                       3 * D, BF16, tm=tm, tn=tn, name="merge_gates", act="sigmoid")

        conv_w_t = gdn_conv_w[l].T.astype(F32)
        y_init = jnp.zeros((T, WIDTH), F32)
        ya, hg_p = _hgrn(proj_ab, 0, 0, Bp, Lp, hgrn_lb_logits, hgrn_norm_g[l], l, None, (y_init, hg_p))
        ya, hg_s = _hgrn(proj_ab, 0, Tp, Bs, Ls, hgrn_lb_logits, hgrn_norm_g[l], l, state_hgrn, (ya, hg_s))
        yb, *ml_p = _mlstm(proj_ab, 4, gates, 0, Bp, Lp, l, None, (y_init, *ml_p))
        yb, *ml_s = _mlstm(proj_ab, 4, gates, Tp, Bs, Ls, l, (state_mlstm_C, mlstm_n_in, mlstm_m_in),
                           (yb, *ml_s))
        yc, *gd_p = _gdn(proj_c, 0, gates, 0, Bp, Lp, conv_w_t, gdn_A_log[l], gdn_dt_bias[l],
                         gdn_norm_g[l], l, None, (y_init, *gd_p))
        yc, *gd_s = _gdn(proj_c, 0, gates, Tp, Bs, Ls, conv_w_t, gdn_A_log[l], gdn_dt_bias[l],
                         gdn_norm_g[l], l, (state_gdn, state_conv), (yc, *gd_s))

        merged = _merge((ya, yb, yc), gate, w_branch, l)
        mix = _matmul(merged, w_out, (None, D, tn), lambda n: (l, 0, n),
                      zero_bias, (1, tn), lambda n: (0, n), D, F32, tm=tm, tn=tn, name="out_proj")

        xp, xs, h2, rinfo = pl.pallas_call(
            functools.partial(_post_route_kernel, npb=tr["npb"], spb=tr["spb"], alpha=alpha),
            grid=tr["grid"],
            in_specs=[tr["xp"], tr["xs"], tr["tok_sm"], tr["mp"], tr["ms"], vec_spec, vec_spec,
                      pl.BlockSpec((D, LANES), lambda i: (0, 0)), pl.BlockSpec((1, LANES), lambda i: (0, 0))],
            out_specs=[tr["xp"], tr["xs"], tr["tok"], tr["lanes"]],
            out_shape=[x_shapes[0], x_shapes[1], jax.ShapeDtypeStruct((T, D), F32),
                       jax.ShapeDtypeStruct((T, LANES), F32)],
            compiler_params=_params("arbitrary"),
            name="post_route",
        )(xp, xs, mix, mod, mod, vec(ln_mix_g, l), vec(ln_mix_b, l), wr, br)

        ffn = _moe(h2, rinfo, w_exp_gate, w_exp_up, w_exp_down, l)
        xp, xs = pl.pallas_call(
            functools.partial(_post_final_kernel, npb=tr["npb"], spb=tr["spb"], alpha=alpha),
            grid=tr["grid"],
            in_specs=[tr["xp"], tr["xs"], tr["tok"], tr["tok_hi"], tr["mp"], tr["ms"], vec_spec, vec_spec],
            out_specs=[tr["xp"], tr["xs"]],
            out_shape=x_shapes,
            compiler_params=_params("arbitrary"),
            name="post_final",
        )(xp, xs, ffn, ffn, mod, mod, vec(ln_ffn_g, l), vec(ln_ffn_b, l))

    as_vec = lambda n, B: n.reshape(depth, B, N_HEADS, HEAD_DIM)
    as_scalar = lambda m: m[:, :, :, 0, 0]
    return (xp.reshape(Bp, Lp, D), xs.reshape(Bs, Ls, D), hg_p, hg_s, ml_p[0], ml_s[0],
            as_vec(ml_p[1], Bp), as_vec(ml_s[1], Bs), as_scalar(ml_p[2]), as_scalar(ml_s[2]),
            gd_p[0], gd_s[0], gd_p[1], gd_s[1])
```
